```python
import math
import jax, jax.numpy as jnp
from jax import lax
import numpy as np

D_MODEL = 1024
BATCH = 8
SEQ = 2048
DEPTH = 2
DEC_BATCH = 128
DEC_SEQ = 1
PAST_LEN = 16384
PAGE_SIZE = 128

N_EVEN = (DEPTH + 1) // 2
N_ODD = DEPTH // 2

S5_WIDTH = D_MODEL // 2
S5_GROUP = 16
S5_GROUPS = S5_WIDTH // S5_GROUP
S5_STATE = 64

MLA_HEADS = 16
MLA_NOPE = 64
MLA_ROPE = 32
MLA_V = 64
MLA_Q_LORA = 384
MLA_KV_LORA = 256
MLA_THETA = 10000.0
MLA_SCALE = 1.0 / math.sqrt(MLA_NOPE + MLA_ROPE)

SWA_HEADS = 16
SWA_KV_HEADS = 4
SWA_HD = 64
SWA_GQ = SWA_HEADS // SWA_KV_HEADS
WINDOW = 128
ROT_DIM = SWA_HD // 4
ROPE_THETA = 500000.0
SWA_SCALE = 1.0 / math.sqrt(SWA_HD)

D_FF = 2816
CONV_W = 3

EPS = 1e-6
NEG = -1e30
Q_BLOCK = 128

EVEN_IN = S5_WIDTH + MLA_Q_LORA + MLA_KV_LORA + MLA_ROPE
EVEN_MIX = S5_WIDTH + MLA_HEADS * MLA_V
ODD_IN = (SWA_HEADS + 2 * SWA_KV_HEADS) * SWA_HD
ODD_MIX = SWA_HEADS * SWA_HD

kernel_name = 'hybrid_s5_mla_swa_convffn_step'


def rmsnorm(x, g):
    xf = x.astype(jnp.float32)
    y = xf * lax.rsqrt(jnp.mean(xf * xf, axis=-1, keepdims=True) + EPS)
    return (y * g.astype(jnp.float32)).astype(x.dtype)


def rope(x, pos, theta):
    d = x.shape[-1]
    half = d // 2
    inv = theta ** (-jnp.arange(half, dtype=jnp.float32) * 2.0 / d)
    ang = pos.astype(jnp.float32)[:, None] * inv[None, :]
    shape = (ang.shape[0],) + (1,) * (x.ndim - 3) + (half,)
    cos = jnp.cos(ang).reshape(shape)
    sin = jnp.sin(ang).reshape(shape)
    xf = x.astype(jnp.float32)
    x1, x2 = xf[..., :half], xf[..., half:]
    return jnp.concatenate([x1 * cos - x2 * sin, x1 * sin + x2 * cos], axis=-1).astype(x.dtype)


def partial_rope(x, pos):
    return jnp.concatenate([rope(x[..., :ROT_DIM], pos, ROPE_THETA), x[..., ROT_DIM:]], axis=-1)


def sink_softmax(s, sink):
    full = jnp.concatenate([s, jnp.broadcast_to(sink, s.shape[:-1] + (1,))], axis=-1)
    return jax.nn.softmax(full, axis=-1)[..., :-1]


def _s5_combine(e1, e2):
    a1r, a1i, b1r, b1i = e1
    a2r, a2i, b2r, b2i = e2
    return (a2r * a1r - a2i * a1i,
            a2r * a1i + a2i * a1r,
            a2r * b1r - a2i * b1i + b2r,
            a2r * b1i + a2i * b1r + b2i)


def s5_mixer(u, h0_re, h0_im, lam_re, lam_im, log_dt, b_re, b_im, c_re, c_im, d_skip, w_glu, b_glu):
    f32 = jnp.float32
    n, t, _ = u.shape
    uf = u.astype(f32).reshape(n, t, S5_GROUPS, S5_GROUP)
    lr = jnp.minimum(lam_re.astype(f32), -1e-4)
    li = lam_im.astype(f32)
    dt = jnp.exp(log_dt.astype(f32))[:, None]
    mag = jnp.exp(lr * dt)
    ar = mag * jnp.cos(li * dt)
    ai = mag * jnp.sin(li * dt)
    den = lr * lr + li * li
    zr = ((ar - 1.0) * lr + ai * li) / den
    zi = (ai * lr - (ar - 1.0) * li) / den
    bu_re = jnp.einsum('ntgh,gph->ntgp', uf, b_re.astype(f32))
    bu_im = jnp.einsum('ntgh,gph->ntgp', uf, b_im.astype(f32))
    xr = zr * bu_re - zi * bu_im
    xi = zr * bu_im + zi * bu_re
    h0r = h0_re.astype(f32)
    h0i = h0_im.astype(f32)
    xr = xr.at[:, 0].add(ar * h0r - ai * h0i)
    xi = xi.at[:, 0].add(ar * h0i + ai * h0r)
    a_re = jnp.broadcast_to(ar, xr.shape)
    a_im = jnp.broadcast_to(ai, xi.shape)
    _, _, hr, hi = lax.associative_scan(_s5_combine, (a_re, a_im, xr, xi), axis=1)
    y = (jnp.einsum('ntgp,ghp->ntgh', hr, c_re.astype(f32))
         - jnp.einsum('ntgp,ghp->ntgh', hi, c_im.astype(f32))
         + d_skip.astype(f32).reshape(S5_GROUPS, S5_GROUP) * uf)
    y = jax.nn.gelu(y.reshape(n, t, S5_WIDTH))
    y = y * jax.nn.sigmoid(y @ w_glu.astype(f32) + b_glu.astype(f32))
    return y.astype(u.dtype), hr[:, -1], hi[:, -1]


def mla_project(xq, xkv, kr, pos, q_norm, w_uq, kv_norm):
    q = jnp.einsum('ntr,rhd->nthd', rmsnorm(xq, q_norm), w_uq)
    q_nope = q[..., :MLA_NOPE]
    q_rope = rope(q[..., MLA_NOPE:], pos, MLA_THETA)
    ckv = rmsnorm(xkv, kv_norm)
    k_rope = rope(kr, pos, MLA_THETA)
    return q_nope, q_rope, ckv, k_rope


def mla_prompt_attn(q_nope, q_rope, ckv, k_rope, w_uk, w_uv):
    n, t = q_nope.shape[:2]
    k_nope = jnp.einsum('nsc,chd->nshd', ckv, w_uk)
    v = jnp.einsum('nsc,chd->nshd', ckv, w_uv)
    nb = t // Q_BLOCK
    qn = q_nope.reshape(n, nb, Q_BLOCK, MLA_HEADS, MLA_NOPE).transpose(1, 0, 2, 3, 4)
    qr = q_rope.reshape(n, nb, Q_BLOCK, MLA_HEADS, MLA_ROPE).transpose(1, 0, 2, 3, 4)
    kpos = jnp.arange(t)

    def block(args):
        i, qn_b, qr_b = args
        s = (jnp.einsum('nqhd,nshd->nhqs', qn_b, k_nope)
             + jnp.einsum('nqhr,nsr->nhqs', qr_b, k_rope)).astype(jnp.float32) * MLA_SCALE
        qpos = i * Q_BLOCK + jnp.arange(Q_BLOCK)
        s = jnp.where(kpos[None, :] <= qpos[:, None], s, NEG)
        p = jax.nn.softmax(s, axis=-1).astype(v.dtype)
        return jnp.einsum('nhqs,nshd->nqhd', p, v)

    o = lax.map(block, (jnp.arange(nb), qn, qr))
    return o.transpose(1, 0, 2, 3, 4).reshape(n, t, MLA_HEADS * MLA_V)


def mla_sample_attn(q_nope, q_rope, ckv, k_rope, past_ckv, past_kr, w_uk, w_uv):
    n, t = q_nope.shape[:2]
    l = past_ckv.shape[1]
    q_lat = jnp.einsum('nthd,chd->nthc', q_nope, w_uk)
    s_past = (jnp.einsum('nthc,nlc->nhtl', q_lat, past_ckv)
              + jnp.einsum('nthr,nlr->nhtl', q_rope, past_kr)).astype(jnp.float32) * MLA_SCALE
    s_new = (jnp.einsum('nthc,nsc->nhts', q_lat, ckv)
             + jnp.einsum('nthr,nsr->nhts', q_rope, k_rope)).astype(jnp.float32) * MLA_SCALE
    causal = jnp.arange(t)[None, :] <= jnp.arange(t)[:, None]
    s_new = jnp.where(causal, s_new, NEG)
    p = jax.nn.softmax(jnp.concatenate([s_past, s_new], axis=-1), axis=-1).astype(ckv.dtype)
    o_lat = (jnp.einsum('nhtl,nlc->nthc', p[..., :l], past_ckv)
             + jnp.einsum('nhts,nsc->nthc', p[..., l:], ckv))
    return jnp.einsum('nthc,chd->nthd', o_lat, w_uv).reshape(n, t, MLA_HEADS * MLA_V)


def swa_prompt_attn(q, k, v, sinks):
    f32 = jnp.float32
    n, t = q.shape[:2]
    nb = t // WINDOW
    qb = q.reshape(n, nb, WINDOW, SWA_KV_HEADS, SWA_GQ, SWA_HD)
    kb = k.reshape(n, nb, WINDOW, SWA_KV_HEADS, SWA_HD)
    vb = v.reshape(n, nb, WINDOW, SWA_KV_HEADS, SWA_HD)
    pad = ((0, 0), (1, 0), (0, 0), (0, 0), (0, 0))
    kk = jnp.concatenate([jnp.pad(kb, pad)[:, :-1], kb], axis=2)
    vv = jnp.concatenate([jnp.pad(vb, pad)[:, :-1], vb], axis=2)
    s = jnp.einsum('nbqkgd,nbskd->nbkgqs', qb, kk).astype(f32) * SWA_SCALE
    qi = jnp.arange(WINDOW)[:, None] + WINDOW
    kj = jnp.arange(2 * WINDOW)[None, :]
    rel = qi - kj
    band = (rel >= 0) & (rel <= WINDOW)
    real = (jnp.arange(nb)[:, None, None] > 0) | (kj[None] >= WINDOW)
    mask = band[None] & real
    s = jnp.where(mask[None, :, None, None], s, NEG)
    sink = sinks.astype(f32).reshape(SWA_KV_HEADS, SWA_GQ)[:, :, None, None]
    p = sink_softmax(s, sink).astype(v.dtype)
    o = jnp.einsum('nbkgqs,nbskd->nbqkgd', p, vv)
    return o.reshape(n, t, SWA_HEADS * SWA_HD)


def swa_sample_attn(q, k, v, buf_k, buf_v, sinks):
    f32 = jnp.float32
    n, t = q.shape[:2]
    wb = buf_k.shape[1]
    kk = jnp.concatenate([buf_k.astype(k.dtype), k], axis=1)
    vv = jnp.concatenate([buf_v.astype(v.dtype), v], axis=1)
    qpos = PAST_LEN + jnp.arange(t)
    kpos = PAST_LEN - wb + jnp.arange(wb + t)
    rel = qpos[:, None] - kpos[None, :]
    mask = (rel >= 0) & (rel <= WINDOW)
    qg = q.reshape(n, t, SWA_KV_HEADS, SWA_GQ, SWA_HD)
    s = jnp.einsum('ntkgd,nskd->nkgts', qg, kk).astype(f32) * SWA_SCALE
    s = jnp.where(mask, s, NEG)
    sink = sinks.astype(f32).reshape(SWA_KV_HEADS, SWA_GQ)[:, :, None, None]
    p = sink_softmax(s, sink).astype(v.dtype)
    o = jnp.einsum('nkgts,nskd->ntkgd', p, vv)
    return o.reshape(n, t, SWA_HEADS * SWA_HD)


def conv_ffn(h, buf, w_gate, w_up, conv_w, conv_b, w_down):
    t = h.shape[1]
    g = h @ w_gate
    u = h @ w_up
    gp = jnp.concatenate([buf.astype(g.dtype), g], axis=1)
    c = conv_b
    for j in range(CONV_W):
        c = c + conv_w[j] * gp[:, j:j + t]
    out = (jax.nn.gelu(c) * u) @ w_down
    return out, gp[:, -(CONV_W - 1):]


def setup_inputs(seed: int = 0) -> dict:
    key = jax.random.key(seed)
    ks = iter(jax.random.split(key, 64))
    f32 = jnp.float32

    def nrm(shape, scale):
        return scale * jax.random.normal(next(ks), shape, f32)

    n_pages = PAST_LEN // PAGE_SIZE
    n_used = DEC_BATCH * n_pages
    n_pool = n_used + max(1, n_used // 4)
    perm = jax.random.permutation(next(ks), n_pool)[:n_used]
    page_table = perm.reshape(DEC_BATCH, n_pages).astype(jnp.int32)
    w_buf = min(WINDOW, PAST_LEN)

    lam_im = jnp.pi * jnp.arange(S5_STATE, dtype=f32)
    log_lo, log_hi = math.log(0.001), math.log(0.1)

    return {
        'x_prompt': nrm((BATCH, SEQ, D_MODEL), 1.0),
        'x_sample': nrm((DEC_BATCH, DEC_SEQ, D_MODEL), 1.0),
        'page_table': page_table,
        'state_s5_re': nrm((N_EVEN, DEC_BATCH, S5_GROUPS, S5_STATE), 0.5),
        'state_s5_im': nrm((N_EVEN, DEC_BATCH, S5_GROUPS, S5_STATE), 0.5),
        'cache_mla_ckv': nrm((N_EVEN, n_pool, PAGE_SIZE, MLA_KV_LORA), 1.0),
        'cache_mla_krope': nrm((N_EVEN, n_pool, PAGE_SIZE, MLA_ROPE), 1.0),
        'cache_swa_k': nrm((N_ODD, DEC_BATCH, w_buf, SWA_KV_HEADS, SWA_HD), 1.0),
        'cache_swa_v': nrm((N_ODD, DEC_BATCH, w_buf, SWA_KV_HEADS, SWA_HD), 1.0),
        'state_ffn_conv': nrm((DEPTH, DEC_BATCH, CONV_W - 1, D_FF), 1.0),
        'norm_mix': 1.0 + nrm((DEPTH, D_MODEL), 0.01),
        'norm_ffn': 1.0 + nrm((DEPTH, D_MODEL), 0.01),
        'final_norm': 1.0 + nrm((D_MODEL,), 0.01),
        'e_w_in': nrm((N_EVEN, D_MODEL, EVEN_IN), D_MODEL ** -0.5),
        'e_s5_lam_re': -0.5 + nrm((N_EVEN, S5_GROUPS, S5_STATE), 0.01),
        'e_s5_lam_im': lam_im + nrm((N_EVEN, S5_GROUPS, S5_STATE), 0.01),
        'e_s5_log_dt': log_lo + (log_hi - log_lo) * jax.random.uniform(next(ks), (N_EVEN, S5_GROUPS), f32),
        'e_s5_b_re': nrm((N_EVEN, S5_GROUPS, S5_STATE, S5_GROUP), (2 * S5_GROUP) ** -0.5),
        'e_s5_b_im': nrm((N_EVEN, S5_GROUPS, S5_STATE, S5_GROUP), (2 * S5_GROUP) ** -0.5),
        'e_s5_c_re': nrm((N_EVEN, S5_GROUPS, S5_GROUP, S5_STATE), S5_STATE ** -0.5),
        'e_s5_c_im': nrm((N_EVEN, S5_GROUPS, S5_GROUP, S5_STATE), S5_STATE ** -0.5),
        'e_s5_d': nrm((N_EVEN, S5_WIDTH), 1.0),
        'e_s5_w_glu': nrm((N_EVEN, S5_WIDTH, S5_WIDTH), S5_WIDTH ** -0.5),
        'e_s5_b_glu': nrm((N_EVEN, S5_WIDTH), 0.01),
        'e_q_norm': 1.0 + nrm((N_EVEN, MLA_Q_LORA), 0.01),
        'e_w_uq': nrm((N_EVEN, MLA_Q_LORA, MLA_HEADS, MLA_NOPE + MLA_ROPE), MLA_Q_LORA ** -0.5),
        'e_kv_norm': 1.0 + nrm((N_EVEN, MLA_KV_LORA), 0.01),
        'e_w_uk': nrm((N_EVEN, MLA_KV_LORA, MLA_HEADS, MLA_NOPE), MLA_KV_LORA ** -0.5),
        'e_w_uv': nrm((N_EVEN, MLA_KV_LORA, MLA_HEADS, MLA_V), MLA_KV_LORA ** -0.5),
        'e_w_out': nrm((N_EVEN, EVEN_MIX, D_MODEL), EVEN_MIX ** -0.5),
        'o_w_in': nrm((N_ODD, D_MODEL, ODD_IN), D_MODEL ** -0.5),
        'o_sinks': nrm((N_ODD, SWA_HEADS), 1.0),
        'o_w_out': nrm((N_ODD, ODD_MIX, D_MODEL), ODD_MIX ** -0.5),
        'f_w_gate': nrm((DEPTH, D_MODEL, D_FF), D_MODEL ** -0.5),
        'f_w_up': nrm((DEPTH, D_MODEL, D_FF), D_MODEL ** -0.5),
        'f_conv_w': nrm((DEPTH, CONV_W, D_FF), CONV_W ** -0.5),
        'f_conv_b': nrm((DEPTH, D_FF), 0.01),
        'f_w_down': nrm((DEPTH, D_FF, D_MODEL), D_FF ** -0.5),
    }


def reference(x_prompt, x_sample, page_table, state_s5_re, state_s5_im,
              cache_mla_ckv, cache_mla_krope, cache_swa_k, cache_swa_v, state_ffn_conv,
              norm_mix, norm_ffn, final_norm,
              e_w_in, e_s5_lam_re, e_s5_lam_im, e_s5_log_dt, e_s5_b_re, e_s5_b_im,
              e_s5_c_re, e_s5_c_im, e_s5_d, e_s5_w_glu, e_s5_b_glu,
              e_q_norm, e_w_uq, e_kv_norm, e_w_uk, e_w_uv, e_w_out,
              o_w_in, o_sinks, o_w_out,
              f_w_gate, f_w_up, f_conv_w, f_conv_b, f_w_down):

    def run(x, pos, prompt):
        n, t, _ = x.shape
        s5_re_l, s5_im_l, ckv_l, kr_l, k_l, v_l, conv_l = [], [], [], [], [], [], []
        for layer in range(DEPTH):
            li = layer // 2
            h = rmsnorm(x, norm_mix[layer])
            if layer % 2 == 0:
                z = h @ e_w_in[li]
                o1 = S5_WIDTH
                o2 = o1 + MLA_Q_LORA
                o3 = o2 + MLA_KV_LORA
                u, xq, xkv, kr = z[..., :o1], z[..., o1:o2], z[..., o2:o3], z[..., o3:]
                if prompt:
                    h0_re = jnp.zeros((n, S5_GROUPS, S5_STATE), jnp.float32)
                    h0_im = jnp.zeros((n, S5_GROUPS, S5_STATE), jnp.float32)
                else:
                    h0_re, h0_im = state_s5_re[li], state_s5_im[li]
                y_s5, hr, hi = s5_mixer(u, h0_re, h0_im, e_s5_lam_re[li], e_s5_lam_im[li], e_s5_log_dt[li],
                                        e_s5_b_re[li], e_s5_b_im[li], e_s5_c_re[li], e_s5_c_im[li],
                                        e_s5_d[li], e_s5_w_glu[li], e_s5_b_glu[li])
                q_nope, q_rope, ckv, k_rope = mla_project(xq, xkv, kr, pos, e_q_norm[li], e_w_uq[li], e_kv_norm[li])
                if prompt:
                    o = mla_prompt_attn(q_nope, q_rope, ckv, k_rope, e_w_uk[li], e_w_uv[li])
                else:
                    past_ckv = cache_mla_ckv[li][page_table].reshape(n, -1, MLA_KV_LORA)
                    past_kr = cache_mla_krope[li][page_table].reshape(n, -1, MLA_ROPE)
                    o = mla_sample_attn(q_nope, q_rope, ckv, k_rope, past_ckv.astype(ckv.dtype),
                                        past_kr.astype(k_rope.dtype), e_w_uk[li], e_w_uv[li])
                mix = jnp.concatenate([y_s5, o.astype(y_s5.dtype)], axis=-1) @ e_w_out[li]
                s5_re_l.append(hr)
                s5_im_l.append(hi)
                ckv_l.append(ckv)
                kr_l.append(k_rope)
            else:
                z = h @ o_w_in[li]
                c1 = SWA_HEADS * SWA_HD
                c2 = c1 + SWA_KV_HEADS * SWA_HD
                q = z[..., :c1].reshape(n, t, SWA_HEADS, SWA_HD)
                k = z[..., c1:c2].reshape(n, t, SWA_KV_HEADS, SWA_HD)
                v = z[..., c2:].reshape(n, t, SWA_KV_HEADS, SWA_HD)
                q = partial_rope(q, pos)
                k = partial_rope(k, pos)
                if prompt:
                    o = swa_prompt_attn(q, k, v, o_sinks[li])
                    w_keep = min(WINDOW, t)
                    k_l.append(k[:, t - w_keep:])
                    v_l.append(v[:, t - w_keep:])
                else:
                    o = swa_sample_attn(q, k, v, cache_swa_k[li], cache_swa_v[li], o_sinks[li])
                    k_l.append(k)
                    v_l.append(v)
                mix = o @ o_w_out[li]
            x = x + mix.astype(x.dtype)
            h = rmsnorm(x, norm_ffn[layer])
            if prompt:
                buf = jnp.zeros((n, CONV_W - 1, D_FF), x.dtype)
            else:
                buf = state_ffn_conv[layer]
            f, new_buf = conv_ffn(h, buf, f_w_gate[layer], f_w_up[layer], f_conv_w[layer], f_conv_b[layer], f_w_down[layer])
            conv_l.append(new_buf)
            x = x + f.astype(x.dtype)
        y = rmsnorm(x, final_norm)
        return (y, jnp.stack(s5_re_l), jnp.stack(s5_im_l), jnp.stack(ckv_l), jnp.stack(kr_l),
                jnp.stack(k_l), jnp.stack(v_l), jnp.stack(conv_l))

    (y_prompt, p_s5_re, p_s5_im, p_ckv, p_krope, p_swa_k, p_swa_v, p_conv) = run(
        x_prompt, jnp.arange(SEQ), True)
    (y_sample, s_s5_re, s_s5_im, s_ckv, s_krope, s_swa_k, s_swa_v, s_conv) = run(
        x_sample, PAST_LEN + jnp.arange(DEC_SEQ), False)
    return (y_prompt, y_sample, p_s5_re, p_s5_im, p_ckv, p_krope, p_swa_k, p_swa_v, p_conv,
            s_s5_re, s_s5_im, s_ckv, s_krope, s_swa_k, s_swa_v, s_conv)
```

```python
import functools
import math

import numpy as np
import jax
import jax.numpy as jnp
from jax import lax
from jax.experimental import pallas as pl
from jax.experimental.pallas import tpu as pltpu

F32 = jnp.float32
BF16 = jnp.bfloat16

D_MODEL = 1024
S5_WIDTH = 512
S5_GROUPS = 32
S5_GROUP = 16
S5_STATE = 64
S5_NSTATE = S5_GROUPS * S5_STATE
MLA_HEADS = 16
MLA_NOPE = 64
MLA_ROPE = 32
MLA_V = 64
MLA_Q_LORA = 384
MLA_KV_LORA = 256
MLA_THETA = 10000.0
MLA_SCALE = 1.0 / math.sqrt(MLA_NOPE + MLA_ROPE)
SWA_HEADS = 16
SWA_KV_HEADS = 4
SWA_HD = 64
SWA_GQ = SWA_HEADS // SWA_KV_HEADS
WINDOW = 128
ROT_DIM = SWA_HD // 4
ROPE_THETA = 500000.0
SWA_SCALE = 1.0 / math.sqrt(SWA_HD)
D_FF = 2816
PAGE = 128
EPS = 1e-6
NEG = -1e30

LANE = 128
SUBLANE = 8
HEAD_PAD = LANE
Z_EVEN = S5_WIDTH + MLA_Q_LORA + MLA_KV_LORA + LANE
QK_ODD = (SWA_HEADS + SWA_KV_HEADS) * SWA_HD
FF_CHUNK = 256
VMEM_LIMIT = 48 * 1024 * 1024


def _cparams(sem):
    return pltpu.CompilerParams(dimension_semantics=sem, vmem_limit_bytes=VMEM_LIMIT)


def _rms(x, g):
    return x * lax.rsqrt(jnp.mean(x * x, axis=-1, keepdims=True) + EPS) * g


def _const_spec(shape):
    nd = len(shape)
    return pl.BlockSpec(shape, lambda *_: (0,) * nd)


def _single_spec(shape):
    nd = len(shape)
    return pl.BlockSpec(shape, lambda *_: (0,) * nd, pipeline_mode=pl.Buffered(1))


def _mla_rope_tables(pos):
    half = MLA_ROPE // 2
    inv = MLA_THETA ** (-np.arange(half, dtype=np.float64) * 2.0 / MLA_ROPE)
    ang = np.asarray(pos, np.float64)[:, None] * inv[None, :]
    cos, sin = np.cos(ang), np.sin(ang)
    p = ang.shape[0]
    c = np.zeros((p, LANE)); sa = np.zeros((p, LANE)); sb = np.zeros((p, LANE))
    c[:, :MLA_NOPE] = 1.0
    c[:, MLA_NOPE:MLA_NOPE + half] = cos
    c[:, MLA_NOPE + half:MLA_NOPE + 2 * half] = cos
    sa[:, MLA_NOPE + half:MLA_NOPE + 2 * half] = sin
    sb[:, MLA_NOPE:MLA_NOPE + half] = -sin
    return tuple(jnp.asarray(t, F32) for t in (c, sa, sb))


def _swa_rope_tables(pos):
    half = ROT_DIM // 2
    inv = ROPE_THETA ** (-np.arange(half, dtype=np.float64) * 2.0 / ROT_DIM)
    ang = np.asarray(pos, np.float64)[:, None] * inv[None, :]
    cos, sin = np.cos(ang), np.sin(ang)
    p = ang.shape[0]
    c = np.ones((p, LANE)); sa = np.zeros((p, LANE)); sb = np.zeros((p, LANE))
    for o in (0, SWA_HD):
        c[:, o:o + half] = cos
        c[:, o + half:o + 2 * half] = cos
        sa[:, o + half:o + 2 * half] = sin
        sb[:, o:o + half] = -sin
    return tuple(jnp.asarray(t, F32) for t in (c, sa, sb))


def _rope_lanes(x, c, sa, sb, half):
    width = x.shape[1]
    reps = width // LANE
    if reps > 1:
        c = jnp.concatenate([c] * reps, axis=1)
        sa = jnp.concatenate([sa] * reps, axis=1)
        sb = jnp.concatenate([sb] * reps, axis=1)
    return x * c + pltpu.roll(x, half, 1) * sa + pltpu.roll(x, width - half, 1) * sb


def _even_proj_kernel(x_ref, c_ref, sa_ref, sb_ref, nm_ref, win_ref, qn_ref, wuq_ref, kvn_ref,
                      wuk_ref, wuv_ref, u_ref, q_ref, ckv_ref, kr_ref, *kv_refs):
    h = _rms(x_ref[...], nm_ref[...]).astype(BF16)
    z = jnp.dot(h, win_ref[...], preferred_element_type=F32)
    u_ref[...] = z[:, :S5_WIDTH]
    c, sa, sb = c_ref[...], sa_ref[...], sb_ref[...]
    o1 = S5_WIDTH + MLA_Q_LORA
    o2 = o1 + MLA_KV_LORA
    qn = _rms(z[:, S5_WIDTH:o1], qn_ref[...]).astype(BF16)
    q = jnp.dot(qn, wuq_ref[...], preferred_element_type=F32)
    q_ref[...] = _rope_lanes(q, c, sa, sb, MLA_ROPE // 2).astype(BF16)
    ckv = _rms(z[:, o1:o2], kvn_ref[...])
    ckv_ref[...] = ckv
    kr = _rope_lanes(z[:, o2:], c, sa, sb, MLA_ROPE // 2)
    kr_ref[...] = kr
    if kv_refs:
        k_ref, v_ref = kv_refs
        ckvb = ckv.astype(BF16)
        k = jnp.dot(ckvb, wuk_ref[...], preferred_element_type=F32)
        k_ref[...] = (k + jnp.concatenate([kr] * MLA_HEADS, axis=1)).astype(BF16)
        v_ref[...] = jnp.dot(ckvb, wuv_ref[...], preferred_element_type=F32).astype(BF16)


def _even_proj(x, tabs, nm, win, qnorm, wuq, kvnorm, wuk, wuv, *, tm, with_kv):
    rows = x.shape[0]
    nt = tabs[0].shape[0] // tm
    row = lambda w: pl.BlockSpec((tm, w), lambda i: (i, 0))
    tab = pl.BlockSpec((tm, LANE), lambda i: (i % nt, 0))
    qw = MLA_HEADS * HEAD_PAD
    out_shape = [jax.ShapeDtypeStruct((rows, S5_WIDTH), F32),
                 jax.ShapeDtypeStruct((rows, qw), BF16),
                 jax.ShapeDtypeStruct((rows, MLA_KV_LORA), F32),
                 jax.ShapeDtypeStruct((rows, LANE), F32)]
    out_specs = [row(S5_WIDTH), row(qw), row(MLA_KV_LORA), row(LANE)]
    if with_kv:
        out_shape += [jax.ShapeDtypeStruct((rows, qw), BF16),
                      jax.ShapeDtypeStruct((rows, MLA_HEADS * MLA_V), BF16)]
        out_specs += [row(qw), row(MLA_HEADS * MLA_V)]
    return pl.pallas_call(
        _even_proj_kernel,
        grid=(rows // tm,),
        in_specs=[row(D_MODEL), tab, tab, tab, _const_spec(nm.shape), _const_spec(win.shape),
                  _const_spec(qnorm.shape), _const_spec(wuq.shape), _const_spec(kvnorm.shape),
                  _const_spec(wuk.shape), _const_spec(wuv.shape)],
        out_specs=out_specs,
        out_shape=out_shape,
        compiler_params=_cparams(("arbitrary",)),
        name="even_proj",
    )(x, *tabs, nm, win, qnorm, wuq, kvnorm, wuk, wuv)


S5_STRIP = 512


def _s5_kernel(u_ref, h0r_ref, h0i_ref, lamr_ref, lami_ref, ldt_ref, wb_ref, wcr_ref, wci_ref,
               d_ref, wglu_ref, bglu_ref, y_ref, hr_ref, hi_ref, ut_ref, xr_ref, xi_ref, yt_ref,
               *, nb, tt, transpose_io):
    @pl.when(pl.program_id(0) == 0)
    def _():
        xr_ref[0:nb, :] = h0r_ref[...]
        xi_ref[0:nb, :] = h0i_ref[...]

    if transpose_io:
        for t in range(tt):
            ut_ref[t * nb:(t + 1) * nb, :] = u_ref[:, t, :]
    else:
        ut_ref[...] = u_ref[0]
    ub = ut_ref[...].astype(BF16)

    lr = jnp.minimum(lamr_ref[...], -1e-4)
    li = lami_ref[...]
    dt = jnp.exp(ldt_ref[...])
    mag = jnp.exp(lr * dt)
    ar = mag * jnp.cos(li * dt)
    ai = mag * jnp.sin(li * dt)
    den = lr * lr + li * li
    zr = ((ar - 1.0) * lr + ai * li) / den
    zi = (ai * lr - (ar - 1.0) * li) / den

    for s in range(S5_NSTATE // S5_STRIP):
        cols = slice(s * S5_STRIP, (s + 1) * S5_STRIP)
        br = jnp.dot(ub, wb_ref[:, cols], preferred_element_type=F32)
        bi = jnp.dot(ub, wb_ref[:, S5_NSTATE + s * S5_STRIP:S5_NSTATE + (s + 1) * S5_STRIP],
                     preferred_element_type=F32)
        zrs, zis = zr[:, cols], zi[:, cols]
        xr_ref[nb:, cols] = zrs * br - zis * bi
        xi_ref[nb:, cols] = zrs * bi + zis * br
        ars = jnp.broadcast_to(ar[:, cols], (nb, S5_STRIP))
        ais = jnp.broadcast_to(ai[:, cols], (nb, S5_STRIP))

        def step(t, carry):
            hr, hi = carry
            r0 = pl.multiple_of((t + 1) * nb, nb)
            nhr = ars * hr - ais * hi + xr_ref[pl.ds(r0, nb), cols]
            nhi = ars * hi + ais * hr + xi_ref[pl.ds(r0, nb), cols]
            xr_ref[pl.ds(r0, nb), cols] = nhr
            xi_ref[pl.ds(r0, nb), cols] = nhi
            return nhr, nhi

        hr, hi = lax.fori_loop(0, tt, step, (xr_ref[0:nb, cols], xi_ref[0:nb, cols]),
                               unroll=min(tt, 8))
        xr_ref[0:nb, cols] = hr
        xi_ref[0:nb, cols] = hi

    hr_ref[...] = xr_ref[0:nb, :]
    hi_ref[...] = xi_ref[0:nb, :]
    y = (jnp.dot(xr_ref[nb:, :].astype(BF16), wcr_ref[...], preferred_element_type=F32)
         + jnp.dot(xi_ref[nb:, :].astype(BF16), wci_ref[...], preferred_element_type=F32)
         + d_ref[...] * ut_ref[...])
    y = jax.nn.gelu(y)
    gate = jnp.dot(y.astype(BF16), wglu_ref[...], preferred_element_type=F32) + bglu_ref[...]
    y = y * jax.nn.sigmoid(gate)
    if transpose_io:
        groups = S5_WIDTH // LANE
        for c in range(groups):
            yt_ref[c] = y[:, c * LANE:(c + 1) * LANE]
        for n in range(nb):
            y_ref[n] = jnp.concatenate(
                [yt_ref[c, pl.ds(n, tt, stride=nb), :] for c in range(groups)],
                axis=1).astype(y_ref.dtype)
    else:
        y_ref[0] = y.astype(y_ref.dtype)


def _s5(u, h0r, h0i, lamr, lami, ldt, wb, wcr, wci, d, wglu, bglu, *, nb, tt, transpose_io):
    t_total = u.shape[1] if transpose_io else u.shape[0]
    if transpose_io:
        u_spec = pl.BlockSpec((nb, tt, S5_WIDTH), lambda i: (0, i, 0))
        y_shape = (nb, t_total, S5_WIDTH)
    else:
        u_spec = pl.BlockSpec((tt, nb, S5_WIDTH), lambda i: (i, 0, 0))
        y_shape = (t_total, nb, S5_WIDTH)
    consts = (h0r, h0i, lamr, lami, ldt, wb, wcr, wci, d, wglu, bglu)
    kern = functools.partial(_s5_kernel, nb=nb, tt=tt, transpose_io=transpose_io)
    return pl.pallas_call(
        kern,
        grid=(t_total // tt,),
        in_specs=[u_spec] + [_const_spec(c.shape) for c in consts],
        out_specs=[u_spec, _const_spec((nb, S5_NSTATE)), _const_spec((nb, S5_NSTATE))],
        out_shape=[jax.ShapeDtypeStruct(y_shape, BF16),
                   jax.ShapeDtypeStruct((nb, S5_NSTATE), F32),
                   jax.ShapeDtypeStruct((nb, S5_NSTATE), F32)],
        scratch_shapes=[pltpu.VMEM((nb * tt, S5_WIDTH), F32),
                        pltpu.VMEM((nb * (tt + 1), S5_NSTATE), F32),
                        pltpu.VMEM((nb * (tt + 1), S5_NSTATE), F32),
                        pltpu.VMEM((S5_WIDTH // LANE, nb * tt, LANE), F32)],
        compiler_params=_cparams(("arbitrary",)),
        name="s5_mixer",
    )(u, *consts)


def _mla_attn_kernel(q_ref, k_ref, v_ref, o_ref, *, blk):
    qi = pl.program_id(1)
    row = lax.broadcasted_iota(jnp.int32, (blk, blk), 0)
    col = lax.broadcasted_iota(jnp.int32, (blk, blk), 1)
    causal = col <= row
    lane = lax.broadcasted_iota(jnp.int32, (blk, LANE), 1)
    for j in range(MLA_HEADS // 2):
        heads = (2 * j, 2 * j + 1)
        qs = [q_ref[0, :, h * HEAD_PAD:(h + 1) * HEAD_PAD] for h in heads]

        def block(kb, carry, masked, heads=heads, qs=qs, j=j):
            k0 = pl.multiple_of(kb * blk, blk)
            vp = v_ref[0, pl.ds(k0, blk), j * LANE:(j + 1) * LANE]
            new = []
            for idx, h in enumerate(heads):
                m, l, acc = carry[idx]
                kh = k_ref[0, pl.ds(k0, blk), h * HEAD_PAD:(h + 1) * HEAD_PAD]
                s = lax.dot_general(qs[idx], kh, (((1,), (1,)), ((), ())),
                                    preferred_element_type=F32) * MLA_SCALE
                if masked:
                    s = jnp.where(causal, s, NEG)
                m_new = jnp.maximum(m, jnp.max(s, axis=1, keepdims=True))
                alpha = jnp.exp(m - m_new)
                p = jnp.exp(s - m_new)
                l = alpha * l + jnp.sum(p, axis=1, keepdims=True)
                acc = alpha * acc + jnp.dot(p.astype(BF16), vp, preferred_element_type=F32)
                new.append((m_new, l, acc))
            return tuple(new)

        one = (jnp.full((blk, 1), NEG, F32), jnp.zeros((blk, 1), F32), jnp.zeros((blk, LANE), F32))
        carry = lax.fori_loop(0, qi, lambda kb, c: block(kb, c, False), (one, one))
        (_, l0, a0), (_, l1, a1) = block(qi, carry, True)
        o = jnp.where(lane < MLA_V, a0 / l0, a1 / l1)
        o_ref[0, :, j * LANE:(j + 1) * LANE] = o.astype(o_ref.dtype)


def _mla_attn(q, k, v, *, blk):
    n, t, _ = q.shape
    kern = functools.partial(_mla_attn_kernel, blk=blk)
    return pl.pallas_call(
        kern,
        grid=(n, t // blk),
        in_specs=[pl.BlockSpec((1, blk, q.shape[2]), lambda b, i: (b, i, 0)),
                  pl.BlockSpec((1, t, k.shape[2]), lambda b, i: (b, 0, 0)),
                  pl.BlockSpec((1, t, v.shape[2]), lambda b, i: (b, 0, 0))],
        out_specs=pl.BlockSpec((1, blk, MLA_HEADS * MLA_V), lambda b, i: (b, i, 0)),
        out_shape=jax.ShapeDtypeStruct((n, t, MLA_HEADS * MLA_V), BF16),
        compiler_params=_cparams(("arbitrary", "arbitrary")),
        name="mla_prompt_attn",
    )(q, k, v)


DEC_PAGES = 16
DEC_KEYS = DEC_PAGES * PAGE


def _qlat_kernel(q_ref, wk_ref, o_ref):
    for h in range(MLA_HEADS):
        o_ref[:, h * MLA_KV_LORA:(h + 1) * MLA_KV_LORA] = jnp.dot(
            q_ref[:, h * HEAD_PAD:(h + 1) * HEAD_PAD], wk_ref[h], preferred_element_type=F32)


def _qlat(q_big, wk_t):
    rows = q_big.shape[0]
    return pl.pallas_call(
        _qlat_kernel,
        grid=(1,),
        in_specs=[_const_spec(q_big.shape), _const_spec(wk_t.shape)],
        out_specs=_const_spec((rows, MLA_HEADS * MLA_KV_LORA)),
        out_shape=jax.ShapeDtypeStruct((rows, MLA_HEADS * MLA_KV_LORA), F32),
        compiler_params=_cparams(("arbitrary",)),
        name="mla_q_absorb",
    )(q_big, wk_t)


def _olat_kernel(o_ref, wv_ref, out_ref):
    for j in range(MLA_HEADS // 2):
        acc = None
        for h in (2 * j, 2 * j + 1):
            part = jnp.dot(o_ref[:, h * MLA_KV_LORA:(h + 1) * MLA_KV_LORA].astype(BF16),
                           wv_ref[:, h * LANE:(h + 1) * LANE], preferred_element_type=F32)
            acc = part if acc is None else acc + part
        out_ref[:, j * LANE:(j + 1) * LANE] = acc.astype(out_ref.dtype)


def _olat(o_lat, wv_big):
    rows = o_lat.shape[0]
    return pl.pallas_call(
        _olat_kernel,
        grid=(1,),
        in_specs=[_const_spec(o_lat.shape), _const_spec(wv_big.shape)],
        out_specs=_const_spec((rows, MLA_HEADS * MLA_V)),
        out_shape=jax.ShapeDtypeStruct((rows, MLA_HEADS * MLA_V), BF16),
        compiler_params=_cparams(("arbitrary",)),
        name="mla_v_expand",
    )(o_lat, wv_big)


def _page_copies(cache_ckv, cache_kr, ckv_buf, kr_buf, sem, slot, page, p):
    rows = pl.ds(p * PAGE, PAGE)
    return (pltpu.make_async_copy(cache_ckv.at[0, page], ckv_buf.at[slot, rows], sem.at[0, slot]),
            pltpu.make_async_copy(cache_kr.at[0, page], kr_buf.at[slot, rows], sem.at[1, slot]))


def _mla_decode_kernel(ptc_ref, ptn_ref, ql_ref, qr_ref, ckvn_ref, krn_ref, cache_ckv, cache_kr,
                       o_ref, ckv_buf, kr_buf, sem, *, n_seq, n_chunks):
    b = pl.program_id(0)

    def start_chunk(pt_ref, chunk, slot):
        for p in range(DEC_PAGES):
            page = pt_ref[0, 0, chunk * DEC_PAGES + p]
            for cp in _page_copies(cache_ckv, cache_kr, ckv_buf, kr_buf, sem, slot, page, p):
                cp.start()

    @pl.when(b == 0)
    def _():
        start_chunk(ptc_ref, 0, 0)

    ql = ql_ref[0].astype(BF16)
    qr = qr_ref[0].astype(BF16)

    def chunk_body(c, carry):
        m, l, acc = carry
        slot = lax.rem(c, 2)
        nxt = 1 - slot

        @pl.when(c + 1 < n_chunks)
        def _():
            start_chunk(ptc_ref, c + 1, nxt)

        @pl.when(jnp.logical_and(c + 1 == n_chunks, b + 1 < n_seq))
        def _():
            start_chunk(ptn_ref, 0, nxt)

        for p in range(DEC_PAGES):
            for cp in _page_copies(cache_ckv, cache_kr, ckv_buf, kr_buf, sem, slot, 0, p):
                cp.wait()
        kb = ckv_buf[slot].astype(BF16)
        krb = kr_buf[slot].astype(BF16)
        dn = (((1,), (1,)), ((), ()))
        s = (lax.dot_general(ql, kb, dn, preferred_element_type=F32)
             + lax.dot_general(qr, krb, dn, preferred_element_type=F32)) * MLA_SCALE
        m_new = jnp.maximum(m, jnp.max(s, axis=1, keepdims=True))
        alpha = jnp.exp(m - m_new)
        pr = jnp.exp(s - m_new)
        l = alpha * l + jnp.sum(pr, axis=1, keepdims=True)
        acc = alpha * acc + jnp.dot(pr.astype(BF16), kb, preferred_element_type=F32)
        return m_new, l, acc

    init = (jnp.full((MLA_HEADS, 1), NEG, F32), jnp.zeros((MLA_HEADS, 1), F32),
            jnp.zeros((MLA_HEADS, MLA_KV_LORA), F32))
    m, l, acc = lax.fori_loop(0, n_chunks, chunk_body, init)

    cn = ckvn_ref[0].astype(BF16).astype(F32)
    kn = krn_ref[0].astype(BF16).astype(F32)
    s_new = (jnp.sum(ql.astype(F32) * cn, axis=1, keepdims=True)
             + jnp.sum(qr.astype(F32) * kn, axis=1, keepdims=True)) * MLA_SCALE
    m_new = jnp.maximum(m, s_new)
    alpha = jnp.exp(m - m_new)
    p_new = jnp.exp(s_new - m_new)
    l = alpha * l + p_new
    acc = alpha * acc + p_new.astype(BF16).astype(F32) * cn
    o_ref[0] = acc / l


def _mla_decode(page_table, q_lat, q_rope, ckv_new, kr_new, cache_ckv, cache_kr):
    n_seq, n_pages = page_table.shape
    n_chunks = n_pages // DEC_PAGES
    pt = page_table.reshape(n_seq, 1, n_pages)
    smem_row = lambda f: pl.BlockSpec((1, 1, n_pages), f, memory_space=pltpu.SMEM)
    per_seq = lambda a: pl.BlockSpec((1,) + a.shape[1:], lambda b: (b, 0, 0))
    kern = functools.partial(_mla_decode_kernel, n_seq=n_seq, n_chunks=n_chunks)
    return pl.pallas_call(
        kern,
        grid=(n_seq,),
        in_specs=[smem_row(lambda b: (b, 0, 0)),
                  smem_row(lambda b: (jnp.minimum(b + 1, n_seq - 1), 0, 0)),
                  per_seq(q_lat), per_seq(q_rope), per_seq(ckv_new), per_seq(kr_new),
                  pl.BlockSpec(memory_space=pl.ANY), pl.BlockSpec(memory_space=pl.ANY)],
        out_specs=pl.BlockSpec((1, MLA_HEADS, MLA_KV_LORA), lambda b: (b, 0, 0)),
        out_shape=jax.ShapeDtypeStruct((n_seq, MLA_HEADS, MLA_KV_LORA), F32),
        scratch_shapes=[pltpu.VMEM((2, DEC_KEYS, MLA_KV_LORA), F32),
                        pltpu.VMEM((2, DEC_KEYS, MLA_ROPE), F32),
                        pltpu.SemaphoreType.DMA((2, 2))],
        compiler_params=_cparams(("arbitrary",)),
        name="mla_paged_decode",
    )(pt, pt, q_lat, q_rope, ckv_new, kr_new, cache_ckv, cache_kr)


def _odd_proj_kernel(x_ref, c_ref, sa_ref, sb_ref, nm_ref, win_ref, q_ref, k_ref, v_ref):
    h = _rms(x_ref[...], nm_ref[...]).astype(BF16)
    z = jnp.dot(h, win_ref[...], preferred_element_type=F32)
    qk = _rope_lanes(z[:, :QK_ODD], c_ref[...], sa_ref[...], sb_ref[...], ROT_DIM // 2)
    q_ref[...] = qk[:, :SWA_HEADS * SWA_HD].astype(BF16)
    k_ref[...] = qk[:, SWA_HEADS * SWA_HD:]
    v_ref[...] = z[:, QK_ODD:]


def _odd_proj(x, tabs, nm, win, *, tm):
    rows = x.shape[0]
    nt = tabs[0].shape[0] // tm
    row = lambda w: pl.BlockSpec((tm, w), lambda i: (i, 0))
    tab = pl.BlockSpec((tm, LANE), lambda i: (i % nt, 0))
    kvw = SWA_KV_HEADS * SWA_HD
    return pl.pallas_call(
        _odd_proj_kernel,
        grid=(rows // tm,),
        in_specs=[row(D_MODEL), tab, tab, tab, _const_spec(nm.shape), _const_spec(win.shape)],
        out_specs=[row(SWA_HEADS * SWA_HD), row(kvw), row(kvw)],
        out_shape=[jax.ShapeDtypeStruct((rows, SWA_HEADS * SWA_HD), BF16),
                   jax.ShapeDtypeStruct((rows, kvw), F32),
                   jax.ShapeDtypeStruct((rows, kvw), F32)],
        compiler_params=_cparams(("arbitrary",)),
        name="odd_proj",
    )(x, *tabs, nm, win)


def _swa_attn_kernel(sink_ref, q_ref, kp_ref, kc_ref, vp_ref, vc_ref, o_ref):
    b = pl.program_id(1)
    kk = jnp.concatenate([kp_ref[0], kc_ref[0]], axis=0).astype(BF16)
    vv = jnp.concatenate([vp_ref[0], vc_ref[0]], axis=0).astype(BF16)
    rows = SWA_GQ * WINDOW
    r = lax.broadcasted_iota(jnp.int32, (rows, 2 * WINDOW), 0)
    j = lax.broadcasted_iota(jnp.int32, (rows, 2 * WINDOW), 1)
    rel = jnp.bitwise_and(r, WINDOW - 1) + WINDOW - j
    valid = (rel >= 0) & (rel <= WINDOW) & ((j >= WINDOW) | (b > 0))
    rcol = lax.broadcasted_iota(jnp.int32, (rows, 1), 0)
    outs = []
    for kh in range(SWA_KV_HEADS):
        qg = jnp.concatenate(
            [q_ref[0, :, (kh * SWA_GQ + g) * SWA_HD:(kh * SWA_GQ + g + 1) * SWA_HD]
             for g in range(SWA_GQ)], axis=0)
        khd = kk[:, kh * SWA_HD:(kh + 1) * SWA_HD]
        vhd = vv[:, kh * SWA_HD:(kh + 1) * SWA_HD]
        s = lax.dot_general(qg, khd, (((1,), (1,)), ((), ())),
                            preferred_element_type=F32) * SWA_SCALE
        s = jnp.where(valid, s, NEG)
        sink = jnp.full((rows, 1), sink_ref[kh * SWA_GQ + SWA_GQ - 1], F32)
        for g in range(SWA_GQ - 2, -1, -1):
            sink = jnp.where(rcol < (g + 1) * WINDOW, sink_ref[kh * SWA_GQ + g], sink)
        m = jnp.maximum(jnp.max(s, axis=1, keepdims=True), sink)
        p = jnp.exp(s - m)
        l = jnp.sum(p, axis=1, keepdims=True) + jnp.exp(sink - m)
        o = jnp.dot(p.astype(BF16), vhd, preferred_element_type=F32) / l
        outs += [o[g * WINDOW:(g + 1) * WINDOW] for g in range(SWA_GQ)]
    o_ref[0] = jnp.concatenate(outs, axis=1).astype(o_ref.dtype)


def _swa_attn(sinks, q, k, v):
    n, t, _ = q.shape
    kvw = SWA_KV_HEADS * SWA_HD
    prev = pl.BlockSpec((1, WINDOW, kvw), lambda a, i: (a, jnp.maximum(i - 1, 0), 0))
    cur = pl.BlockSpec((1, WINDOW, kvw), lambda a, i: (a, i, 0))
    qspec = pl.BlockSpec((1, WINDOW, SWA_HEADS * SWA_HD), lambda a, i: (a, i, 0))
    return pl.pallas_call(
        _swa_attn_kernel,
        grid=(n, t // WINDOW),
        in_specs=[pl.BlockSpec(memory_space=pltpu.SMEM), qspec, prev, cur, prev, cur],
        out_specs=qspec,
        out_shape=jax.ShapeDtypeStruct((n, t, SWA_HEADS * SWA_HD), BF16),
        compiler_params=_cparams(("arbitrary", "arbitrary")),
        name="swa_prompt_attn",
    )(sinks, q, k, k, v, v)


SWA_DEC_SEQS = 8


def _swa_decode_kernel(sink_ref, q_ref, kn_ref, vn_ref, ck_ref, cv_ref, o_ref):
    q = q_ref[...]
    hgrp = lax.broadcasted_iota(jnp.int32, (1, SWA_HEADS, 1), 1) // SWA_GQ
    hidx = lax.broadcasted_iota(jnp.int32, (1, SWA_HEADS, 1), 1)
    sink = jnp.zeros((1, SWA_HEADS, 1), F32)
    for h in range(SWA_HEADS):
        sink = jnp.where(hidx == h, sink_ref[h], sink)
    s = jnp.zeros((q.shape[0], SWA_HEADS, WINDOW), F32)
    for kh in range(SWA_KV_HEADS):
        skh = jnp.einsum('nhd,njd->nhj', q, ck_ref[kh].astype(BF16), preferred_element_type=F32)
        s = jnp.where(hgrp == kh, skh, s)
    s = s * SWA_SCALE
    qf = q.astype(F32)
    kn = kn_ref[...].astype(BF16).astype(F32)
    vn = vn_ref[...].astype(BF16).astype(F32)
    s_new = jnp.sum(qf * kn, axis=2, keepdims=True) * SWA_SCALE
    m = jnp.maximum(jnp.maximum(jnp.max(s, axis=2, keepdims=True), s_new), sink)
    p = jnp.exp(s - m)
    p_new = jnp.exp(s_new - m)
    l = jnp.sum(p, axis=2, keepdims=True) + p_new + jnp.exp(sink - m)
    o = p_new.astype(BF16).astype(F32) * vn
    for kh in range(SWA_KV_HEADS):
        pk = jnp.where(hgrp == kh, p, 0.0).astype(BF16)
        o = o + jnp.einsum('nhj,njd->nhd', pk, cv_ref[kh].astype(BF16), preferred_element_type=F32)
    o_ref[...] = (o / l).astype(o_ref.dtype)


def _swa_decode(sinks, q, k_new, v_new, cache_k, cache_v):
    n = q.shape[0]
    nb = SWA_DEC_SEQS
    seq = pl.BlockSpec((nb, SWA_HEADS, SWA_HD), lambda i: (i, 0, 0))
    cache = pl.BlockSpec((SWA_KV_HEADS, nb, WINDOW, SWA_HD), lambda i: (0, i, 0, 0))
    return pl.pallas_call(
        _swa_decode_kernel,
        grid=(n // nb,),
        in_specs=[pl.BlockSpec(memory_space=pltpu.SMEM), seq, seq, seq, cache, cache],
        out_specs=seq,
        out_shape=jax.ShapeDtypeStruct((n, SWA_HEADS, SWA_HD), BF16),
        compiler_params=_cparams(("arbitrary",)),
        name="swa_decode_attn",
    )(sinks, q, k_new, v_new, cache_k, cache_v)


def _post_kernel(*refs, n_mix, decode, final, tm):
    it = iter(refs)
    x_ref = next(it)
    mix_refs = [next(it) for _ in range(n_mix)]
    wo_refs = [next(it) for _ in range(n_mix)]
    nf_ref, wg_ref, wu_ref, cw_ref, cb_ref, wd_ref = (next(it) for _ in range(6))
    if decode:
        buf0_ref, buf1_ref = next(it), next(it)
    fn_ref = next(it) if final else None
    y_ref = next(it)
    g_ref = next(it)
    act_ref = next(it)
    carry_ref = None if decode else next(it)

    x1 = x_ref[...]
    for a_ref, w_ref in zip(mix_refs, wo_refs):
        x1 = x1 + jnp.dot(a_ref[...], w_ref[...], preferred_element_type=F32)
    h2 = _rms(x1, nf_ref[...]).astype(BF16)

    if not decode:
        @pl.when(pl.program_id(1) == 0)
        def _():
            carry_ref[...] = jnp.zeros_like(carry_ref)
        row = lax.broadcasted_iota(jnp.int32, (tm, FF_CHUNK), 0)

    for c in range(D_FF // FF_CHUNK):
        sl = slice(c * FF_CHUNK, (c + 1) * FF_CHUNK)
        g = jnp.dot(h2, wg_ref[:, sl], preferred_element_type=F32)
        u = jnp.dot(h2, wu_ref[:, sl], preferred_element_type=F32)
        if decode:
            gm2, gm1 = buf0_ref[:, sl], buf1_ref[:, sl]
            g_ref[:, sl] = g
        else:
            prev = carry_ref[:, sl]
            p6, p7 = prev[SUBLANE - 2:SUBLANE - 1, :], prev[SUBLANE - 1:SUBLANE, :]
            gm1 = jnp.where(row == 0, p7, pltpu.roll(g, 1, 0))
            gm2 = jnp.where(row == 0, p6, jnp.where(row == 1, p7, pltpu.roll(g, 2, 0)))
            last = g[tm - SUBLANE:tm, :]
            carry_ref[:, sl] = last
            g_ref[0, :, sl] = last
        cc = cb_ref[:, sl] + cw_ref[0:1, sl] * gm2 + cw_ref[1:2, sl] * gm1 + cw_ref[2:3, sl] * g
        act_ref[:, sl] = (jax.nn.gelu(cc) * u).astype(BF16)

    x2 = x1 + jnp.dot(act_ref[...], wd_ref[...], preferred_element_type=F32)
    y_ref[...] = _rms(x2, fn_ref[...]) if final else x2


def _post(x, mixes, wos, nf, wg, wu, cw, cb, wd, *, n_seq, tm, conv_bufs=None, final_norm=None):
    rows = x.shape[0]
    decode = conv_bufs is not None
    nt = rows // n_seq // tm if not decode else rows // tm
    grid = (1, nt) if decode else (n_seq, nt)
    row = lambda w: pl.BlockSpec((tm, w), lambda a, i: (a * nt + i, 0))
    args = [x] + list(mixes) + list(wos) + [nf, wg, wu, cw, cb, wd]
    in_specs = ([row(D_MODEL)] + [row(m.shape[1]) for m in mixes]
                + [_single_spec(w.shape) for w in wos]
                + [_const_spec(nf.shape), _single_spec(wg.shape), _single_spec(wu.shape),
                   _const_spec(cw.shape), _const_spec(cb.shape), _single_spec(wd.shape)])
    if decode:
        args += list(conv_bufs)
        in_specs += [row(D_FF), row(D_FF)]
    if final_norm is not None:
        args.append(final_norm)
        in_specs.append(_const_spec(final_norm.shape))
    if decode:
        g_shape = jax.ShapeDtypeStruct((rows, D_FF), F32)
        g_spec = row(D_FF)
    else:
        g_shape = jax.ShapeDtypeStruct((n_seq, SUBLANE, D_FF), F32)
        g_spec = pl.BlockSpec((1, SUBLANE, D_FF), lambda a, i: (a, 0, 0))
    scratch = [pltpu.VMEM((tm, D_FF), BF16)]
    if not decode:
        scratch.append(pltpu.VMEM((SUBLANE, D_FF), F32))
    kern = functools.partial(_post_kernel, n_mix=len(mixes), decode=decode,
                             final=final_norm is not None, tm=tm)
    return pl.pallas_call(
        kern,
        grid=grid,
        in_specs=in_specs,
        out_specs=[row(D_MODEL), g_spec],
        out_shape=[jax.ShapeDtypeStruct((rows, D_MODEL), F32), g_shape],
        scratch_shapes=scratch,
        compiler_params=_cparams(("arbitrary", "arbitrary")),
        name="post_decode" if decode else "post_prompt",
    )(*args)


def _prep_even(e_w_in, e_w_uq, e_w_uk, e_w_uv, e_s5_b_re, e_s5_b_im, e_s5_c_re, e_s5_c_im):
    o1 = S5_WIDTH
    o2 = o1 + MLA_Q_LORA
    o3 = o2 + MLA_KV_LORA
    zpad = lambda n: jnp.zeros((D_MODEL, n), F32)
    win = jnp.concatenate([e_w_in[:, :o3], zpad(MLA_NOPE), e_w_in[:, o3:],
                           zpad(HEAD_PAD - MLA_NOPE - MLA_ROPE)], axis=1).astype(BF16)
    pad_last = lambda w, n: jnp.pad(w, ((0, 0), (0, 0), (0, n - w.shape[2])))
    wuq = pad_last(e_w_uq, HEAD_PAD).reshape(MLA_Q_LORA, MLA_HEADS * HEAD_PAD).astype(BF16)
    wuk = pad_last(e_w_uk, HEAD_PAD).reshape(MLA_KV_LORA, MLA_HEADS * HEAD_PAD).astype(BF16)
    wuv = e_w_uv.reshape(MLA_KV_LORA, MLA_HEADS * MLA_V).astype(BF16)
    wk_t = jnp.pad(jnp.transpose(e_w_uk, (1, 2, 0)),
                   ((0, 0), (0, HEAD_PAD - MLA_NOPE), (0, 0))).astype(BF16)
    wv4 = e_w_uv.reshape(MLA_KV_LORA, MLA_HEADS // 2, 2, MLA_V)
    zv = jnp.zeros_like(wv4[:, :, 0])
    wv_big = jnp.stack([jnp.concatenate([wv4[:, :, 0], zv], axis=-1),
                        jnp.concatenate([zv, wv4[:, :, 1]], axis=-1)], axis=2)
    wv_big = wv_big.reshape(MLA_KV_LORA, MLA_HEADS * LANE).astype(BF16)
    eye = jnp.eye(S5_GROUPS, dtype=F32)
    bd_in = lambda b: jnp.einsum('gph,gk->ghkp', b, eye).reshape(S5_WIDTH, S5_NSTATE)
    bd_out = lambda c: jnp.einsum('ghp,gk->gpkh', c, eye).reshape(S5_NSTATE, S5_WIDTH)
    wb = jnp.concatenate([bd_in(e_s5_b_re), bd_in(e_s5_b_im)], axis=1).astype(BF16)
    wcr = bd_out(e_s5_c_re).astype(BF16)
    wci = (-bd_out(e_s5_c_im)).astype(BF16)
    return win, wuq, wuk, wuv, wk_t, wv_big, wb, wcr, wci


def kernel(x_prompt, x_sample, page_table, state_s5_re, state_s5_im, cache_mla_ckv, cache_mla_krope,
           cache_swa_k, cache_swa_v, state_ffn_conv, norm_mix, norm_ffn, final_norm, e_w_in,
           e_s5_lam_re, e_s5_lam_im, e_s5_log_dt, e_s5_b_re, e_s5_b_im, e_s5_c_re, e_s5_c_im, e_s5_d,
           e_s5_w_glu, e_s5_b_glu, e_q_norm, e_w_uq, e_kv_norm, e_w_uk, e_w_uv, e_w_out, o_w_in,
           o_sinks, o_w_out, f_w_gate, f_w_up, f_conv_w, f_conv_b, f_w_down):
    n_p, t_p, _ = x_prompt.shape
    n_s, t_s, _ = x_sample.shape
    assert t_s == 1
    past_len = page_table.shape[1] * PAGE
    row2 = lambda v: v.reshape(1, -1)

    (win, wuq, wuk, wuv, wk_t, wv_big, wb, wcr, wci) = _prep_even(
        e_w_in[0], e_w_uq[0], e_w_uk[0], e_w_uv[0], e_s5_b_re[0], e_s5_b_im[0],
        e_s5_c_re[0], e_s5_c_im[0])
    lamr, lami = row2(e_s5_lam_re[0]), row2(e_s5_lam_im[0])
    ldt = row2(jnp.repeat(e_s5_log_dt[0], S5_STATE))
    s5_consts = (lamr, lami, ldt, wb, wcr, wci, row2(e_s5_d[0]), e_s5_w_glu[0].astype(BF16),
                 row2(e_s5_b_glu[0]))
    wout_s5 = e_w_out[0][:S5_WIDTH].astype(BF16)
    wout_mla = e_w_out[0][S5_WIDTH:].astype(BF16)
    owin = o_w_in[0].astype(BF16)
    owout = o_w_out[0].astype(BF16)
    ffn = [(row2(norm_ffn[l]), f_w_gate[l].astype(BF16), f_w_up[l].astype(BF16), f_conv_w[l],
            row2(f_conv_b[l]), f_w_down[l].astype(BF16)) for l in range(2)]
    nm0, nm1 = row2(norm_mix[0]), row2(norm_mix[1])
    qnorm, kvnorm = row2(e_q_norm[0]), row2(e_kv_norm[0])
    fnorm = row2(final_norm)
    sinks = o_sinks[0]

    xp = x_prompt.reshape(n_p * t_p, D_MODEL)
    pos_p = np.arange(t_p)
    u, q_big, ckv_p, kr_p, k_big, v_p = _even_proj(
        xp, _mla_rope_tables(pos_p), nm0, win, qnorm, wuq, kvnorm, wuk, wuv, tm=256, with_kv=True)
    zeros_state = jnp.zeros((n_p, S5_NSTATE), F32)
    y_s5, p_hr, p_hi = _s5(u.reshape(n_p, t_p, S5_WIDTH), zeros_state, zeros_state, *s5_consts,
                           nb=n_p, tt=64, transpose_io=True)
    o_mla = _mla_attn(q_big.reshape(n_p, t_p, -1), k_big.reshape(n_p, t_p, -1),
                      v_p.reshape(n_p, t_p, -1), blk=256)
    x1, conv0 = _post(xp, [y_s5.reshape(n_p * t_p, -1), o_mla.reshape(n_p * t_p, -1)],
                      [wout_s5, wout_mla], *ffn[0], n_seq=n_p, tm=256)
    q1, k1, v1 = _odd_proj(x1, _swa_rope_tables(pos_p), nm1, owin, tm=256)
    kvw = SWA_KV_HEADS * SWA_HD
    o_swa = _swa_attn(sinks, q1.reshape(n_p, t_p, -1), k1.reshape(n_p, t_p, kvw),
                      v1.reshape(n_p, t_p, kvw))
    y_p, conv1 = _post(x1, [o_swa.reshape(n_p * t_p, -1)], [owout], *ffn[1], n_seq=n_p, tm=256,
                       final_norm=fnorm)

    y_prompt = y_p.reshape(n_p, t_p, D_MODEL)
    p_s5_re = p_hr.reshape(1, n_p, S5_GROUPS, S5_STATE)
    p_s5_im = p_hi.reshape(1, n_p, S5_GROUPS, S5_STATE)
    p_ckv = ckv_p.reshape(1, n_p, t_p, MLA_KV_LORA)
    p_krope = kr_p[:, MLA_NOPE:MLA_NOPE + MLA_ROPE].reshape(1, n_p, t_p, MLA_ROPE)
    w_keep = min(WINDOW, t_p)
    p_swa_k = k1.reshape(n_p, t_p, SWA_KV_HEADS, SWA_HD)[None, :, t_p - w_keep:]
    p_swa_v = v1.reshape(n_p, t_p, SWA_KV_HEADS, SWA_HD)[None, :, t_p - w_keep:]
    p_conv = jnp.stack([conv0[:, SUBLANE - 2:], conv1[:, SUBLANE - 2:]])

    xs = x_sample.reshape(n_s, D_MODEL)
    pos_s = np.full((n_s,), past_len)
    u_s, qs_big, ckv_s, kr_s = _even_proj(
        xs, _mla_rope_tables(pos_s), nm0, win, qnorm, wuq, kvnorm, wuk, wuv, tm=n_s, with_kv=False)
    ys_s5, s_hr, s_hi = _s5(u_s.reshape(1, n_s, S5_WIDTH), state_s5_re[0].reshape(n_s, S5_NSTATE),
                            state_s5_im[0].reshape(n_s, S5_NSTATE), *s5_consts,
                            nb=n_s, tt=1, transpose_io=False)
    q_lat = _qlat(qs_big, wk_t).reshape(n_s, MLA_HEADS, MLA_KV_LORA)
    q_rope = qs_big.reshape(n_s, MLA_HEADS, HEAD_PAD)[:, :, MLA_NOPE:MLA_NOPE + MLA_ROPE]
    kr_s32 = kr_s[:, MLA_NOPE:MLA_NOPE + MLA_ROPE]
    o_lat = _mla_decode(page_table, q_lat, q_rope, ckv_s.reshape(n_s, 1, MLA_KV_LORA),
                        kr_s32.reshape(n_s, 1, MLA_ROPE), cache_mla_ckv, cache_mla_krope)
    os_mla = _olat(o_lat.reshape(n_s, MLA_HEADS * MLA_KV_LORA), wv_big)
    xs1, g0 = _post(xs, [ys_s5.reshape(n_s, -1), os_mla], [wout_s5, wout_mla], *ffn[0],
                    n_seq=n_s, tm=n_s, conv_bufs=(state_ffn_conv[0, :, 0], state_ffn_conv[0, :, 1]))
    qs1, ks1, vs1 = _odd_proj(xs1, _swa_rope_tables(pos_s), nm1, owin, tm=n_s)
    expand = lambda a: jnp.repeat(a.reshape(n_s, SWA_KV_HEADS, SWA_HD), SWA_GQ, axis=1)
    os_swa = _swa_decode(sinks, qs1.reshape(n_s, SWA_HEADS, SWA_HD), expand(ks1), expand(vs1),
                         jnp.transpose(cache_swa_k[0], (2, 0, 1, 3)),
                         jnp.transpose(cache_swa_v[0], (2, 0, 1, 3)))
    ys, g1 = _post(xs1, [os_swa.reshape(n_s, -1)], [owout], *ffn[1], n_seq=n_s, tm=n_s,
                   conv_bufs=(state_ffn_conv[1, :, 0], state_ffn_conv[1, :, 1]), final_norm=fnorm)

    y_sample = ys.reshape(n_s, 1, D_MODEL)
    s_s5_re = s_hr.reshape(1, n_s, S5_GROUPS, S5_STATE)
    s_s5_im = s_hi.reshape(1, n_s, S5_GROUPS, S5_STATE)
    s_ckv = ckv_s.reshape(1, n_s, 1, MLA_KV_LORA)
    s_krope = kr_s32.reshape(1, n_s, 1, MLA_ROPE)
    s_swa_k = ks1.reshape(1, n_s, 1, SWA_KV_HEADS, SWA_HD)
    s_swa_v = vs1.reshape(1, n_s, 1, SWA_KV_HEADS, SWA_HD)
    s_conv = jnp.stack([jnp.stack([state_ffn_conv[0, :, 1], g0], axis=1),
                        jnp.stack([state_ffn_conv[1, :, 1], g1], axis=1)])

    return (y_prompt, y_sample, p_s5_re, p_s5_im, p_ckv, p_krope, p_swa_k, p_swa_v, p_conv,
            s_s5_re, s_s5_im, s_ckv, s_krope, s_swa_k, s_swa_v, s_conv)
```

```python
import functools
import math

import numpy as np
import jax
import jax.numpy as jnp
from jax import lax
from jax.experimental import pallas as pl
from jax.experimental.pallas import tpu as pltpu

F32 = jnp.float32
BF16 = jnp.bfloat16

D_MODEL = 1024
S5_WIDTH = 512
S5_GROUPS = 32
S5_GROUP = 16
S5_STATE = 64
S5_NSTATE = S5_GROUPS * S5_STATE
MLA_HEADS = 16
MLA_NOPE = 64
MLA_ROPE = 32
MLA_V = 64
MLA_Q_LORA = 384
MLA_KV_LORA = 256
MLA_THETA = 10000.0
MLA_SCALE = 1.0 / math.sqrt(MLA_NOPE + MLA_ROPE)
SWA_HEADS = 16
SWA_KV_HEADS = 4
SWA_HD = 64
SWA_GQ = SWA_HEADS // SWA_KV_HEADS
WINDOW = 128
ROT_DIM = SWA_HD // 4
ROPE_THETA = 500000.0
SWA_SCALE = 1.0 / math.sqrt(SWA_HD)
D_FF = 2816
PAGE = 128
EPS = 1e-6
NEG = -1e30

LANE = 128
SUBLANE = 8
HEAD_PAD = LANE
Z_EVEN = S5_WIDTH + MLA_Q_LORA + MLA_KV_LORA + LANE
QK_ODD = (SWA_HEADS + SWA_KV_HEADS) * SWA_HD
FF_CHUNK = 256
VMEM_LIMIT = 48 * 1024 * 1024


def _cparams(sem):
    return pltpu.CompilerParams(dimension_semantics=sem, vmem_limit_bytes=VMEM_LIMIT)


def _rms(x, g):
    return x * lax.rsqrt(jnp.mean(x * x, axis=-1, keepdims=True) + EPS) * g


def _const_spec(shape):
    nd = len(shape)
    return pl.BlockSpec(shape, lambda *_: (0,) * nd)


def _single_spec(shape):
    nd = len(shape)
    return pl.BlockSpec(shape, lambda *_: (0,) * nd, pipeline_mode=pl.Buffered(1))


def _mla_rope_tables(pos):
    half = MLA_ROPE // 2
    inv = MLA_THETA ** (-np.arange(half, dtype=np.float64) * 2.0 / MLA_ROPE)
    ang = np.asarray(pos, np.float64)[:, None] * inv[None, :]
    cos, sin = np.cos(ang), np.sin(ang)
    p = ang.shape[0]
    c = np.zeros((p, LANE)); sa = np.zeros((p, LANE)); sb = np.zeros((p, LANE))
    c[:, :MLA_NOPE] = 1.0
    c[:, MLA_NOPE:MLA_NOPE + half] = cos
    c[:, MLA_NOPE + half:MLA_NOPE + 2 * half] = cos
    sa[:, MLA_NOPE + half:MLA_NOPE + 2 * half] = sin
    sb[:, MLA_NOPE:MLA_NOPE + half] = -sin
    return tuple(jnp.asarray(t, F32) for t in (c, sa, sb))


def _swa_rope_tables(pos):
    half = ROT_DIM // 2
    inv = ROPE_THETA ** (-np.arange(half, dtype=np.float64) * 2.0 / ROT_DIM)
    ang = np.asarray(pos, np.float64)[:, None] * inv[None, :]
    cos, sin = np.cos(ang), np.sin(ang)
    p = ang.shape[0]
    c = np.ones((p, LANE)); sa = np.zeros((p, LANE)); sb = np.zeros((p, LANE))
    for o in (0, SWA_HD):
        c[:, o:o + half] = cos
        c[:, o + half:o + 2 * half] = cos
        sa[:, o + half:o + 2 * half] = sin
        sb[:, o:o + half] = -sin
    return tuple(jnp.asarray(t, F32) for t in (c, sa, sb))


def _rope_lanes(x, c, sa, sb, half):
    width = x.shape[1]
    reps = width // LANE
    if reps > 1:
        c = jnp.concatenate([c] * reps, axis=1)
        sa = jnp.concatenate([sa] * reps, axis=1)
        sb = jnp.concatenate([sb] * reps, axis=1)
    return x * c + pltpu.roll(x, half, 1) * sa + pltpu.roll(x, width - half, 1) * sb


def _even_proj_kernel(x_ref, c_ref, sa_ref, sb_ref, nm_ref, win_ref, qn_ref, wuq_ref, kvn_ref,
                      wuk_ref, wuv_ref, u_ref, q_ref, ckv_ref, kr_ref, *kv_refs):
    h = _rms(x_ref[...], nm_ref[...]).astype(BF16)
    z = jnp.dot(h, win_ref[...], preferred_element_type=F32)
    u_ref[...] = z[:, :S5_WIDTH]
    c, sa, sb = c_ref[...], sa_ref[...], sb_ref[...]
    o1 = S5_WIDTH + MLA_Q_LORA
    o2 = o1 + MLA_KV_LORA
    qn = _rms(z[:, S5_WIDTH:o1], qn_ref[...]).astype(BF16)
    q = jnp.dot(qn, wuq_ref[...], preferred_element_type=F32)
    q_ref[...] = _rope_lanes(q, c, sa, sb, MLA_ROPE // 2).astype(BF16)
    ckv = _rms(z[:, o1:o2], kvn_ref[...])
    ckv_ref[...] = ckv
    kr = _rope_lanes(z[:, o2:], c, sa, sb, MLA_ROPE // 2)
    kr_ref[...] = kr
    if kv_refs:
        k_ref, v_ref = kv_refs
        ckvb = ckv.astype(BF16)
        k = jnp.dot(ckvb, wuk_ref[...], preferred_element_type=F32)
        k_ref[...] = (k + jnp.concatenate([kr] * MLA_HEADS, axis=1)).astype(BF16)
        v_ref[...] = jnp.dot(ckvb, wuv_ref[...], preferred_element_type=F32).astype(BF16)


def _even_proj(x, tabs, nm, win, qnorm, wuq, kvnorm, wuk, wuv, *, tm, with_kv):
    rows = x.shape[0]
    nt = tabs[0].shape[0] // tm
    row = lambda w: pl.BlockSpec((tm, w), lambda i: (i, 0))
    tab = pl.BlockSpec((tm, LANE), lambda i: (i % nt, 0))
    qw = MLA_HEADS * HEAD_PAD
    out_shape = [jax.ShapeDtypeStruct((rows, S5_WIDTH), F32),
                 jax.ShapeDtypeStruct((rows, qw), BF16),
                 jax.ShapeDtypeStruct((rows, MLA_KV_LORA), F32),
                 jax.ShapeDtypeStruct((rows, LANE), F32)]
    out_specs = [row(S5_WIDTH), row(qw), row(MLA_KV_LORA), row(LANE)]
    if with_kv:
        out_shape += [jax.ShapeDtypeStruct((rows, qw), BF16),
                      jax.ShapeDtypeStruct((rows, MLA_HEADS * MLA_V), BF16)]
        out_specs += [row(qw), row(MLA_HEADS * MLA_V)]
    return pl.pallas_call(
        _even_proj_kernel,
        grid=(rows // tm,),
        in_specs=[row(D_MODEL), tab, tab, tab, _const_spec(nm.shape), _const_spec(win.shape),
                  _const_spec(qnorm.shape), _const_spec(wuq.shape), _const_spec(kvnorm.shape),
                  _const_spec(wuk.shape), _const_spec(wuv.shape)],
        out_specs=out_specs,
        out_shape=out_shape,
        compiler_params=_cparams(("arbitrary",)),
        name="even_proj",
    )(x, *tabs, nm, win, qnorm, wuq, kvnorm, wuk, wuv)


S5_STRIP = 512


def _s5_kernel(u_ref, h0r_ref, h0i_ref, lamr_ref, lami_ref, ldt_ref, wb_ref, wcr_ref, wci_ref,
               d_ref, wglu_ref, bglu_ref, y_ref, hr_ref, hi_ref, ut_ref, xr_ref, xi_ref, yt_ref,
               *, nb, tt, transpose_io):
    @pl.when(pl.program_id(0) == 0)
    def _():
        xr_ref[0:nb, :] = h0r_ref[...]
        xi_ref[0:nb, :] = h0i_ref[...]

    if transpose_io:
        for t in range(tt):
            ut_ref[t * nb:(t + 1) * nb, :] = u_ref[:, t, :]
    else:
        ut_ref[...] = u_ref[0]
    ub = ut_ref[...].astype(BF16)

    lr = jnp.minimum(lamr_ref[...], -1e-4)
    li = lami_ref[...]
    dt = jnp.exp(ldt_ref[...])
    mag = jnp.exp(lr * dt)
    ar = mag * jnp.cos(li * dt)
    ai = mag * jnp.sin(li * dt)
    den = lr * lr + li * li
    zr = ((ar - 1.0) * lr + ai * li) / den
    zi = (ai * lr - (ar - 1.0) * li) / den

    for s in range(S5_NSTATE // S5_STRIP):
        cols = slice(s * S5_STRIP, (s + 1) * S5_STRIP)
        br = jnp.dot(ub, wb_ref[:, cols], preferred_element_type=F32)
        bi = jnp.dot(ub, wb_ref[:, S5_NSTATE + s * S5_STRIP:S5_NSTATE + (s + 1) * S5_STRIP],
                     preferred_element_type=F32)
        zrs, zis = zr[:, cols], zi[:, cols]
        xr_ref[nb:, cols] = zrs * br - zis * bi
        xi_ref[nb:, cols] = zrs * bi + zis * br
        ars = jnp.broadcast_to(ar[:, cols], (nb, S5_STRIP))
        ais = jnp.broadcast_to(ai[:, cols], (nb, S5_STRIP))

        def step(t, carry):
            hr, hi = carry
            r0 = pl.multiple_of((t + 1) * nb, nb)
            nhr = ars * hr - ais * hi + xr_ref[pl.ds(r0, nb), cols]
            nhi = ars * hi + ais * hr + xi_ref[pl.ds(r0, nb), cols]
            xr_ref[pl.ds(r0, nb), cols] = nhr
            xi_ref[pl.ds(r0, nb), cols] = nhi
            return nhr, nhi

        hr, hi = lax.fori_loop(0, tt, step, (xr_ref[0:nb, cols], xi_ref[0:nb, cols]),
                               unroll=min(tt, 8))
        xr_ref[0:nb, cols] = hr
        xi_ref[0:nb, cols] = hi

    hr_ref[...] = xr_ref[0:nb, :]
    hi_ref[...] = xi_ref[0:nb, :]
    y = (jnp.dot(xr_ref[nb:, :].astype(BF16), wcr_ref[...], preferred_element_type=F32)
         + jnp.dot(xi_ref[nb:, :].astype(BF16), wci_ref[...], preferred_element_type=F32)
         + d_ref[...] * ut_ref[...])
    y = jax.nn.gelu(y)
    gate = jnp.dot(y.astype(BF16), wglu_ref[...], preferred_element_type=F32) + bglu_ref[...]
    y = y * jax.nn.sigmoid(gate)
    if transpose_io:
        groups = S5_WIDTH // LANE
        for c in range(groups):
            yt_ref[c] = y[:, c * LANE:(c + 1) * LANE]
        for n in range(nb):
            y_ref[n] = jnp.concatenate(
                [yt_ref[c, pl.ds(n, tt, stride=nb), :] for c in range(groups)],
                axis=1).astype(y_ref.dtype)
    else:
        y_ref[0] = y.astype(y_ref.dtype)


def _s5(u, h0r, h0i, lamr, lami, ldt, wb, wcr, wci, d, wglu, bglu, *, nb, tt, transpose_io):
    t_total = u.shape[1] if transpose_io else u.shape[0]
    if transpose_io:
        u_spec = pl.BlockSpec((nb, tt, S5_WIDTH), lambda i: (0, i, 0))
        y_shape = (nb, t_total, S5_WIDTH)
    else:
        u_spec = pl.BlockSpec((tt, nb, S5_WIDTH), lambda i: (i, 0, 0))
        y_shape = (t_total, nb, S5_WIDTH)
    consts = (h0r, h0i, lamr, lami, ldt, wb, wcr, wci, d, wglu, bglu)
    kern = functools.partial(_s5_kernel, nb=nb, tt=tt, transpose_io=transpose_io)
    return pl.pallas_call(
        kern,
        grid=(t_total // tt,),
        in_specs=[u_spec] + [_const_spec(c.shape) for c in consts],
        out_specs=[u_spec, _const_spec((nb, S5_NSTATE)), _const_spec((nb, S5_NSTATE))],
        out_shape=[jax.ShapeDtypeStruct(y_shape, BF16),
                   jax.ShapeDtypeStruct((nb, S5_NSTATE), F32),
                   jax.ShapeDtypeStruct((nb, S5_NSTATE), F32)],
        scratch_shapes=[pltpu.VMEM((nb * tt, S5_WIDTH), F32),
                        pltpu.VMEM((nb * (tt + 1), S5_NSTATE), F32),
                        pltpu.VMEM((nb * (tt + 1), S5_NSTATE), F32),
                        pltpu.VMEM((S5_WIDTH // LANE, nb * tt, LANE), F32)],
        compiler_params=_cparams(("arbitrary",)),
        name="s5_mixer",
    )(u, *consts)


def _mla_attn_kernel(q_ref, k_ref, v_ref, o_ref, m_ref, l_ref, acc_ref, *, tq, wide):
    qi = pl.program_id(1)
    m_ref[...] = jnp.full(m_ref.shape, NEG, F32)
    l_ref[...] = jnp.zeros(l_ref.shape, F32)
    acc_ref[...] = jnp.zeros(acc_ref.shape, F32)
    row = lax.broadcasted_iota(jnp.int32, (tq, tq), 0)
    col = lax.broadcasted_iota(jnp.int32, (tq, tq), 1)
    causal = col <= row

    def process(k0, width, masked):
        keys = pl.ds(k0, width)
        for j in range(MLA_HEADS // 2):
            vp = v_ref[0, keys, j * LANE:(j + 1) * LANE]
            for h in (2 * j, 2 * j + 1):
                hl = slice(h * HEAD_PAD, (h + 1) * HEAD_PAD)
                s = lax.dot_general(q_ref[0, :, hl], k_ref[0, keys, hl], (((1,), (1,)), ((), ())),
                                    preferred_element_type=F32) * MLA_SCALE
                if masked:
                    s = jnp.where(causal, s, NEG)
                m_old = m_ref[h]
                m_new = jnp.maximum(m_old, jnp.max(s, axis=1, keepdims=True))
                alpha = jnp.exp(m_old - m_new)
                p = jnp.exp(s - m_new)
                l_ref[h] = alpha * l_ref[h] + jnp.sum(p, axis=1, keepdims=True)
                m_ref[h] = m_new
                acc_ref[h] = alpha * acc_ref[h] + jnp.dot(p.astype(BF16), vp,
                                                          preferred_element_type=F32)

    per_wide = wide // tq
    n_wide = qi // per_wide

    def wide_step(i, carry):
        process(pl.multiple_of(i * wide, wide), wide, False)
        return carry

    lax.fori_loop(0, n_wide, wide_step, 0)
    for r in range(1, per_wide):
        @pl.when(qi - n_wide * per_wide >= r)
        def _(r=r):
            process(pl.multiple_of((n_wide * per_wide + r - 1) * tq, tq), tq, False)
    process(pl.multiple_of(qi * tq, tq), tq, True)

    lane = lax.broadcasted_iota(jnp.int32, (tq, LANE), 1)
    for j in range(MLA_HEADS // 2):
        o = jnp.where(lane < MLA_V, acc_ref[2 * j] / l_ref[2 * j],
                      acc_ref[2 * j + 1] / l_ref[2 * j + 1])
        o_ref[0, :, j * LANE:(j + 1) * LANE] = o.astype(o_ref.dtype)


def _mla_attn(q, k, v, *, tq, wide):
    n, t, _ = q.shape
    kern = functools.partial(_mla_attn_kernel, tq=tq, wide=wide)
    return pl.pallas_call(
        kern,
        grid=(n, t // tq),
        in_specs=[pl.BlockSpec((1, tq, q.shape[2]), lambda b, i: (b, i, 0)),
                  pl.BlockSpec((1, t, k.shape[2]), lambda b, i: (b, 0, 0)),
                  pl.BlockSpec((1, t, v.shape[2]), lambda b, i: (b, 0, 0))],
        out_specs=pl.BlockSpec((1, tq, MLA_HEADS * MLA_V), lambda b, i: (b, i, 0)),
        out_shape=jax.ShapeDtypeStruct((n, t, MLA_HEADS * MLA_V), BF16),
        scratch_shapes=[pltpu.VMEM((MLA_HEADS, tq, 1), F32),
                        pltpu.VMEM((MLA_HEADS, tq, 1), F32),
                        pltpu.VMEM((MLA_HEADS, tq, LANE), F32)],
        compiler_params=_cparams(("arbitrary", "arbitrary")),
        name="mla_prompt_attn",
    )(q, k, v)


DEC_PAGES = 32
DEC_SUB = 4
DEC_KEYS = DEC_PAGES * PAGE


def _qlat_kernel(q_ref, wk_ref, o_ref):
    for h in range(MLA_HEADS):
        o_ref[:, h * MLA_KV_LORA:(h + 1) * MLA_KV_LORA] = jnp.dot(
            q_ref[:, h * HEAD_PAD:(h + 1) * HEAD_PAD], wk_ref[h], preferred_element_type=F32)


def _qlat(q_big, wk_t):
    rows = q_big.shape[0]
    return pl.pallas_call(
        _qlat_kernel,
        grid=(1,),
        in_specs=[_const_spec(q_big.shape), _const_spec(wk_t.shape)],
        out_specs=_const_spec((rows, MLA_HEADS * MLA_KV_LORA)),
        out_shape=jax.ShapeDtypeStruct((rows, MLA_HEADS * MLA_KV_LORA), F32),
        compiler_params=_cparams(("arbitrary",)),
        name="mla_q_absorb",
    )(q_big, wk_t)


def _olat_kernel(o_ref, wv_ref, out_ref):
    for j in range(MLA_HEADS // 2):
        acc = None
        for h in (2 * j, 2 * j + 1):
            part = jnp.dot(o_ref[:, h * MLA_KV_LORA:(h + 1) * MLA_KV_LORA].astype(BF16),
                           wv_ref[:, h * LANE:(h + 1) * LANE], preferred_element_type=F32)
            acc = part if acc is None else acc + part
        out_ref[:, j * LANE:(j + 1) * LANE] = acc.astype(out_ref.dtype)


def _olat(o_lat, wv_big):
    rows = o_lat.shape[0]
    return pl.pallas_call(
        _olat_kernel,
        grid=(1,),
        in_specs=[_const_spec(o_lat.shape), _const_spec(wv_big.shape)],
        out_specs=_const_spec((rows, MLA_HEADS * MLA_V)),
        out_shape=jax.ShapeDtypeStruct((rows, MLA_HEADS * MLA_V), BF16),
        compiler_params=_cparams(("arbitrary",)),
        name="mla_v_expand",
    )(o_lat, wv_big)


def _page_copies(cache_ckv, cache_kr, ckv_buf, kr_buf, sem, slot, page, p):
    rows = pl.ds(p * PAGE, PAGE)
    return (pltpu.make_async_copy(cache_ckv.at[0, page], ckv_buf.at[slot, rows], sem.at[0, slot]),
            pltpu.make_async_copy(cache_kr.at[0, page], kr_buf.at[slot, p], sem.at[1, slot]))


def _mla_decode_kernel(ptc_ref, ptn_ref, ql_ref, qr_ref, ckvn_ref, krn_ref, cache_ckv, cache_kr,
                       o_ref, ckv_buf, kr_buf, sem, *, n_seq, n_chunks):
    b = pl.program_id(0)

    def start_page(page, slot, p):
        for cp in _page_copies(cache_ckv, cache_kr, ckv_buf, kr_buf, sem, slot, page, p):
            cp.start()

    def wait_chunk(slot):
        for p in range(DEC_PAGES):
            for cp in _page_copies(cache_ckv, cache_kr, ckv_buf, kr_buf, sem, slot, 0, p):
                cp.wait()

    @pl.when(b == 0)
    def _():
        for p in range(DEC_PAGES):
            start_page(ptc_ref[0, 0, p], 0, p)

    ql = ql_ref[0].astype(BF16)
    qr = qr_ref[0].astype(BF16)

    def chunk_body(c, carry):
        m, l, acc = carry
        slot = lax.rem(c, 2)
        nxt = 1 - slot
        wait_chunk(slot)
        wraps = c + 1 == n_chunks
        nxt_base = jnp.where(wraps, 0, (c + 1) * DEC_PAGES)

        def next_page(p):
            return jnp.where(wraps, ptn_ref[0, 0, p], ptc_ref[0, 0, nxt_base + p])

        kbs, scores = [], []
        for i in range(DEC_PAGES // DEC_SUB):
            kb = ckv_buf[slot, i * DEC_SUB * PAGE:(i + 1) * DEC_SUB * PAGE, :].astype(BF16)
            krt = jnp.concatenate([kr_buf[slot, i * DEC_SUB + p] for p in range(DEC_SUB)],
                                  axis=1).astype(BF16)
            kbs.append(kb)
            scores.append(
                (lax.dot_general(ql, kb, (((1,), (1,)), ((), ())), preferred_element_type=F32)
                 + jnp.dot(qr, krt, preferred_element_type=F32)) * MLA_SCALE)
            for p in range(i * DEC_SUB, (i + 1) * DEC_SUB):
                start_page(next_page(p), nxt, p)
        for kb, s in zip(kbs, scores):
            m_new = jnp.maximum(m, jnp.max(s, axis=1, keepdims=True))
            alpha = jnp.exp(m - m_new)
            pr = jnp.exp(s - m_new)
            l = alpha * l + jnp.sum(pr, axis=1, keepdims=True)
            acc = alpha * acc + jnp.dot(pr.astype(BF16), kb, preferred_element_type=F32)
            m = m_new
        return m, l, acc

    init = (jnp.full((MLA_HEADS, 1), NEG, F32), jnp.zeros((MLA_HEADS, 1), F32),
            jnp.zeros((MLA_HEADS, MLA_KV_LORA), F32))
    m, l, acc = lax.fori_loop(0, n_chunks, chunk_body, init)

    @pl.when(b == n_seq - 1)
    def _():
        wait_chunk(n_chunks % 2)

    cn = ckvn_ref[0].astype(BF16).astype(F32)
    kn = krn_ref[0].astype(BF16).astype(F32)
    s_new = (jnp.sum(ql.astype(F32) * cn, axis=1, keepdims=True)
             + jnp.sum(qr.astype(F32) * kn, axis=1, keepdims=True)) * MLA_SCALE
    m_new = jnp.maximum(m, s_new)
    alpha = jnp.exp(m - m_new)
    p_new = jnp.exp(s_new - m_new)
    l = alpha * l + p_new
    acc = alpha * acc + p_new.astype(BF16).astype(F32) * cn
    o_ref[0] = acc / l


def _mla_decode(page_table, q_lat, q_rope, ckv_new, kr_new, cache_ckv, cache_kr):
    n_seq, n_pages = page_table.shape
    n_chunks = n_pages // DEC_PAGES
    assert n_pages % DEC_PAGES == 0 and n_chunks % 2 == 0
    pt = page_table.reshape(n_seq, 1, n_pages)
    smem_row = lambda f: pl.BlockSpec((1, 1, n_pages), f, memory_space=pltpu.SMEM)
    per_seq = lambda a: pl.BlockSpec((1,) + a.shape[1:], lambda b: (b, 0, 0))
    kern = functools.partial(_mla_decode_kernel, n_seq=n_seq, n_chunks=n_chunks)
    return pl.pallas_call(
        kern,
        grid=(n_seq,),
        in_specs=[smem_row(lambda b: (b, 0, 0)),
                  smem_row(lambda b: (jnp.minimum(b + 1, n_seq - 1), 0, 0)),
                  per_seq(q_lat), per_seq(q_rope), per_seq(ckv_new), per_seq(kr_new),
                  pl.BlockSpec(memory_space=pl.ANY), pl.BlockSpec(memory_space=pl.ANY)],
        out_specs=pl.BlockSpec((1, MLA_HEADS, MLA_KV_LORA), lambda b: (b, 0, 0)),
        out_shape=jax.ShapeDtypeStruct((n_seq, MLA_HEADS, MLA_KV_LORA), F32),
        scratch_shapes=[pltpu.VMEM((2, DEC_KEYS, MLA_KV_LORA), F32),
                        pltpu.VMEM((2, DEC_PAGES, MLA_ROPE, PAGE), F32),
                        pltpu.SemaphoreType.DMA((2, 2))],
        compiler_params=_cparams(("arbitrary",)),
        name="mla_paged_decode",
    )(pt, pt, q_lat, q_rope, ckv_new, kr_new, cache_ckv, cache_kr)


def _odd_proj_kernel(x_ref, c_ref, sa_ref, sb_ref, nm_ref, win_ref, q_ref, k_ref, v_ref):
    h = _rms(x_ref[...], nm_ref[...]).astype(BF16)
    z = jnp.dot(h, win_ref[...], preferred_element_type=F32)
    qk = _rope_lanes(z[:, :QK_ODD], c_ref[...], sa_ref[...], sb_ref[...], ROT_DIM // 2)
    q_ref[...] = qk[:, :SWA_HEADS * SWA_HD].astype(BF16)
    k_ref[...] = qk[:, SWA_HEADS * SWA_HD:]
    v_ref[...] = z[:, QK_ODD:]


def _odd_proj(x, tabs, nm, win, *, tm):
    rows = x.shape[0]
    nt = tabs[0].shape[0] // tm
    row = lambda w: pl.BlockSpec((tm, w), lambda i: (i, 0))
    tab = pl.BlockSpec((tm, LANE), lambda i: (i % nt, 0))
    kvw = SWA_KV_HEADS * SWA_HD
    return pl.pallas_call(
        _odd_proj_kernel,
        grid=(rows // tm,),
        in_specs=[row(D_MODEL), tab, tab, tab, _const_spec(nm.shape), _const_spec(win.shape)],
        out_specs=[row(SWA_HEADS * SWA_HD), row(kvw), row(kvw)],
        out_shape=[jax.ShapeDtypeStruct((rows, SWA_HEADS * SWA_HD), BF16),
                   jax.ShapeDtypeStruct((rows, kvw), F32),
                   jax.ShapeDtypeStruct((rows, kvw), F32)],
        compiler_params=_cparams(("arbitrary",)),
        name="odd_proj",
    )(x, *tabs, nm, win)


def _swa_attn_kernel(sink_ref, q_ref, kp_ref, kc_ref, vp_ref, vc_ref, o_ref):
    b = pl.program_id(1)
    kk = jnp.concatenate([kp_ref[0], kc_ref[0]], axis=0).astype(BF16)
    vv = jnp.concatenate([vp_ref[0], vc_ref[0]], axis=0).astype(BF16)
    rows = SWA_GQ * WINDOW
    r = lax.broadcasted_iota(jnp.int32, (rows, 2 * WINDOW), 0)
    j = lax.broadcasted_iota(jnp.int32, (rows, 2 * WINDOW), 1)
    rel = jnp.bitwise_and(r, WINDOW - 1) + WINDOW - j
    valid = (rel >= 0) & (rel <= WINDOW) & ((j >= WINDOW) | (b > 0))
    rcol = lax.broadcasted_iota(jnp.int32, (rows, 1), 0)
    outs = []
    for kh in range(SWA_KV_HEADS):
        qg = jnp.concatenate(
            [q_ref[0, :, (kh * SWA_GQ + g) * SWA_HD:(kh * SWA_GQ + g + 1) * SWA_HD]
             for g in range(SWA_GQ)], axis=0)
        khd = kk[:, kh * SWA_HD:(kh + 1) * SWA_HD]
        vhd = vv[:, kh * SWA_HD:(kh + 1) * SWA_HD]
        s = lax.dot_general(qg, khd, (((1,), (1,)), ((), ())),
                            preferred_element_type=F32) * SWA_SCALE
        s = jnp.where(valid, s, NEG)
        sink = jnp.full((rows, 1), sink_ref[kh * SWA_GQ + SWA_GQ - 1], F32)
        for g in range(SWA_GQ - 2, -1, -1):
            sink = jnp.where(rcol < (g + 1) * WINDOW, sink_ref[kh * SWA_GQ + g], sink)
        m = jnp.maximum(jnp.max(s, axis=1, keepdims=True), sink)
        p = jnp.exp(s - m)
        l = jnp.sum(p, axis=1, keepdims=True) + jnp.exp(sink - m)
        o = jnp.dot(p.astype(BF16), vhd, preferred_element_type=F32) / l
        outs += [o[g * WINDOW:(g + 1) * WINDOW] for g in range(SWA_GQ)]
    o_ref[0] = jnp.concatenate(outs, axis=1).astype(o_ref.dtype)


def _swa_attn(sinks, q, k, v):
    n, t, _ = q.shape
    kvw = SWA_KV_HEADS * SWA_HD
    prev = pl.BlockSpec((1, WINDOW, kvw), lambda a, i: (a, jnp.maximum(i - 1, 0), 0))
    cur = pl.BlockSpec((1, WINDOW, kvw), lambda a, i: (a, i, 0))
    qspec = pl.BlockSpec((1, WINDOW, SWA_HEADS * SWA_HD), lambda a, i: (a, i, 0))
    return pl.pallas_call(
        _swa_attn_kernel,
        grid=(n, t // WINDOW),
        in_specs=[pl.BlockSpec(memory_space=pltpu.SMEM), qspec, prev, cur, prev, cur],
        out_specs=qspec,
        out_shape=jax.ShapeDtypeStruct((n, t, SWA_HEADS * SWA_HD), BF16),
        compiler_params=_cparams(("arbitrary", "arbitrary")),
        name="swa_prompt_attn",
    )(sinks, q, k, k, v, v)


SWA_DEC_SEQS = 8


def _swa_decode_kernel(sink_ref, q_ref, kn_ref, vn_ref, ck_ref, cv_ref, o_ref):
    q = q_ref[...]
    hgrp = lax.broadcasted_iota(jnp.int32, (1, SWA_HEADS, 1), 1) // SWA_GQ
    hidx = lax.broadcasted_iota(jnp.int32, (1, SWA_HEADS, 1), 1)
    sink = jnp.zeros((1, SWA_HEADS, 1), F32)
    for h in range(SWA_HEADS):
        sink = jnp.where(hidx == h, sink_ref[h], sink)
    s = jnp.zeros((q.shape[0], SWA_HEADS, WINDOW), F32)
    for kh in range(SWA_KV_HEADS):
        skh = jnp.einsum('nhd,ndj->nhj', q, ck_ref[:, kh].astype(BF16),
                         preferred_element_type=F32)
        s = jnp.where(hgrp == kh, skh, s)
    s = s * SWA_SCALE
    qf = q.astype(F32)
    kn = kn_ref[...].astype(BF16).astype(F32)
    vn = vn_ref[...].astype(BF16).astype(F32)
    s_new = jnp.sum(qf * kn, axis=2, keepdims=True) * SWA_SCALE
    m = jnp.maximum(jnp.maximum(jnp.max(s, axis=2, keepdims=True), s_new), sink)
    p = jnp.exp(s - m)
    p_new = jnp.exp(s_new - m)
    l = jnp.sum(p, axis=2, keepdims=True) + p_new + jnp.exp(sink - m)
    o = p_new.astype(BF16).astype(F32) * vn
    for kh in range(SWA_KV_HEADS):
        pk = jnp.where(hgrp == kh, p, 0.0).astype(BF16)
        o = o + jnp.einsum('nhj,ndj->nhd', pk, cv_ref[:, kh].astype(BF16),
                           preferred_element_type=F32)
    o_ref[...] = (o / l).astype(o_ref.dtype)


def _swa_decode(sinks, q, k_new, v_new, cache_k, cache_v):
    n = q.shape[0]
    nb = SWA_DEC_SEQS
    seq = pl.BlockSpec((nb, SWA_HEADS, SWA_HD), lambda i: (i, 0, 0))
    cache = pl.BlockSpec((nb, SWA_KV_HEADS, SWA_HD, WINDOW), lambda i: (i, 0, 0, 0))
    return pl.pallas_call(
        _swa_decode_kernel,
        grid=(n // nb,),
        in_specs=[pl.BlockSpec(memory_space=pltpu.SMEM), seq, seq, seq, cache, cache],
        out_specs=seq,
        out_shape=jax.ShapeDtypeStruct((n, SWA_HEADS, SWA_HD), BF16),
        compiler_params=_cparams(("arbitrary",)),
        name="swa_decode_attn",
    )(sinks, q, k_new, v_new, cache_k, cache_v)


def _post_kernel(*refs, n_mix, decode, final, tm):
    it = iter(refs)
    x_ref = next(it)
    mix_refs = [next(it) for _ in range(n_mix)]
    wo_refs = [next(it) for _ in range(n_mix)]
    nf_ref, wg_ref, wu_ref, cw_ref, cb_ref, wd_ref = (next(it) for _ in range(6))
    if decode:
        buf0_ref, buf1_ref = next(it), next(it)
    fn_ref = next(it) if final else None
    y_ref = next(it)
    g_ref = next(it)
    act_ref = next(it)
    carry_ref = None if decode else next(it)

    x1 = x_ref[...]
    for a_ref, w_ref in zip(mix_refs, wo_refs):
        x1 = x1 + jnp.dot(a_ref[...], w_ref[...], preferred_element_type=F32)
    h2 = _rms(x1, nf_ref[...]).astype(BF16)

    if not decode:
        @pl.when(pl.program_id(1) == 0)
        def _():
            carry_ref[...] = jnp.zeros_like(carry_ref)
        row = lax.broadcasted_iota(jnp.int32, (tm, FF_CHUNK), 0)

    for c in range(D_FF // FF_CHUNK):
        sl = slice(c * FF_CHUNK, (c + 1) * FF_CHUNK)
        g = jnp.dot(h2, wg_ref[:, sl], preferred_element_type=F32)
        u = jnp.dot(h2, wu_ref[:, sl], preferred_element_type=F32)
        if decode:
            gm2, gm1 = buf0_ref[:, sl], buf1_ref[:, sl]
            g_ref[:, sl] = g
        else:
            prev = carry_ref[:, sl]
            p6, p7 = prev[SUBLANE - 2:SUBLANE - 1, :], prev[SUBLANE - 1:SUBLANE, :]
            gm1 = jnp.where(row == 0, p7, pltpu.roll(g, 1, 0))
            gm2 = jnp.where(row == 0, p6, jnp.where(row == 1, p7, pltpu.roll(g, 2, 0)))
            last = g[tm - SUBLANE:tm, :]
            carry_ref[:, sl] = last
            g_ref[0, :, sl] = last
        cc = cb_ref[:, sl] + cw_ref[0:1, sl] * gm2 + cw_ref[1:2, sl] * gm1 + cw_ref[2:3, sl] * g
        act_ref[:, sl] = (jax.nn.gelu(cc) * u).astype(BF16)

    x2 = x1 + jnp.dot(act_ref[...], wd_ref[...], preferred_element_type=F32)
    y_ref[...] = _rms(x2, fn_ref[...]) if final else x2


def _post(x, mixes, wos, nf, wg, wu, cw, cb, wd, *, n_seq, tm, conv_bufs=None, final_norm=None):
    rows = x.shape[0]
    decode = conv_bufs is not None
    nt = rows // n_seq // tm if not decode else rows // tm
    grid = (1, nt) if decode else (n_seq, nt)
    row = lambda w: pl.BlockSpec((tm, w), lambda a, i: (a * nt + i, 0))
    args = [x] + list(mixes) + list(wos) + [nf, wg, wu, cw, cb, wd]
    in_specs = ([row(D_MODEL)] + [row(m.shape[1]) for m in mixes]
                + [_single_spec(w.shape) for w in wos]
                + [_const_spec(nf.shape), _single_spec(wg.shape), _single_spec(wu.shape),
                   _const_spec(cw.shape), _const_spec(cb.shape), _single_spec(wd.shape)])
    if decode:
        args += list(conv_bufs)
        in_specs += [row(D_FF), row(D_FF)]
    if final_norm is not None:
        args.append(final_norm)
        in_specs.append(_const_spec(final_norm.shape))
    if decode:
        g_shape = jax.ShapeDtypeStruct((rows, D_FF), F32)
        g_spec = row(D_FF)
    else:
        g_shape = jax.ShapeDtypeStruct((n_seq, SUBLANE, D_FF), F32)
        g_spec = pl.BlockSpec((1, SUBLANE, D_FF), lambda a, i: (a, 0, 0))
    scratch = [pltpu.VMEM((tm, D_FF), BF16)]
    if not decode:
        scratch.append(pltpu.VMEM((SUBLANE, D_FF), F32))
    kern = functools.partial(_post_kernel, n_mix=len(mixes), decode=decode,
                             final=final_norm is not None, tm=tm)
    return pl.pallas_call(
        kern,
        grid=grid,
        in_specs=in_specs,
        out_specs=[row(D_MODEL), g_spec],
        out_shape=[jax.ShapeDtypeStruct((rows, D_MODEL), F32), g_shape],
        scratch_shapes=scratch,
        compiler_params=_cparams(("arbitrary", "arbitrary")),
        name="post_decode" if decode else "post_prompt",
    )(*args)


def _prep_even(e_w_in, e_w_uq, e_w_uk, e_w_uv, e_s5_b_re, e_s5_b_im, e_s5_c_re, e_s5_c_im):
    o1 = S5_WIDTH
    o2 = o1 + MLA_Q_LORA
    o3 = o2 + MLA_KV_LORA
    zpad = lambda n: jnp.zeros((D_MODEL, n), F32)
    win = jnp.concatenate([e_w_in[:, :o3], zpad(MLA_NOPE), e_w_in[:, o3:],
                           zpad(HEAD_PAD - MLA_NOPE - MLA_ROPE)], axis=1).astype(BF16)
    pad_last = lambda w, n: jnp.pad(w, ((0, 0), (0, 0), (0, n - w.shape[2])))
    wuq = pad_last(e_w_uq, HEAD_PAD).reshape(MLA_Q_LORA, MLA_HEADS * HEAD_PAD).astype(BF16)
    wuk = pad_last(e_w_uk, HEAD_PAD).reshape(MLA_KV_LORA, MLA_HEADS * HEAD_PAD).astype(BF16)
    wuv = e_w_uv.reshape(MLA_KV_LORA, MLA_HEADS * MLA_V).astype(BF16)
    wk_t = jnp.pad(jnp.transpose(e_w_uk, (1, 2, 0)),
                   ((0, 0), (0, HEAD_PAD - MLA_NOPE), (0, 0))).astype(BF16)
    wv4 = e_w_uv.reshape(MLA_KV_LORA, MLA_HEADS // 2, 2, MLA_V)
    zv = jnp.zeros_like(wv4[:, :, 0])
    wv_big = jnp.stack([jnp.concatenate([wv4[:, :, 0], zv], axis=-1),
                        jnp.concatenate([zv, wv4[:, :, 1]], axis=-1)], axis=2)
    wv_big = wv_big.reshape(MLA_KV_LORA, MLA_HEADS * LANE).astype(BF16)
    eye = jnp.eye(S5_GROUPS, dtype=F32)
    bd_in = lambda b: jnp.einsum('gph,gk->ghkp', b, eye).reshape(S5_WIDTH, S5_NSTATE)
    bd_out = lambda c: jnp.einsum('ghp,gk->gpkh', c, eye).reshape(S5_NSTATE, S5_WIDTH)
    wb = jnp.concatenate([bd_in(e_s5_b_re), bd_in(e_s5_b_im)], axis=1).astype(BF16)
    wcr = bd_out(e_s5_c_re).astype(BF16)
    wci = (-bd_out(e_s5_c_im)).astype(BF16)
    return win, wuq, wuk, wuv, wk_t, wv_big, wb, wcr, wci


def kernel(x_prompt, x_sample, page_table, state_s5_re, state_s5_im, cache_mla_ckv, cache_mla_krope,
           cache_swa_k, cache_swa_v, state_ffn_conv, norm_mix, norm_ffn, final_norm, e_w_in,
           e_s5_lam_re, e_s5_lam_im, e_s5_log_dt, e_s5_b_re, e_s5_b_im, e_s5_c_re, e_s5_c_im, e_s5_d,
           e_s5_w_glu, e_s5_b_glu, e_q_norm, e_w_uq, e_kv_norm, e_w_uk, e_w_uv, e_w_out, o_w_in,
           o_sinks, o_w_out, f_w_gate, f_w_up, f_conv_w, f_conv_b, f_w_down):
    n_p, t_p, _ = x_prompt.shape
    n_s, t_s, _ = x_sample.shape
    assert t_s == 1
    past_len = page_table.shape[1] * PAGE
    row2 = lambda v: v.reshape(1, -1)

    (win, wuq, wuk, wuv, wk_t, wv_big, wb, wcr, wci) = _prep_even(
        e_w_in[0], e_w_uq[0], e_w_uk[0], e_w_uv[0], e_s5_b_re[0], e_s5_b_im[0],
        e_s5_c_re[0], e_s5_c_im[0])
    lamr, lami = row2(e_s5_lam_re[0]), row2(e_s5_lam_im[0])
    ldt = row2(jnp.repeat(e_s5_log_dt[0], S5_STATE))
    s5_consts = (lamr, lami, ldt, wb, wcr, wci, row2(e_s5_d[0]), e_s5_w_glu[0].astype(BF16),
                 row2(e_s5_b_glu[0]))
    wout_s5 = e_w_out[0][:S5_WIDTH].astype(BF16)
    wout_mla = e_w_out[0][S5_WIDTH:].astype(BF16)
    owin = o_w_in[0].astype(BF16)
    owout = o_w_out[0].astype(BF16)
    ffn = [(row2(norm_ffn[l]), f_w_gate[l].astype(BF16), f_w_up[l].astype(BF16), f_conv_w[l],
            row2(f_conv_b[l]), f_w_down[l].astype(BF16)) for l in range(2)]
    nm0, nm1 = row2(norm_mix[0]), row2(norm_mix[1])
    qnorm, kvnorm = row2(e_q_norm[0]), row2(e_kv_norm[0])
    fnorm = row2(final_norm)
    sinks = o_sinks[0]

    xp = x_prompt.reshape(n_p * t_p, D_MODEL)
    pos_p = np.arange(t_p)
    u, q_big, ckv_p, kr_p, k_big, v_p = _even_proj(
        xp, _mla_rope_tables(pos_p), nm0, win, qnorm, wuq, kvnorm, wuk, wuv, tm=256, with_kv=True)
    zeros_state = jnp.zeros((n_p, S5_NSTATE), F32)
    y_s5, p_hr, p_hi = _s5(u.reshape(n_p, t_p, S5_WIDTH), zeros_state, zeros_state, *s5_consts,
                           nb=n_p, tt=64, transpose_io=True)
    o_mla = _mla_attn(q_big.reshape(n_p, t_p, -1), k_big.reshape(n_p, t_p, -1),
                      v_p.reshape(n_p, t_p, -1), tq=256, wide=512)
    x1, conv0 = _post(xp, [y_s5.reshape(n_p * t_p, -1), o_mla.reshape(n_p * t_p, -1)],
                      [wout_s5, wout_mla], *ffn[0], n_seq=n_p, tm=256)
    q1, k1, v1 = _odd_proj(x1, _swa_rope_tables(pos_p), nm1, owin, tm=256)
    kvw = SWA_KV_HEADS * SWA_HD
    o_swa = _swa_attn(sinks, q1.reshape(n_p, t_p, -1), k1.reshape(n_p, t_p, kvw),
                      v1.reshape(n_p, t_p, kvw))
    y_p, conv1 = _post(x1, [o_swa.reshape(n_p * t_p, -1)], [owout], *ffn[1], n_seq=n_p, tm=256,
                       final_norm=fnorm)

    y_prompt = y_p.reshape(n_p, t_p, D_MODEL)
    p_s5_re = p_hr.reshape(1, n_p, S5_GROUPS, S5_STATE)
    p_s5_im = p_hi.reshape(1, n_p, S5_GROUPS, S5_STATE)
    p_ckv = ckv_p.reshape(1, n_p, t_p, MLA_KV_LORA)
    p_krope = kr_p[:, MLA_NOPE:MLA_NOPE + MLA_ROPE].reshape(1, n_p, t_p, MLA_ROPE)
    w_keep = min(WINDOW, t_p)
    p_swa_k = k1.reshape(n_p, t_p, SWA_KV_HEADS, SWA_HD)[None, :, t_p - w_keep:]
    p_swa_v = v1.reshape(n_p, t_p, SWA_KV_HEADS, SWA_HD)[None, :, t_p - w_keep:]
    p_conv = jnp.stack([conv0[:, SUBLANE - 2:], conv1[:, SUBLANE - 2:]])

    xs = x_sample.reshape(n_s, D_MODEL)
    pos_s = np.full((n_s,), past_len)
    u_s, qs_big, ckv_s, kr_s = _even_proj(
        xs, _mla_rope_tables(pos_s), nm0, win, qnorm, wuq, kvnorm, wuk, wuv, tm=n_s, with_kv=False)
    ys_s5, s_hr, s_hi = _s5(u_s.reshape(1, n_s, S5_WIDTH), state_s5_re[0].reshape(n_s, S5_NSTATE),
                            state_s5_im[0].reshape(n_s, S5_NSTATE), *s5_consts,
                            nb=n_s, tt=1, transpose_io=False)
    q_lat = _qlat(qs_big, wk_t).reshape(n_s, MLA_HEADS, MLA_KV_LORA)
    q_rope = qs_big.reshape(n_s, MLA_HEADS, HEAD_PAD)[:, :, MLA_NOPE:MLA_NOPE + MLA_ROPE]
    kr_s32 = kr_s[:, MLA_NOPE:MLA_NOPE + MLA_ROPE]
    o_lat = _mla_decode(page_table, q_lat, q_rope, ckv_s.reshape(n_s, 1, MLA_KV_LORA),
                        kr_s32.reshape(n_s, 1, MLA_ROPE), cache_mla_ckv,
                        jnp.swapaxes(cache_mla_krope, 2, 3))
    os_mla = _olat(o_lat.reshape(n_s, MLA_HEADS * MLA_KV_LORA), wv_big)
    xs1, g0 = _post(xs, [ys_s5.reshape(n_s, -1), os_mla], [wout_s5, wout_mla], *ffn[0],
                    n_seq=n_s, tm=n_s, conv_bufs=(state_ffn_conv[0, :, 0], state_ffn_conv[0, :, 1]))
    qs1, ks1, vs1 = _odd_proj(xs1, _swa_rope_tables(pos_s), nm1, owin, tm=n_s)
    expand = lambda a: jnp.repeat(a.reshape(n_s, SWA_KV_HEADS, SWA_HD), SWA_GQ, axis=1)
    os_swa = _swa_decode(sinks, qs1.reshape(n_s, SWA_HEADS, SWA_HD), expand(ks1), expand(vs1),
                         jnp.transpose(cache_swa_k[0], (0, 2, 3, 1)),
                         jnp.transpose(cache_swa_v[0], (0, 2, 3, 1)))
    ys, g1 = _post(xs1, [os_swa.reshape(n_s, -1)], [owout], *ffn[1], n_seq=n_s, tm=n_s,
                   conv_bufs=(state_ffn_conv[1, :, 0], state_ffn_conv[1, :, 1]), final_norm=fnorm)

    y_sample = ys.reshape(n_s, 1, D_MODEL)
    s_s5_re = s_hr.reshape(1, n_s, S5_GROUPS, S5_STATE)
    s_s5_im = s_hi.reshape(1, n_s, S5_GROUPS, S5_STATE)
    s_ckv = ckv_s.reshape(1, n_s, 1, MLA_KV_LORA)
    s_krope = kr_s32.reshape(1, n_s, 1, MLA_ROPE)
    s_swa_k = ks1.reshape(1, n_s, 1, SWA_KV_HEADS, SWA_HD)
    s_swa_v = vs1.reshape(1, n_s, 1, SWA_KV_HEADS, SWA_HD)
    s_conv = jnp.stack([jnp.stack([state_ffn_conv[0, :, 1], g0], axis=1),
                        jnp.stack([state_ffn_conv[1, :, 1], g1], axis=1)])

    return (y_prompt, y_sample, p_s5_re, p_s5_im, p_ckv, p_krope, p_swa_k, p_swa_v, p_conv,
            s_s5_re, s_s5_im, s_ckv, s_krope, s_swa_k, s_swa_v, s_conv)
```

```python
import functools
import math

import numpy as np
import jax
import jax.numpy as jnp
from jax import lax
from jax.experimental import pallas as pl
from jax.experimental.pallas import tpu as pltpu

F32 = jnp.float32
BF16 = jnp.bfloat16

D_MODEL = 1024
S5_WIDTH = 512
S5_GROUPS = 32
S5_GROUP = 16
S5_STATE = 64
S5_NSTATE = S5_GROUPS * S5_STATE
MLA_HEADS = 16
MLA_NOPE = 64
MLA_ROPE = 32
MLA_V = 64
MLA_Q_LORA = 384
MLA_KV_LORA = 256
MLA_THETA = 10000.0
MLA_SCALE = 1.0 / math.sqrt(MLA_NOPE + MLA_ROPE)
SWA_HEADS = 16
SWA_KV_HEADS = 4
SWA_HD = 64
SWA_GQ = SWA_HEADS // SWA_KV_HEADS
WINDOW = 128
ROT_DIM = SWA_HD // 4
ROPE_THETA = 500000.0
SWA_SCALE = 1.0 / math.sqrt(SWA_HD)
D_FF = 2816
PAGE = 128
EPS = 1e-6
NEG = -1e30

LANE = 128
SUBLANE = 8
HEAD_PAD = LANE
Z_EVEN = S5_WIDTH + MLA_Q_LORA + MLA_KV_LORA + LANE
QK_ODD = (SWA_HEADS + SWA_KV_HEADS) * SWA_HD
FF_CHUNK = 256
VMEM_LIMIT = 48 * 1024 * 1024


def _cparams(sem):
    return pltpu.CompilerParams(dimension_semantics=sem, vmem_limit_bytes=VMEM_LIMIT)


def _rms(x, g):
    return x * lax.rsqrt(jnp.mean(x * x, axis=-1, keepdims=True) + EPS) * g


def _const_spec(shape):
    nd = len(shape)
    return pl.BlockSpec(shape, lambda *_: (0,) * nd)


def _single_spec(shape):
    nd = len(shape)
    return pl.BlockSpec(shape, lambda *_: (0,) * nd, pipeline_mode=pl.Buffered(1))


def _mla_rope_tables(pos):
    half = MLA_ROPE // 2
    inv = MLA_THETA ** (-np.arange(half, dtype=np.float64) * 2.0 / MLA_ROPE)
    ang = np.asarray(pos, np.float64)[:, None] * inv[None, :]
    cos, sin = np.cos(ang), np.sin(ang)
    p = ang.shape[0]
    c = np.zeros((p, LANE)); sa = np.zeros((p, LANE)); sb = np.zeros((p, LANE))
    c[:, :MLA_NOPE] = 1.0
    c[:, MLA_NOPE:MLA_NOPE + half] = cos
    c[:, MLA_NOPE + half:MLA_NOPE + 2 * half] = cos
    sa[:, MLA_NOPE + half:MLA_NOPE + 2 * half] = sin
    sb[:, MLA_NOPE:MLA_NOPE + half] = -sin
    return tuple(jnp.asarray(t, F32) for t in (c, sa, sb))


def _swa_rope_tables(pos):
    half = ROT_DIM // 2
    inv = ROPE_THETA ** (-np.arange(half, dtype=np.float64) * 2.0 / ROT_DIM)
    ang = np.asarray(pos, np.float64)[:, None] * inv[None, :]
    cos, sin = np.cos(ang), np.sin(ang)
    p = ang.shape[0]
    c = np.ones((p, LANE)); sa = np.zeros((p, LANE)); sb = np.zeros((p, LANE))
    for o in (0, SWA_HD):
        c[:, o:o + half] = cos
        c[:, o + half:o + 2 * half] = cos
        sa[:, o + half:o + 2 * half] = sin
        sb[:, o:o + half] = -sin
    return tuple(jnp.asarray(t, F32) for t in (c, sa, sb))


def _rope_lanes(x, c, sa, sb, half):
    width = x.shape[1]
    reps = width // LANE
    if reps > 1:
        c = jnp.concatenate([c] * reps, axis=1)
        sa = jnp.concatenate([sa] * reps, axis=1)
        sb = jnp.concatenate([sb] * reps, axis=1)
    return x * c + pltpu.roll(x, half, 1) * sa + pltpu.roll(x, width - half, 1) * sb


def _even_proj_kernel(x_ref, c_ref, sa_ref, sb_ref, nm_ref, win_ref, qn_ref, wuq_ref, kvn_ref,
                      wuk_ref, wuv_ref, u_ref, q_ref, ckv_ref, kr_ref, *kv_refs):
    h = _rms(x_ref[...], nm_ref[...]).astype(BF16)
    z = jnp.dot(h, win_ref[...], preferred_element_type=F32)
    u_ref[...] = z[:, :S5_WIDTH]
    c, sa, sb = c_ref[...], sa_ref[...], sb_ref[...]
    o1 = S5_WIDTH + MLA_Q_LORA
    o2 = o1 + MLA_KV_LORA
    qn = _rms(z[:, S5_WIDTH:o1], qn_ref[...]).astype(BF16)
    q = jnp.dot(qn, wuq_ref[...], preferred_element_type=F32)
    q = _rope_lanes(q, c, sa, sb, MLA_ROPE // 2)
    ckv = _rms(z[:, o1:o2], kvn_ref[...])
    ckv_ref[...] = ckv
    kr = _rope_lanes(z[:, o2:], c, sa, sb, MLA_ROPE // 2)
    kr_ref[...] = kr
    if kv_refs:
        k_ref, vt_ref = kv_refs
        q_ref[0, 0] = q.T.astype(BF16)
        ckvb = ckv.astype(BF16)
        k = jnp.dot(ckvb, wuk_ref[...], preferred_element_type=F32)
        k_ref[...] = (k + jnp.concatenate([kr] * MLA_HEADS, axis=1)).astype(BF16)
        vt_ref[0, 0] = jnp.dot(ckvb, wuv_ref[...], preferred_element_type=F32).T.astype(BF16)
    else:
        q_ref[...] = q.astype(BF16)


def _even_proj(x, tabs, nm, win, qnorm, wuq, kvnorm, wuk, wuv, *, tm, n_seq, with_kv):
    rows = x.shape[0]
    nt = tabs[0].shape[0] // tm
    row = lambda w: pl.BlockSpec((tm, w), lambda i: (i, 0))
    tab = pl.BlockSpec((tm, LANE), lambda i: (i % nt, 0))
    tblock = lambda w: pl.BlockSpec((1, 1, w, tm), lambda i: (i // nt, i % nt, 0, 0))
    qw = MLA_HEADS * HEAD_PAD
    vw = MLA_HEADS * MLA_V
    if with_kv:
        q_shape, q_spec = jax.ShapeDtypeStruct((n_seq, nt, qw, tm), BF16), tblock(qw)
    else:
        q_shape, q_spec = jax.ShapeDtypeStruct((rows, qw), BF16), row(qw)
    out_shape = [jax.ShapeDtypeStruct((rows, S5_WIDTH), F32), q_shape,
                 jax.ShapeDtypeStruct((rows, MLA_KV_LORA), F32),
                 jax.ShapeDtypeStruct((rows, LANE), F32)]
    out_specs = [row(S5_WIDTH), q_spec, row(MLA_KV_LORA), row(LANE)]
    if with_kv:
        out_shape += [jax.ShapeDtypeStruct((rows, qw), BF16),
                      jax.ShapeDtypeStruct((n_seq, nt, vw, tm), BF16)]
        out_specs += [row(qw), tblock(vw)]
    return pl.pallas_call(
        _even_proj_kernel,
        grid=(rows // tm,),
        in_specs=[row(D_MODEL), tab, tab, tab, _const_spec(nm.shape), _const_spec(win.shape),
                  _const_spec(qnorm.shape), _const_spec(wuq.shape), _const_spec(kvnorm.shape),
                  _const_spec(wuk.shape), _const_spec(wuv.shape)],
        out_specs=out_specs,
        out_shape=out_shape,
        compiler_params=_cparams(("arbitrary",)),
        name="even_proj",
    )(x, *tabs, nm, win, qnorm, wuq, kvnorm, wuk, wuv)


S5_STRIP = 512


def _s5_kernel(u_ref, h0r_ref, h0i_ref, lamr_ref, lami_ref, ldt_ref, wb_ref, wcr_ref, wci_ref,
               d_ref, wglu_ref, bglu_ref, y_ref, hr_ref, hi_ref, ut_ref, xr_ref, xi_ref, yt_ref,
               *, nb, tt, transpose_io):
    @pl.when(pl.program_id(0) == 0)
    def _():
        xr_ref[0:nb, :] = h0r_ref[...]
        xi_ref[0:nb, :] = h0i_ref[...]

    if transpose_io:
        for t in range(tt):
            ut_ref[t * nb:(t + 1) * nb, :] = u_ref[:, t, :]
    else:
        ut_ref[...] = u_ref[0]
    ub = ut_ref[...].astype(BF16)

    lr = jnp.minimum(lamr_ref[...], -1e-4)
    li = lami_ref[...]
    dt = jnp.exp(ldt_ref[...])
    mag = jnp.exp(lr * dt)
    ar = mag * jnp.cos(li * dt)
    ai = mag * jnp.sin(li * dt)
    den = lr * lr + li * li
    zr = ((ar - 1.0) * lr + ai * li) / den
    zi = (ai * lr - (ar - 1.0) * li) / den

    for s in range(S5_NSTATE // S5_STRIP):
        cols = slice(s * S5_STRIP, (s + 1) * S5_STRIP)
        br = jnp.dot(ub, wb_ref[:, cols], preferred_element_type=F32)
        bi = jnp.dot(ub, wb_ref[:, S5_NSTATE + s * S5_STRIP:S5_NSTATE + (s + 1) * S5_STRIP],
                     preferred_element_type=F32)
        zrs, zis = zr[:, cols], zi[:, cols]
        xr_ref[nb:, cols] = zrs * br - zis * bi
        xi_ref[nb:, cols] = zrs * bi + zis * br
        ars = jnp.broadcast_to(ar[:, cols], (nb, S5_STRIP))
        ais = jnp.broadcast_to(ai[:, cols], (nb, S5_STRIP))

        def step(t, carry):
            hr, hi = carry
            r0 = pl.multiple_of((t + 1) * nb, nb)
            nhr = ars * hr - ais * hi + xr_ref[pl.ds(r0, nb), cols]
            nhi = ars * hi + ais * hr + xi_ref[pl.ds(r0, nb), cols]
            xr_ref[pl.ds(r0, nb), cols] = nhr
            xi_ref[pl.ds(r0, nb), cols] = nhi
            return nhr, nhi

        hr, hi = lax.fori_loop(0, tt, step, (xr_ref[0:nb, cols], xi_ref[0:nb, cols]),
                               unroll=min(tt, 8))
        xr_ref[0:nb, cols] = hr
        xi_ref[0:nb, cols] = hi

    hr_ref[...] = xr_ref[0:nb, :]
    hi_ref[...] = xi_ref[0:nb, :]
    y = (jnp.dot(xr_ref[nb:, :].astype(BF16), wcr_ref[...], preferred_element_type=F32)
         + jnp.dot(xi_ref[nb:, :].astype(BF16), wci_ref[...], preferred_element_type=F32)
         + d_ref[...] * ut_ref[...])
    y = jax.nn.gelu(y)
    gate = jnp.dot(y.astype(BF16), wglu_ref[...], preferred_element_type=F32) + bglu_ref[...]
    y = y * jax.nn.sigmoid(gate)
    if transpose_io:
        groups = S5_WIDTH // LANE
        for c in range(groups):
            yt_ref[c] = y[:, c * LANE:(c + 1) * LANE]
        for n in range(nb):
            y_ref[n] = jnp.concatenate(
                [yt_ref[c, pl.ds(n, tt, stride=nb), :] for c in range(groups)],
                axis=1).astype(y_ref.dtype)
    else:
        y_ref[0] = y.astype(y_ref.dtype)


def _s5(u, h0r, h0i, lamr, lami, ldt, wb, wcr, wci, d, wglu, bglu, *, nb, tt, transpose_io):
    t_total = u.shape[1] if transpose_io else u.shape[0]
    if transpose_io:
        u_spec = pl.BlockSpec((nb, tt, S5_WIDTH), lambda i: (0, i, 0))
        y_shape = (nb, t_total, S5_WIDTH)
    else:
        u_spec = pl.BlockSpec((tt, nb, S5_WIDTH), lambda i: (i, 0, 0))
        y_shape = (t_total, nb, S5_WIDTH)
    consts = (h0r, h0i, lamr, lami, ldt, wb, wcr, wci, d, wglu, bglu)
    kern = functools.partial(_s5_kernel, nb=nb, tt=tt, transpose_io=transpose_io)
    return pl.pallas_call(
        kern,
        grid=(t_total // tt,),
        in_specs=[u_spec] + [_const_spec(c.shape) for c in consts],
        out_specs=[u_spec, _const_spec((nb, S5_NSTATE)), _const_spec((nb, S5_NSTATE))],
        out_shape=[jax.ShapeDtypeStruct(y_shape, BF16),
                   jax.ShapeDtypeStruct((nb, S5_NSTATE), F32),
                   jax.ShapeDtypeStruct((nb, S5_NSTATE), F32)],
        scratch_shapes=[pltpu.VMEM((nb * tt, S5_WIDTH), F32),
                        pltpu.VMEM((nb * (tt + 1), S5_NSTATE), F32),
                        pltpu.VMEM((nb * (tt + 1), S5_NSTATE), F32),
                        pltpu.VMEM((S5_WIDTH // LANE, nb * tt, LANE), F32)],
        compiler_params=_cparams(("arbitrary",)),
        name="s5_mixer",
    )(u, *consts)


MLA_LOOKAHEAD = 2


def _mla_attn_kernel(qt_ref, k_ref, vt_ref, o_ref, m_ref, l_ref, acc_ref, *, tq):
    qi = pl.program_id(1)
    m_ref[...] = jnp.full(m_ref.shape, NEG, F32)
    l_ref[...] = jnp.zeros(l_ref.shape, F32)
    acc_ref[...] = jnp.zeros(acc_ref.shape, F32)

    def causal_mask(nblk):
        key = lax.broadcasted_iota(jnp.int32, (nblk * tq, tq), 0)
        qry = lax.broadcasted_iota(jnp.int32, (nblk * tq, tq), 1)
        return key <= qry + (nblk - 1) * tq

    def process(kb0, nblk, mask):
        keys = pl.ds(pl.multiple_of(kb0 * tq, tq), nblk * tq)

        def scores(h):
            hl = slice(h * HEAD_PAD, (h + 1) * HEAD_PAD)
            return jnp.dot(k_ref[0, keys, hl], qt_ref[0, 0, hl, :],
                           preferred_element_type=F32)

        pending = [scores(h) for h in range(MLA_LOOKAHEAD)]
        for h in range(MLA_HEADS):
            st = pending.pop(0) * MLA_SCALE
            if h + MLA_LOOKAHEAD < MLA_HEADS:
                pending.append(scores(h + MLA_LOOKAHEAD))
            if mask is not None:
                st = jnp.where(mask, st, NEG)
            j = h // 2
            vts = [vt_ref[0, kb0 + i, j * LANE:(j + 1) * LANE, :] for i in range(nblk)]
            vt = vts[0] if nblk == 1 else jnp.concatenate(vts, axis=1)
            m_old = m_ref[h]
            m_new = jnp.maximum(m_old, jnp.max(st, axis=0, keepdims=True))
            alpha = jnp.exp(m_old - m_new)
            pt = jnp.exp(st - m_new)
            l_ref[h] = alpha * l_ref[h] + jnp.sum(pt, axis=0, keepdims=True)
            m_ref[h] = m_new
            acc_ref[h] = alpha * acc_ref[h] + jnp.dot(vt, pt.astype(BF16),
                                                      preferred_element_type=F32)

    n_wide = qi // 2

    def wide_step(i, carry):
        process(i * 2, 2, None)
        return carry

    lax.fori_loop(0, n_wide, wide_step, 0)

    @pl.when(qi % 2 == 0)
    def _():
        process(qi, 1, causal_mask(1))

    @pl.when(qi % 2 == 1)
    def _():
        process(qi - 1, 2, causal_mask(2))

    half = lax.broadcasted_iota(jnp.int32, (LANE, tq), 0) < MLA_V
    for j in range(MLA_HEADS // 2):
        ot = jnp.where(half, acc_ref[2 * j] / l_ref[2 * j], acc_ref[2 * j + 1] / l_ref[2 * j + 1])
        o_ref[0, :, j * LANE:(j + 1) * LANE] = ot.T.astype(o_ref.dtype)


def _mla_attn(qt, k, vt):
    n, nblocks, qw, tq = qt.shape
    t = k.shape[1]
    kern = functools.partial(_mla_attn_kernel, tq=tq)
    return pl.pallas_call(
        kern,
        grid=(n, nblocks),
        in_specs=[pl.BlockSpec((1, 1, qw, tq), lambda b, i: (b, i, 0, 0)),
                  pl.BlockSpec((1, t, k.shape[2]), lambda b, i: (b, 0, 0)),
                  pl.BlockSpec((1, nblocks, vt.shape[2], tq), lambda b, i: (b, 0, 0, 0))],
        out_specs=pl.BlockSpec((1, tq, MLA_HEADS * MLA_V), lambda b, i: (b, i, 0)),
        out_shape=jax.ShapeDtypeStruct((n, t, MLA_HEADS * MLA_V), BF16),
        scratch_shapes=[pltpu.VMEM((MLA_HEADS, 1, tq), F32),
                        pltpu.VMEM((MLA_HEADS, 1, tq), F32),
                        pltpu.VMEM((MLA_HEADS, LANE, tq), F32)],
        compiler_params=_cparams(("arbitrary", "arbitrary")),
        name="mla_prompt_attn",
    )(qt, k, vt)


DEC_PAGES = 32
DEC_SUB = 4
DEC_AHEAD = 2
DEC_SLOTS = DEC_AHEAD + 1
DEC_KEYS = DEC_PAGES * PAGE


def _qlat_kernel(q_ref, wk_ref, o_ref):
    for h in range(MLA_HEADS):
        o_ref[:, h * MLA_KV_LORA:(h + 1) * MLA_KV_LORA] = jnp.dot(
            q_ref[:, h * HEAD_PAD:(h + 1) * HEAD_PAD], wk_ref[h], preferred_element_type=F32)


def _qlat(q_big, wk_t):
    rows = q_big.shape[0]
    return pl.pallas_call(
        _qlat_kernel,
        grid=(1,),
        in_specs=[_const_spec(q_big.shape), _const_spec(wk_t.shape)],
        out_specs=_const_spec((rows, MLA_HEADS * MLA_KV_LORA)),
        out_shape=jax.ShapeDtypeStruct((rows, MLA_HEADS * MLA_KV_LORA), F32),
        compiler_params=_cparams(("arbitrary",)),
        name="mla_q_absorb",
    )(q_big, wk_t)


def _olat_kernel(o_ref, wv_ref, out_ref):
    for j in range(MLA_HEADS // 2):
        acc = None
        for h in (2 * j, 2 * j + 1):
            part = jnp.dot(o_ref[:, h * MLA_KV_LORA:(h + 1) * MLA_KV_LORA].astype(BF16),
                           wv_ref[:, h * LANE:(h + 1) * LANE], preferred_element_type=F32)
            acc = part if acc is None else acc + part
        out_ref[:, j * LANE:(j + 1) * LANE] = acc.astype(out_ref.dtype)


def _olat(o_lat, wv_big):
    rows = o_lat.shape[0]
    return pl.pallas_call(
        _olat_kernel,
        grid=(1,),
        in_specs=[_const_spec(o_lat.shape), _const_spec(wv_big.shape)],
        out_specs=_const_spec((rows, MLA_HEADS * MLA_V)),
        out_shape=jax.ShapeDtypeStruct((rows, MLA_HEADS * MLA_V), BF16),
        compiler_params=_cparams(("arbitrary",)),
        name="mla_v_expand",
    )(o_lat, wv_big)


def _page_copies(cache_ckv, cache_kr, ckv_buf, kr_buf, sem, slot, page, p):
    rows = pl.ds(p * PAGE, PAGE)
    return (pltpu.make_async_copy(cache_ckv.at[0, page], ckv_buf.at[slot, rows], sem.at[0, slot]),
            pltpu.make_async_copy(cache_kr.at[0, page], kr_buf.at[slot, p], sem.at[1, slot]))


def _mla_decode_kernel(ptc_ref, ptn_ref, ql_ref, qr_ref, ckvn_ref, krn_ref, cache_ckv, cache_kr,
                       o_ref, ckv_buf, kr_buf, sem, *, n_seq, n_chunks):
    b = pl.program_id(0)
    g0 = b * n_chunks

    def start_page(page, slot, p):
        for cp in _page_copies(cache_ckv, cache_kr, ckv_buf, kr_buf, sem, slot, page, p):
            cp.start()

    def wait_chunk(slot):
        for p in range(DEC_PAGES):
            for cp in _page_copies(cache_ckv, cache_kr, ckv_buf, kr_buf, sem, slot, 0, p):
                cp.wait()

    @pl.when(b == 0)
    def _():
        for a in range(DEC_AHEAD):
            for p in range(DEC_PAGES):
                start_page(ptc_ref[0, 0, a * DEC_PAGES + p], a, p)

    ql = ql_ref[0].astype(BF16)
    qr = qr_ref[0].astype(BF16)

    def chunk_body(c, carry):
        m, l, acc = carry
        slot = lax.rem(g0 + c, DEC_SLOTS)
        ahead_slot = lax.rem(g0 + c + DEC_AHEAD, DEC_SLOTS)
        wait_chunk(slot)
        wraps = c + DEC_AHEAD >= n_chunks
        ahead_base = jnp.where(wraps, c + DEC_AHEAD - n_chunks, c + DEC_AHEAD) * DEC_PAGES

        def ahead_page(p):
            return jnp.where(wraps, ptn_ref[0, 0, ahead_base + p], ptc_ref[0, 0, ahead_base + p])

        kbs, scores = [], []
        for i in range(DEC_PAGES // DEC_SUB):
            kb = ckv_buf[slot, i * DEC_SUB * PAGE:(i + 1) * DEC_SUB * PAGE, :].astype(BF16)
            krt = jnp.concatenate([kr_buf[slot, i * DEC_SUB + p] for p in range(DEC_SUB)],
                                  axis=1).astype(BF16)
            kbs.append(kb)
            scores.append(
                (lax.dot_general(ql, kb, (((1,), (1,)), ((), ())), preferred_element_type=F32)
                 + jnp.dot(qr, krt, preferred_element_type=F32)) * MLA_SCALE)
            for p in range(i * DEC_SUB, (i + 1) * DEC_SUB):
                start_page(ahead_page(p), ahead_slot, p)
        for kb, s in zip(kbs, scores):
            m_new = jnp.maximum(m, jnp.max(s, axis=1, keepdims=True))
            alpha = jnp.exp(m - m_new)
            pr = jnp.exp(s - m_new)
            l = alpha * l + jnp.sum(pr, axis=1, keepdims=True)
            acc = alpha * acc + jnp.dot(pr.astype(BF16), kb, preferred_element_type=F32)
            m = m_new
        return m, l, acc

    init = (jnp.full((MLA_HEADS, 1), NEG, F32), jnp.zeros((MLA_HEADS, 1), F32),
            jnp.zeros((MLA_HEADS, MLA_KV_LORA), F32))
    m, l, acc = lax.fori_loop(0, n_chunks, chunk_body, init)

    @pl.when(b == n_seq - 1)
    def _():
        for a in range(DEC_AHEAD):
            wait_chunk((n_seq * n_chunks + a) % DEC_SLOTS)

    cn = ckvn_ref[0].astype(BF16).astype(F32)
    kn = krn_ref[0].astype(BF16).astype(F32)
    s_new = (jnp.sum(ql.astype(F32) * cn, axis=1, keepdims=True)
             + jnp.sum(qr.astype(F32) * kn, axis=1, keepdims=True)) * MLA_SCALE
    m_new = jnp.maximum(m, s_new)
    alpha = jnp.exp(m - m_new)
    p_new = jnp.exp(s_new - m_new)
    l = alpha * l + p_new
    acc = alpha * acc + p_new.astype(BF16).astype(F32) * cn
    o_ref[0] = acc / l


def _mla_decode(page_table, q_lat, q_rope, ckv_new, kr_new, cache_ckv, cache_kr):
    n_seq, n_pages = page_table.shape
    n_chunks = n_pages // DEC_PAGES
    assert n_pages % DEC_PAGES == 0 and n_chunks >= DEC_AHEAD
    pt = page_table.reshape(n_seq, 1, n_pages)
    smem_row = lambda f: pl.BlockSpec((1, 1, n_pages), f, memory_space=pltpu.SMEM)
    per_seq = lambda a: pl.BlockSpec((1,) + a.shape[1:], lambda b: (b, 0, 0))
    kern = functools.partial(_mla_decode_kernel, n_seq=n_seq, n_chunks=n_chunks)
    return pl.pallas_call(
        kern,
        grid=(n_seq,),
        in_specs=[smem_row(lambda b: (b, 0, 0)),
                  smem_row(lambda b: (jnp.minimum(b + 1, n_seq - 1), 0, 0)),
                  per_seq(q_lat), per_seq(q_rope), per_seq(ckv_new), per_seq(kr_new),
                  pl.BlockSpec(memory_space=pl.ANY), pl.BlockSpec(memory_space=pl.ANY)],
        out_specs=pl.BlockSpec((1, MLA_HEADS, MLA_KV_LORA), lambda b: (b, 0, 0)),
        out_shape=jax.ShapeDtypeStruct((n_seq, MLA_HEADS, MLA_KV_LORA), F32),
        scratch_shapes=[pltpu.VMEM((DEC_SLOTS, DEC_KEYS, MLA_KV_LORA), F32),
                        pltpu.VMEM((DEC_SLOTS, DEC_PAGES, MLA_ROPE, PAGE), F32),
                        pltpu.SemaphoreType.DMA((2, DEC_SLOTS))],
        compiler_params=_cparams(("arbitrary",)),
        name="mla_paged_decode",
    )(pt, pt, q_lat, q_rope, ckv_new, kr_new, cache_ckv, cache_kr)


def _odd_proj_kernel(x_ref, c_ref, sa_ref, sb_ref, nm_ref, win_ref, q_ref, k_ref, v_ref):
    h = _rms(x_ref[...], nm_ref[...]).astype(BF16)
    z = jnp.dot(h, win_ref[...], preferred_element_type=F32)
    qk = _rope_lanes(z[:, :QK_ODD], c_ref[...], sa_ref[...], sb_ref[...], ROT_DIM // 2)
    q_ref[...] = qk[:, :SWA_HEADS * SWA_HD].astype(BF16)
    k_ref[...] = qk[:, SWA_HEADS * SWA_HD:]
    v_ref[...] = z[:, QK_ODD:]


def _odd_proj(x, tabs, nm, win, *, tm):
    rows = x.shape[0]
    nt = tabs[0].shape[0] // tm
    row = lambda w: pl.BlockSpec((tm, w), lambda i: (i, 0))
    tab = pl.BlockSpec((tm, LANE), lambda i: (i % nt, 0))
    kvw = SWA_KV_HEADS * SWA_HD
    return pl.pallas_call(
        _odd_proj_kernel,
        grid=(rows // tm,),
        in_specs=[row(D_MODEL), tab, tab, tab, _const_spec(nm.shape), _const_spec(win.shape)],
        out_specs=[row(SWA_HEADS * SWA_HD), row(kvw), row(kvw)],
        out_shape=[jax.ShapeDtypeStruct((rows, SWA_HEADS * SWA_HD), BF16),
                   jax.ShapeDtypeStruct((rows, kvw), F32),
                   jax.ShapeDtypeStruct((rows, kvw), F32)],
        compiler_params=_cparams(("arbitrary",)),
        name="odd_proj",
    )(x, *tabs, nm, win)


def _swa_attn_kernel(sink_ref, q_ref, kp_ref, kc_ref, vp_ref, vc_ref, o_ref):
    b = pl.program_id(1)
    kk = jnp.concatenate([kp_ref[0], kc_ref[0]], axis=0).astype(BF16)
    vv = jnp.concatenate([vp_ref[0], vc_ref[0]], axis=0).astype(BF16)
    rows = SWA_GQ * WINDOW
    r = lax.broadcasted_iota(jnp.int32, (rows, 2 * WINDOW), 0)
    j = lax.broadcasted_iota(jnp.int32, (rows, 2 * WINDOW), 1)
    rel = jnp.bitwise_and(r, WINDOW - 1) + WINDOW - j
    valid = (rel >= 0) & (rel <= WINDOW) & ((j >= WINDOW) | (b > 0))
    rcol = lax.broadcasted_iota(jnp.int32, (rows, 1), 0)
    outs = []
    scores = []
    for kh in range(SWA_KV_HEADS):
        qg = jnp.concatenate(
            [q_ref[0, :, (kh * SWA_GQ + g) * SWA_HD:(kh * SWA_GQ + g + 1) * SWA_HD]
             for g in range(SWA_GQ)], axis=0)
        khd = kk[:, kh * SWA_HD:(kh + 1) * SWA_HD]
        scores.append(lax.dot_general(qg, khd, (((1,), (1,)), ((), ())),
                                      preferred_element_type=F32))
    for kh in range(SWA_KV_HEADS):
        vhd = vv[:, kh * SWA_HD:(kh + 1) * SWA_HD]
        s = jnp.where(valid, scores[kh] * SWA_SCALE, NEG)
        sink = jnp.full((rows, 1), sink_ref[kh * SWA_GQ + SWA_GQ - 1], F32)
        for g in range(SWA_GQ - 2, -1, -1):
            sink = jnp.where(rcol < (g + 1) * WINDOW, sink_ref[kh * SWA_GQ + g], sink)
        m = jnp.maximum(jnp.max(s, axis=1, keepdims=True), sink)
        p = jnp.exp(s - m)
        l = jnp.sum(p, axis=1, keepdims=True) + jnp.exp(sink - m)
        o = jnp.dot(p.astype(BF16), vhd, preferred_element_type=F32) / l
        outs += [o[g * WINDOW:(g + 1) * WINDOW] for g in range(SWA_GQ)]
    o_ref[0] = jnp.concatenate(outs, axis=1).astype(o_ref.dtype)


def _swa_attn(sinks, q, k, v):
    n, t, _ = q.shape
    kvw = SWA_KV_HEADS * SWA_HD
    prev = pl.BlockSpec((1, WINDOW, kvw), lambda a, i: (a, jnp.maximum(i - 1, 0), 0))
    cur = pl.BlockSpec((1, WINDOW, kvw), lambda a, i: (a, i, 0))
    qspec = pl.BlockSpec((1, WINDOW, SWA_HEADS * SWA_HD), lambda a, i: (a, i, 0))
    return pl.pallas_call(
        _swa_attn_kernel,
        grid=(n, t // WINDOW),
        in_specs=[pl.BlockSpec(memory_space=pltpu.SMEM), qspec, prev, cur, prev, cur],
        out_specs=qspec,
        out_shape=jax.ShapeDtypeStruct((n, t, SWA_HEADS * SWA_HD), BF16),
        compiler_params=_cparams(("arbitrary", "arbitrary")),
        name="swa_prompt_attn",
    )(sinks, q, k, k, v, v)


SWA_DEC_SEQS = 8


def _swa_decode_kernel(sink_ref, q_ref, kn_ref, vn_ref, ck_ref, cv_ref, o_ref):
    q = q_ref[...]
    hgrp = lax.broadcasted_iota(jnp.int32, (1, SWA_HEADS, 1), 1) // SWA_GQ
    hidx = lax.broadcasted_iota(jnp.int32, (1, SWA_HEADS, 1), 1)
    sink = jnp.zeros((1, SWA_HEADS, 1), F32)
    for h in range(SWA_HEADS):
        sink = jnp.where(hidx == h, sink_ref[h], sink)
    s = jnp.zeros((q.shape[0], SWA_HEADS, WINDOW), F32)
    for kh in range(SWA_KV_HEADS):
        skh = jnp.einsum('nhd,ndj->nhj', q, ck_ref[:, kh].astype(BF16),
                         preferred_element_type=F32)
        s = jnp.where(hgrp == kh, skh, s)
    s = s * SWA_SCALE
    qf = q.astype(F32)
    kn = kn_ref[...].astype(BF16).astype(F32)
    vn = vn_ref[...].astype(BF16).astype(F32)
    s_new = jnp.sum(qf * kn, axis=2, keepdims=True) * SWA_SCALE
    m = jnp.maximum(jnp.maximum(jnp.max(s, axis=2, keepdims=True), s_new), sink)
    p = jnp.exp(s - m)
    p_new = jnp.exp(s_new - m)
    l = jnp.sum(p, axis=2, keepdims=True) + p_new + jnp.exp(sink - m)
    o = p_new.astype(BF16).astype(F32) * vn
    for kh in range(SWA_KV_HEADS):
        pk = jnp.where(hgrp == kh, p, 0.0).astype(BF16)
        o = o + jnp.einsum('nhj,ndj->nhd', pk, cv_ref[:, kh].astype(BF16),
                           preferred_element_type=F32)
    o_ref[...] = (o / l).astype(o_ref.dtype)


def _swa_decode(sinks, q, k_new, v_new, cache_k, cache_v):
    n = q.shape[0]
    nb = SWA_DEC_SEQS
    seq = pl.BlockSpec((nb, SWA_HEADS, SWA_HD), lambda i: (i, 0, 0))
    cache = pl.BlockSpec((nb, SWA_KV_HEADS, SWA_HD, WINDOW), lambda i: (i, 0, 0, 0))
    return pl.pallas_call(
        _swa_decode_kernel,
        grid=(n // nb,),
        in_specs=[pl.BlockSpec(memory_space=pltpu.SMEM), seq, seq, seq, cache, cache],
        out_specs=seq,
        out_shape=jax.ShapeDtypeStruct((n, SWA_HEADS, SWA_HD), BF16),
        compiler_params=_cparams(("arbitrary",)),
        name="swa_decode_attn",
    )(sinks, q, k_new, v_new, cache_k, cache_v)


def _post_kernel(*refs, n_mix, decode, final, tm):
    it = iter(refs)
    x_ref = next(it)
    mix_refs = [next(it) for _ in range(n_mix)]
    wo_refs = [next(it) for _ in range(n_mix)]
    nf_ref, wg_ref, wu_ref, cw_ref, cb_ref, wd_ref = (next(it) for _ in range(6))
    if decode:
        buf0_ref, buf1_ref = next(it), next(it)
    fn_ref = next(it) if final else None
    y_ref = next(it)
    g_ref = next(it)
    act_ref = next(it)
    carry_ref = None if decode else next(it)

    x1 = x_ref[...]
    for a_ref, w_ref in zip(mix_refs, wo_refs):
        x1 = x1 + jnp.dot(a_ref[...], w_ref[...], preferred_element_type=F32)
    h2 = _rms(x1, nf_ref[...]).astype(BF16)

    if not decode:
        @pl.when(pl.program_id(1) == 0)
        def _():
            carry_ref[...] = jnp.zeros_like(carry_ref)
        row = lax.broadcasted_iota(jnp.int32, (tm, FF_CHUNK), 0)

    for c in range(D_FF // FF_CHUNK):
        sl = slice(c * FF_CHUNK, (c + 1) * FF_CHUNK)
        g = jnp.dot(h2, wg_ref[:, sl], preferred_element_type=F32)
        u = jnp.dot(h2, wu_ref[:, sl], preferred_element_type=F32)
        if decode:
            gm2, gm1 = buf0_ref[:, sl], buf1_ref[:, sl]
            g_ref[:, sl] = g
        else:
            prev = carry_ref[:, sl]
            p6, p7 = prev[SUBLANE - 2:SUBLANE - 1, :], prev[SUBLANE - 1:SUBLANE, :]
            gm1 = jnp.where(row == 0, p7, pltpu.roll(g, 1, 0))
            gm2 = jnp.where(row == 0, p6, jnp.where(row == 1, p7, pltpu.roll(g, 2, 0)))
            last = g[tm - SUBLANE:tm, :]
            carry_ref[:, sl] = last
            g_ref[0, :, sl] = last
        cc = cb_ref[:, sl] + cw_ref[0:1, sl] * gm2 + cw_ref[1:2, sl] * gm1 + cw_ref[2:3, sl] * g
        act_ref[:, sl] = (jax.nn.gelu(cc) * u).astype(BF16)

    x2 = x1 + jnp.dot(act_ref[...], wd_ref[...], preferred_element_type=F32)
    y_ref[...] = _rms(x2, fn_ref[...]) if final else x2


def _post(x, mixes, wos, nf, wg, wu, cw, cb, wd, *, n_seq, tm, conv_bufs=None, final_norm=None):
    rows = x.shape[0]
    decode = conv_bufs is not None
    nt = rows // n_seq // tm if not decode else rows // tm
    grid = (1, nt) if decode else (n_seq, nt)
    row = lambda w: pl.BlockSpec((tm, w), lambda a, i: (a * nt + i, 0))
    args = [x] + list(mixes) + list(wos) + [nf, wg, wu, cw, cb, wd]
    in_specs = ([row(D_MODEL)] + [row(m.shape[1]) for m in mixes]
                + [_single_spec(w.shape) for w in wos]
                + [_const_spec(nf.shape), _single_spec(wg.shape), _single_spec(wu.shape),
                   _const_spec(cw.shape), _const_spec(cb.shape), _single_spec(wd.shape)])
    if decode:
        args += list(conv_bufs)
        in_specs += [row(D_FF), row(D_FF)]
    if final_norm is not None:
        args.append(final_norm)
        in_specs.append(_const_spec(final_norm.shape))
    if decode:
        g_shape = jax.ShapeDtypeStruct((rows, D_FF), F32)
        g_spec = row(D_FF)
    else:
        g_shape = jax.ShapeDtypeStruct((n_seq, SUBLANE, D_FF), F32)
        g_spec = pl.BlockSpec((1, SUBLANE, D_FF), lambda a, i: (a, 0, 0))
    scratch = [pltpu.VMEM((tm, D_FF), BF16)]
    if not decode:
        scratch.append(pltpu.VMEM((SUBLANE, D_FF), F32))
    kern = functools.partial(_post_kernel, n_mix=len(mixes), decode=decode,
                             final=final_norm is not None, tm=tm)
    return pl.pallas_call(
        kern,
        grid=grid,
        in_specs=in_specs,
        out_specs=[row(D_MODEL), g_spec],
        out_shape=[jax.ShapeDtypeStruct((rows, D_MODEL), F32), g_shape],
        scratch_shapes=scratch,
        compiler_params=_cparams(("arbitrary", "arbitrary")),
        name="post_decode" if decode else "post_prompt",
    )(*args)


def _prep_even(e_w_in, e_w_uq, e_w_uk, e_w_uv, e_s5_b_re, e_s5_b_im, e_s5_c_re, e_s5_c_im):
    o1 = S5_WIDTH
    o2 = o1 + MLA_Q_LORA
    o3 = o2 + MLA_KV_LORA
    zpad = lambda n: jnp.zeros((D_MODEL, n), F32)
    win = jnp.concatenate([e_w_in[:, :o3], zpad(MLA_NOPE), e_w_in[:, o3:],
                           zpad(HEAD_PAD - MLA_NOPE - MLA_ROPE)], axis=1).astype(BF16)
    pad_last = lambda w, n: jnp.pad(w, ((0, 0), (0, 0), (0, n - w.shape[2])))
    wuq = pad_last(e_w_uq, HEAD_PAD).reshape(MLA_Q_LORA, MLA_HEADS * HEAD_PAD).astype(BF16)
    wuk = pad_last(e_w_uk, HEAD_PAD).reshape(MLA_KV_LORA, MLA_HEADS * HEAD_PAD).astype(BF16)
    wuv = e_w_uv.reshape(MLA_KV_LORA, MLA_HEADS * MLA_V).astype(BF16)
    wk_t = jnp.pad(jnp.transpose(e_w_uk, (1, 2, 0)),
                   ((0, 0), (0, HEAD_PAD - MLA_NOPE), (0, 0))).astype(BF16)
    wv4 = e_w_uv.reshape(MLA_KV_LORA, MLA_HEADS // 2, 2, MLA_V)
    zv = jnp.zeros_like(wv4[:, :, 0])
    wv_big = jnp.stack([jnp.concatenate([wv4[:, :, 0], zv], axis=-1),
                        jnp.concatenate([zv, wv4[:, :, 1]], axis=-1)], axis=2)
    wv_big = wv_big.reshape(MLA_KV_LORA, MLA_HEADS * LANE).astype(BF16)
    eye = jnp.eye(S5_GROUPS, dtype=F32)
    bd_in = lambda b: jnp.einsum('gph,gk->ghkp', b, eye).reshape(S5_WIDTH, S5_NSTATE)
    bd_out = lambda c: jnp.einsum('ghp,gk->gpkh', c, eye).reshape(S5_NSTATE, S5_WIDTH)
    wb = jnp.concatenate([bd_in(e_s5_b_re), bd_in(e_s5_b_im)], axis=1).astype(BF16)
    wcr = bd_out(e_s5_c_re).astype(BF16)
    wci = (-bd_out(e_s5_c_im)).astype(BF16)
    return win, wuq, wuk, wuv, wk_t, wv_big, wb, wcr, wci


def kernel(x_prompt, x_sample, page_table, state_s5_re, state_s5_im, cache_mla_ckv, cache_mla_krope,
           cache_swa_k, cache_swa_v, state_ffn_conv, norm_mix, norm_ffn, final_norm, e_w_in,
           e_s5_lam_re, e_s5_lam_im, e_s5_log_dt, e_s5_b_re, e_s5_b_im, e_s5_c_re, e_s5_c_im, e_s5_d,
           e_s5_w_glu, e_s5_b_glu, e_q_norm, e_w_uq, e_kv_norm, e_w_uk, e_w_uv, e_w_out, o_w_in,
           o_sinks, o_w_out, f_w_gate, f_w_up, f_conv_w, f_conv_b, f_w_down):
    n_p, t_p, _ = x_prompt.shape
    n_s, t_s, _ = x_sample.shape
    assert t_s == 1
    past_len = page_table.shape[1] * PAGE
    row2 = lambda v: v.reshape(1, -1)

    (win, wuq, wuk, wuv, wk_t, wv_big, wb, wcr, wci) = _prep_even(
        e_w_in[0], e_w_uq[0], e_w_uk[0], e_w_uv[0], e_s5_b_re[0], e_s5_b_im[0],
        e_s5_c_re[0], e_s5_c_im[0])
    lamr, lami = row2(e_s5_lam_re[0]), row2(e_s5_lam_im[0])
    ldt = row2(jnp.repeat(e_s5_log_dt[0], S5_STATE))
    s5_consts = (lamr, lami, ldt, wb, wcr, wci, row2(e_s5_d[0]), e_s5_w_glu[0].astype(BF16),
                 row2(e_s5_b_glu[0]))
    wout_s5 = e_w_out[0][:S5_WIDTH].astype(BF16)
    wout_mla = e_w_out[0][S5_WIDTH:].astype(BF16)
    owin = o_w_in[0].astype(BF16)
    owout = o_w_out[0].astype(BF16)
    ffn = [(row2(norm_ffn[l]), f_w_gate[l].astype(BF16), f_w_up[l].astype(BF16), f_conv_w[l],
            row2(f_conv_b[l]), f_w_down[l].astype(BF16)) for l in range(2)]
    nm0, nm1 = row2(norm_mix[0]), row2(norm_mix[1])
    qnorm, kvnorm = row2(e_q_norm[0]), row2(e_kv_norm[0])
    fnorm = row2(final_norm)
    sinks = o_sinks[0]

    xp = x_prompt.reshape(n_p * t_p, D_MODEL)
    pos_p = np.arange(t_p)
    u, qt_p, ckv_p, kr_p, k_big, vt_p = _even_proj(
        xp, _mla_rope_tables(pos_p), nm0, win, qnorm, wuq, kvnorm, wuk, wuv, tm=256, n_seq=n_p,
        with_kv=True)
    zeros_state = jnp.zeros((n_p, S5_NSTATE), F32)
    y_s5, p_hr, p_hi = _s5(u.reshape(n_p, t_p, S5_WIDTH), zeros_state, zeros_state, *s5_consts,
                           nb=n_p, tt=64, transpose_io=True)
    o_mla = _mla_attn(qt_p, k_big.reshape(n_p, t_p, -1), vt_p)
    x1, conv0 = _post(xp, [y_s5.reshape(n_p * t_p, -1), o_mla.reshape(n_p * t_p, -1)],
                      [wout_s5, wout_mla], *ffn[0], n_seq=n_p, tm=256)
    q1, k1, v1 = _odd_proj(x1, _swa_rope_tables(pos_p), nm1, owin, tm=256)
    kvw = SWA_KV_HEADS * SWA_HD
    o_swa = _swa_attn(sinks, q1.reshape(n_p, t_p, -1), k1.reshape(n_p, t_p, kvw),
                      v1.reshape(n_p, t_p, kvw))
    y_p, conv1 = _post(x1, [o_swa.reshape(n_p * t_p, -1)], [owout], *ffn[1], n_seq=n_p, tm=256,
                       final_norm=fnorm)

    y_prompt = y_p.reshape(n_p, t_p, D_MODEL)
    p_s5_re = p_hr.reshape(1, n_p, S5_GROUPS, S5_STATE)
    p_s5_im = p_hi.reshape(1, n_p, S5_GROUPS, S5_STATE)
    p_ckv = ckv_p.reshape(1, n_p, t_p, MLA_KV_LORA)
    p_krope = kr_p[:, MLA_NOPE:MLA_NOPE + MLA_ROPE].reshape(1, n_p, t_p, MLA_ROPE)
    w_keep = min(WINDOW, t_p)
    p_swa_k = k1.reshape(n_p, t_p, SWA_KV_HEADS, SWA_HD)[None, :, t_p - w_keep:]
    p_swa_v = v1.reshape(n_p, t_p, SWA_KV_HEADS, SWA_HD)[None, :, t_p - w_keep:]
    p_conv = jnp.stack([conv0[:, SUBLANE - 2:], conv1[:, SUBLANE - 2:]])

    xs = x_sample.reshape(n_s, D_MODEL)
    pos_s = np.full((n_s,), past_len)
    u_s, qs_big, ckv_s, kr_s = _even_proj(
        xs, _mla_rope_tables(pos_s), nm0, win, qnorm, wuq, kvnorm, wuk, wuv, tm=n_s, n_seq=n_s,
        with_kv=False)
    ys_s5, s_hr, s_hi = _s5(u_s.reshape(1, n_s, S5_WIDTH), state_s5_re[0].reshape(n_s, S5_NSTATE),
                            state_s5_im[0].reshape(n_s, S5_NSTATE), *s5_consts,
                            nb=n_s, tt=1, transpose_io=False)
    q_lat = _qlat(qs_big, wk_t).reshape(n_s, MLA_HEADS, MLA_KV_LORA)
    q_rope = qs_big.reshape(n_s, MLA_HEADS, HEAD_PAD)[:, :, MLA_NOPE:MLA_NOPE + MLA_ROPE]
    kr_s32 = kr_s[:, MLA_NOPE:MLA_NOPE + MLA_ROPE]
    o_lat = _mla_decode(page_table, q_lat, q_rope, ckv_s.reshape(n_s, 1, MLA_KV_LORA),
                        kr_s32.reshape(n_s, 1, MLA_ROPE), cache_mla_ckv,
                        jnp.swapaxes(cache_mla_krope, 2, 3))
    os_mla = _olat(o_lat.reshape(n_s, MLA_HEADS * MLA_KV_LORA), wv_big)
    xs1, g0 = _post(xs, [ys_s5.reshape(n_s, -1), os_mla], [wout_s5, wout_mla], *ffn[0],
                    n_seq=n_s, tm=n_s, conv_bufs=(state_ffn_conv[0, :, 0], state_ffn_conv[0, :, 1]))
    qs1, ks1, vs1 = _odd_proj(xs1, _swa_rope_tables(pos_s), nm1, owin, tm=n_s)
    expand = lambda a: jnp.repeat(a.reshape(n_s, SWA_KV_HEADS, SWA_HD), SWA_GQ, axis=1)
    os_swa = _swa_decode(sinks, qs1.reshape(n_s, SWA_HEADS, SWA_HD), expand(ks1), expand(vs1),
                         jnp.transpose(cache_swa_k[0], (0, 2, 3, 1)),
                         jnp.transpose(cache_swa_v[0], (0, 2, 3, 1)))
    ys, g1 = _post(xs1, [os_swa.reshape(n_s, -1)], [owout], *ffn[1], n_seq=n_s, tm=n_s,
                   conv_bufs=(state_ffn_conv[1, :, 0], state_ffn_conv[1, :, 1]), final_norm=fnorm)

    y_sample = ys.reshape(n_s, 1, D_MODEL)
    s_s5_re = s_hr.reshape(1, n_s, S5_GROUPS, S5_STATE)
    s_s5_im = s_hi.reshape(1, n_s, S5_GROUPS, S5_STATE)
    s_ckv = ckv_s.reshape(1, n_s, 1, MLA_KV_LORA)
    s_krope = kr_s32.reshape(1, n_s, 1, MLA_ROPE)
    s_swa_k = ks1.reshape(1, n_s, 1, SWA_KV_HEADS, SWA_HD)
    s_swa_v = vs1.reshape(1, n_s, 1, SWA_KV_HEADS, SWA_HD)
    s_conv = jnp.stack([jnp.stack([state_ffn_conv[0, :, 1], g0], axis=1),
                        jnp.stack([state_ffn_conv[1, :, 1], g1], axis=1)])

    return (y_prompt, y_sample, p_s5_re, p_s5_im, p_ckv, p_krope, p_swa_k, p_swa_v, p_conv,
            s_s5_re, s_s5_im, s_ckv, s_krope, s_swa_k, s_swa_v, s_conv)
```

```python
import functools
import math

import numpy as np
import jax
import jax.numpy as jnp
from jax import lax
from jax.experimental import pallas as pl
from jax.experimental.pallas import tpu as pltpu

F32 = jnp.float32
BF16 = jnp.bfloat16

D_MODEL = 1024
S5_WIDTH = 512
S5_GROUPS = 32
S5_GROUP = 16
S5_STATE = 64
S5_NSTATE = S5_GROUPS * S5_STATE
MLA_HEADS = 16
MLA_NOPE = 64
MLA_ROPE = 32
MLA_V = 64
MLA_Q_LORA = 384
MLA_KV_LORA = 256
MLA_THETA = 10000.0
MLA_SCALE = 1.0 / math.sqrt(MLA_NOPE + MLA_ROPE)
SWA_HEADS = 16
SWA_KV_HEADS = 4
SWA_HD = 64
SWA_GQ = SWA_HEADS // SWA_KV_HEADS
WINDOW = 128
ROT_DIM = SWA_HD // 4
ROPE_THETA = 500000.0
SWA_SCALE = 1.0 / math.sqrt(SWA_HD)
D_FF = 2816
PAGE = 128
EPS = 1e-6
NEG = -1e30

LANE = 128
SUBLANE = 8
HEAD_PAD = LANE
Z_EVEN = S5_WIDTH + MLA_Q_LORA + MLA_KV_LORA + LANE
QK_ODD = (SWA_HEADS + SWA_KV_HEADS) * SWA_HD
FF_CHUNK = 256
VMEM_LIMIT = 48 * 1024 * 1024


def _cparams(sem):
    return pltpu.CompilerParams(dimension_semantics=sem, vmem_limit_bytes=VMEM_LIMIT)


def _rms(x, g):
    return x * lax.rsqrt(jnp.mean(x * x, axis=-1, keepdims=True) + EPS) * g


def _const_spec(shape):
    nd = len(shape)
    return pl.BlockSpec(shape, lambda *_: (0,) * nd)


def _single_spec(shape):
    nd = len(shape)
    return pl.BlockSpec(shape, lambda *_: (0,) * nd, pipeline_mode=pl.Buffered(1))


def _mla_rope_tables(pos):
    half = MLA_ROPE // 2
    inv = MLA_THETA ** (-np.arange(half, dtype=np.float64) * 2.0 / MLA_ROPE)
    ang = np.asarray(pos, np.float64)[:, None] * inv[None, :]
    cos, sin = np.cos(ang), np.sin(ang)
    p = ang.shape[0]
    c = np.zeros((p, LANE)); sa = np.zeros((p, LANE)); sb = np.zeros((p, LANE))
    c[:, :MLA_NOPE] = 1.0
    c[:, MLA_NOPE:MLA_NOPE + half] = cos
    c[:, MLA_NOPE + half:MLA_NOPE + 2 * half] = cos
    sa[:, MLA_NOPE + half:MLA_NOPE + 2 * half] = sin
    sb[:, MLA_NOPE:MLA_NOPE + half] = -sin
    return tuple(jnp.asarray(t, F32) for t in (c, sa, sb))


def _swa_rope_tables(pos):
    half = ROT_DIM // 2
    inv = ROPE_THETA ** (-np.arange(half, dtype=np.float64) * 2.0 / ROT_DIM)
    ang = np.asarray(pos, np.float64)[:, None] * inv[None, :]
    cos, sin = np.cos(ang), np.sin(ang)
    p = ang.shape[0]
    c = np.ones((p, LANE)); sa = np.zeros((p, LANE)); sb = np.zeros((p, LANE))
    for o in (0, SWA_HD):
        c[:, o:o + half] = cos
        c[:, o + half:o + 2 * half] = cos
        sa[:, o + half:o + 2 * half] = sin
        sb[:, o:o + half] = -sin
    return tuple(jnp.asarray(t, F32) for t in (c, sa, sb))


def _rope_lanes(x, c, sa, sb, half):
    width = x.shape[1]
    reps = width // LANE
    if reps > 1:
        c = jnp.concatenate([c] * reps, axis=1)
        sa = jnp.concatenate([sa] * reps, axis=1)
        sb = jnp.concatenate([sb] * reps, axis=1)
    return x * c + pltpu.roll(x, half, 1) * sa + pltpu.roll(x, width - half, 1) * sb


def _rope_sublanes(xt, ct, sat, sbt, half):
    feats = xt.shape[0]
    reps = feats // LANE
    ct = jnp.concatenate([ct] * reps, axis=0)
    sat = jnp.concatenate([sat] * reps, axis=0)
    sbt = jnp.concatenate([sbt] * reps, axis=0)
    return xt * ct + pltpu.roll(xt, half, 0) * sat + pltpu.roll(xt, feats - half, 0) * sbt


def _even_proj_kernel(x_ref, c_ref, sa_ref, sb_ref, ct_ref, sat_ref, sbt_ref, nm_ref, win_ref, qn_ref,
                      wuq_ref, kvn_ref, wuk_ref, wuv_ref, u_ref, q_ref, ckv_ref, kr_ref, *kv_refs):
    h = _rms(x_ref[...], nm_ref[...]).astype(BF16)
    z = jnp.dot(h, win_ref[...], preferred_element_type=F32)
    u_ref[...] = z[:, :S5_WIDTH]
    c, sa, sb = c_ref[...], sa_ref[...], sb_ref[...]
    o1 = S5_WIDTH + MLA_Q_LORA
    o2 = o1 + MLA_KV_LORA
    qn = _rms(z[:, S5_WIDTH:o1], qn_ref[...]).astype(BF16)
    ckv = _rms(z[:, o1:o2], kvn_ref[...])
    ckv_ref[...] = ckv
    kr = _rope_lanes(z[:, o2:], c, sa, sb, MLA_ROPE // 2)
    kr_ref[...] = kr
    nt_dims = (((1,), (1,)), ((), ()))
    if kv_refs:
        k_ref, vt_ref = kv_refs
        qt = lax.dot_general(wuq_ref[...], qn, nt_dims, preferred_element_type=F32)
        qt = _rope_sublanes(qt, ct_ref[...], sat_ref[...], sbt_ref[...], MLA_ROPE // 2)
        q_ref[0, 0] = qt.astype(BF16)
        ckvb = ckv.astype(BF16)
        k = jnp.dot(ckvb, wuk_ref[...], preferred_element_type=F32)
        k_ref[...] = (k + jnp.concatenate([kr] * MLA_HEADS, axis=1)).astype(BF16)
        vt_ref[0, 0] = lax.dot_general(wuv_ref[...], ckvb, nt_dims,
                                       preferred_element_type=F32).astype(BF16)
    else:
        q = jnp.dot(qn, wuq_ref[...], preferred_element_type=F32)
        q_ref[...] = _rope_lanes(q, c, sa, sb, MLA_ROPE // 2).astype(BF16)


def _even_proj(x, tabs, nm, win, qnorm, wuq, kvnorm, wuk, wuv, *, tm, n_seq, with_kv):
    rows = x.shape[0]
    nt = tabs[0].shape[0] // tm
    tabs_t = tuple(t.T for t in tabs)
    row = lambda w: pl.BlockSpec((tm, w), lambda i: (i, 0))
    tab = pl.BlockSpec((tm, LANE), lambda i: (i % nt, 0))
    tab_t = pl.BlockSpec((LANE, tm), lambda i: (0, i % nt))
    tblock = lambda w: pl.BlockSpec((1, 1, w, tm), lambda i: (i // nt, i % nt, 0, 0))
    qw = MLA_HEADS * HEAD_PAD
    vw = MLA_HEADS * MLA_V
    if with_kv:
        q_shape, q_spec = jax.ShapeDtypeStruct((n_seq, nt, qw, tm), BF16), tblock(qw)
    else:
        q_shape, q_spec = jax.ShapeDtypeStruct((rows, qw), BF16), row(qw)
    out_shape = [jax.ShapeDtypeStruct((rows, S5_WIDTH), F32), q_shape,
                 jax.ShapeDtypeStruct((rows, MLA_KV_LORA), F32),
                 jax.ShapeDtypeStruct((rows, LANE), F32)]
    out_specs = [row(S5_WIDTH), q_spec, row(MLA_KV_LORA), row(LANE)]
    if with_kv:
        out_shape += [jax.ShapeDtypeStruct((rows, qw), BF16),
                      jax.ShapeDtypeStruct((n_seq, nt, vw, tm), BF16)]
        out_specs += [row(qw), tblock(vw)]
    return pl.pallas_call(
        _even_proj_kernel,
        grid=(rows // tm,),
        in_specs=[row(D_MODEL), tab, tab, tab, tab_t, tab_t, tab_t, _const_spec(nm.shape),
                  _const_spec(win.shape), _const_spec(qnorm.shape), _const_spec(wuq.shape),
                  _const_spec(kvnorm.shape), _const_spec(wuk.shape), _const_spec(wuv.shape)],
        out_specs=out_specs,
        out_shape=out_shape,
        compiler_params=_cparams(("arbitrary",)),
        name="even_proj",
    )(x, *tabs, *tabs_t, nm, win, qnorm, wuq, kvnorm, wuk, wuv)


S5_STRIP = 512


def _s5_kernel(u_ref, h0r_ref, h0i_ref, lamr_ref, lami_ref, ldt_ref, wb_ref, wcr_ref, wci_ref,
               d_ref, wglu_ref, bglu_ref, y_ref, hr_ref, hi_ref, ut_ref, xr_ref, xi_ref, yt_ref,
               *, nb, tt, transpose_io):
    @pl.when(pl.program_id(0) == 0)
    def _():
        xr_ref[0:nb, :] = h0r_ref[...]
        xi_ref[0:nb, :] = h0i_ref[...]

    if transpose_io:
        for t in range(tt):
            ut_ref[t * nb:(t + 1) * nb, :] = u_ref[:, t, :]
    else:
        ut_ref[...] = u_ref[0]
    ub = ut_ref[...].astype(BF16)

    lr = jnp.minimum(lamr_ref[...], -1e-4)
    li = lami_ref[...]
    dt = jnp.exp(ldt_ref[...])
    mag = jnp.exp(lr * dt)
    ar = mag * jnp.cos(li * dt)
    ai = mag * jnp.sin(li * dt)
    den = lr * lr + li * li
    zr = ((ar - 1.0) * lr + ai * li) / den
    zi = (ai * lr - (ar - 1.0) * li) / den

    ucols = S5_STRIP // S5_STATE * S5_GROUP
    ys = []
    for s in range(S5_NSTATE // S5_STRIP):
        cols = slice(s * S5_STRIP, (s + 1) * S5_STRIP)
        us = ub[:, s * ucols:(s + 1) * ucols]
        br = jnp.dot(us, wb_ref[s, :, :S5_STRIP], preferred_element_type=F32)
        bi = jnp.dot(us, wb_ref[s, :, S5_STRIP:], preferred_element_type=F32)
        zrs, zis = zr[:, cols], zi[:, cols]
        xr_ref[nb:, cols] = zrs * br - zis * bi
        xi_ref[nb:, cols] = zrs * bi + zis * br
        ars = jnp.broadcast_to(ar[:, cols], (nb, S5_STRIP))
        ais = jnp.broadcast_to(ai[:, cols], (nb, S5_STRIP))
        hr, hi = xr_ref[0:nb, cols], xi_ref[0:nb, cols]
        for t in range(tt):
            r = slice((t + 1) * nb, (t + 2) * nb)
            hr, hi = (ars * hr - ais * hi + xr_ref[r, cols],
                      ars * hi + ais * hr + xi_ref[r, cols])
            xr_ref[r, cols] = hr
            xi_ref[r, cols] = hi
        xr_ref[0:nb, cols] = hr
        xi_ref[0:nb, cols] = hi
        ys.append(jnp.dot(xr_ref[nb:, cols].astype(BF16), wcr_ref[s], preferred_element_type=F32)
                  + jnp.dot(xi_ref[nb:, cols].astype(BF16), wci_ref[s],
                            preferred_element_type=F32))

    hr_ref[...] = xr_ref[0:nb, :]
    hi_ref[...] = xi_ref[0:nb, :]
    y = jnp.concatenate(ys, axis=1) + d_ref[...] * ut_ref[...]
    y = jax.nn.gelu(y)
    gate = jnp.dot(y.astype(BF16), wglu_ref[...], preferred_element_type=F32) + bglu_ref[...]
    y = y * jax.nn.sigmoid(gate)
    if transpose_io:
        groups = S5_WIDTH // LANE
        for c in range(groups):
            yt_ref[c] = y[:, c * LANE:(c + 1) * LANE]
        for n in range(nb):
            y_ref[n] = jnp.concatenate(
                [yt_ref[c, pl.ds(n, tt, stride=nb), :] for c in range(groups)],
                axis=1).astype(y_ref.dtype)
    else:
        y_ref[0] = y.astype(y_ref.dtype)


def _s5(u, h0r, h0i, lamr, lami, ldt, wb, wcr, wci, d, wglu, bglu, *, nb, tt, transpose_io):
    t_total = u.shape[1] if transpose_io else u.shape[0]
    if transpose_io:
        u_spec = pl.BlockSpec((nb, tt, S5_WIDTH), lambda i: (0, i, 0))
        y_shape = (nb, t_total, S5_WIDTH)
    else:
        u_spec = pl.BlockSpec((tt, nb, S5_WIDTH), lambda i: (i, 0, 0))
        y_shape = (t_total, nb, S5_WIDTH)
    consts = (h0r, h0i, lamr, lami, ldt, wb, wcr, wci, d, wglu, bglu)
    kern = functools.partial(_s5_kernel, nb=nb, tt=tt, transpose_io=transpose_io)
    return pl.pallas_call(
        kern,
        grid=(t_total // tt,),
        in_specs=[u_spec] + [_const_spec(c.shape) for c in consts],
        out_specs=[u_spec, _const_spec((nb, S5_NSTATE)), _const_spec((nb, S5_NSTATE))],
        out_shape=[jax.ShapeDtypeStruct(y_shape, BF16),
                   jax.ShapeDtypeStruct((nb, S5_NSTATE), F32),
                   jax.ShapeDtypeStruct((nb, S5_NSTATE), F32)],
        scratch_shapes=[pltpu.VMEM((nb * tt, S5_WIDTH), F32),
                        pltpu.VMEM((nb * (tt + 1), S5_NSTATE), F32),
                        pltpu.VMEM((nb * (tt + 1), S5_NSTATE), F32),
                        pltpu.VMEM((S5_WIDTH // LANE, nb * tt, LANE), F32)],
        compiler_params=_cparams(("arbitrary",)),
        name="s5_mixer",
    )(u, *consts)


MLA_LOOKAHEAD = 2


def _mla_attn_kernel(qt_ref, k_ref, vt_ref, o_ref, m_ref, l_ref, acc_ref, *, tq):
    qi = pl.program_id(1)
    m_ref[...] = jnp.full(m_ref.shape, NEG, F32)
    l_ref[...] = jnp.zeros(l_ref.shape, F32)
    acc_ref[...] = jnp.zeros(acc_ref.shape, F32)

    def causal_mask(nblk):
        key = lax.broadcasted_iota(jnp.int32, (nblk * tq, tq), 0)
        qry = lax.broadcasted_iota(jnp.int32, (nblk * tq, tq), 1)
        return key <= qry + (nblk - 1) * tq

    def process(kb0, nblk, mask):
        keys = pl.ds(pl.multiple_of(kb0 * tq, tq), nblk * tq)

        def scores(h):
            hl = slice(h * HEAD_PAD, (h + 1) * HEAD_PAD)
            return jnp.dot(k_ref[0, keys, hl], qt_ref[0, 0, hl, :],
                           preferred_element_type=F32)

        pending = [scores(h) for h in range(MLA_LOOKAHEAD)]
        for h in range(MLA_HEADS):
            st = pending.pop(0) * MLA_SCALE
            if h + MLA_LOOKAHEAD < MLA_HEADS:
                pending.append(scores(h + MLA_LOOKAHEAD))
            if mask is not None:
                st = jnp.where(mask, st, NEG)
            j = h // 2
            vts = [vt_ref[0, kb0 + i, j * LANE:(j + 1) * LANE, :] for i in range(nblk)]
            vt = vts[0] if nblk == 1 else jnp.concatenate(vts, axis=1)
            m_old = m_ref[h]
            m_new = jnp.maximum(m_old, jnp.max(st, axis=0, keepdims=True))
            alpha = jnp.exp(m_old - m_new)
            pt = jnp.exp(st - m_new)
            l_ref[h] = alpha * l_ref[h] + jnp.sum(pt, axis=0, keepdims=True)
            m_ref[h] = m_new
            acc_ref[h] = alpha * acc_ref[h] + jnp.dot(vt, pt.astype(BF16),
                                                      preferred_element_type=F32)

    n_wide = qi // 2

    def wide_step(i, carry):
        process(i * 2, 2, None)
        return carry

    lax.fori_loop(0, n_wide, wide_step, 0)

    @pl.when(qi % 2 == 0)
    def _():
        process(qi, 1, causal_mask(1))

    @pl.when(qi % 2 == 1)
    def _():
        process(qi - 1, 2, causal_mask(2))

    half = lax.broadcasted_iota(jnp.int32, (LANE, tq), 0) < MLA_V
    for j in range(MLA_HEADS // 2):
        ot = jnp.where(half, acc_ref[2 * j] / l_ref[2 * j], acc_ref[2 * j + 1] / l_ref[2 * j + 1])
        o_ref[0, :, j * LANE:(j + 1) * LANE] = ot.T.astype(o_ref.dtype)


def _mla_attn(qt, k, vt):
    n, nblocks, qw, tq = qt.shape
    t = k.shape[1]
    kern = functools.partial(_mla_attn_kernel, tq=tq)
    return pl.pallas_call(
        kern,
        grid=(n, nblocks),
        in_specs=[pl.BlockSpec((1, 1, qw, tq), lambda b, i: (b, i, 0, 0)),
                  pl.BlockSpec((1, t, k.shape[2]), lambda b, i: (b, 0, 0)),
                  pl.BlockSpec((1, nblocks, vt.shape[2], tq), lambda b, i: (b, 0, 0, 0))],
        out_specs=pl.BlockSpec((1, tq, MLA_HEADS * MLA_V), lambda b, i: (b, i, 0)),
        out_shape=jax.ShapeDtypeStruct((n, t, MLA_HEADS * MLA_V), BF16),
        scratch_shapes=[pltpu.VMEM((MLA_HEADS, 1, tq), F32),
                        pltpu.VMEM((MLA_HEADS, 1, tq), F32),
                        pltpu.VMEM((MLA_HEADS, LANE, tq), F32)],
        compiler_params=_cparams(("arbitrary", "arbitrary")),
        name="mla_prompt_attn",
    )(qt, k, vt)


DEC_PAGES = 32
DEC_SUB = 4
DEC_AHEAD = 2
DEC_SLOTS = DEC_AHEAD + 1
DEC_KEYS = DEC_PAGES * PAGE


def _qlat_kernel(q_ref, wk_ref, o_ref):
    for h in range(MLA_HEADS):
        o_ref[:, h * MLA_KV_LORA:(h + 1) * MLA_KV_LORA] = jnp.dot(
            q_ref[:, h * HEAD_PAD:(h + 1) * HEAD_PAD], wk_ref[h], preferred_element_type=F32)


def _qlat(q_big, wk_t):
    rows = q_big.shape[0]
    return pl.pallas_call(
        _qlat_kernel,
        grid=(1,),
        in_specs=[_const_spec(q_big.shape), _const_spec(wk_t.shape)],
        out_specs=_const_spec((rows, MLA_HEADS * MLA_KV_LORA)),
        out_shape=jax.ShapeDtypeStruct((rows, MLA_HEADS * MLA_KV_LORA), F32),
        compiler_params=_cparams(("arbitrary",)),
        name="mla_q_absorb",
    )(q_big, wk_t)


def _olat_kernel(o_ref, wv_ref, out_ref):
    for j in range(MLA_HEADS // 2):
        acc = None
        for h in (2 * j, 2 * j + 1):
            part = jnp.dot(o_ref[:, h * MLA_KV_LORA:(h + 1) * MLA_KV_LORA].astype(BF16),
                           wv_ref[:, h * LANE:(h + 1) * LANE], preferred_element_type=F32)
            acc = part if acc is None else acc + part
        out_ref[:, j * LANE:(j + 1) * LANE] = acc.astype(out_ref.dtype)


def _olat(o_lat, wv_big):
    rows = o_lat.shape[0]
    return pl.pallas_call(
        _olat_kernel,
        grid=(1,),
        in_specs=[_const_spec(o_lat.shape), _const_spec(wv_big.shape)],
        out_specs=_const_spec((rows, MLA_HEADS * MLA_V)),
        out_shape=jax.ShapeDtypeStruct((rows, MLA_HEADS * MLA_V), BF16),
        compiler_params=_cparams(("arbitrary",)),
        name="mla_v_expand",
    )(o_lat, wv_big)


def _page_copies(cache_ckv, cache_kr, ckv_buf, kr_buf, sem, slot, page, p):
    rows = pl.ds(p * PAGE, PAGE)
    return (pltpu.make_async_copy(cache_ckv.at[0, page], ckv_buf.at[slot, rows], sem.at[0, slot]),
            pltpu.make_async_copy(cache_kr.at[0, page], kr_buf.at[slot, p], sem.at[1, slot]))


def _mla_decode_kernel(ptc_ref, ptn_ref, ql_ref, qr_ref, ckvn_ref, krn_ref, cache_ckv, cache_kr,
                       o_ref, ckv_buf, kr_buf, sem, *, n_seq, n_chunks):
    b = pl.program_id(0)
    g0 = b * n_chunks

    def start_page(page, slot, p):
        for cp in _page_copies(cache_ckv, cache_kr, ckv_buf, kr_buf, sem, slot, page, p):
            cp.start()

    def wait_chunk(slot):
        for p in range(DEC_PAGES):
            for cp in _page_copies(cache_ckv, cache_kr, ckv_buf, kr_buf, sem, slot, 0, p):
                cp.wait()

    @pl.when(b == 0)
    def _():
        for a in range(DEC_AHEAD):
            for p in range(DEC_PAGES):
                start_page(ptc_ref[0, 0, a * DEC_PAGES + p], a, p)

    ql = ql_ref[0].astype(BF16)
    qr = qr_ref[0].astype(BF16)

    def chunk_body(c, carry):
        m, l, acc = carry
        slot = lax.rem(g0 + c, DEC_SLOTS)
        ahead_slot = lax.rem(g0 + c + DEC_AHEAD, DEC_SLOTS)
        wait_chunk(slot)
        wraps = c + DEC_AHEAD >= n_chunks
        ahead_base = jnp.where(wraps, c + DEC_AHEAD - n_chunks, c + DEC_AHEAD) * DEC_PAGES

        def ahead_page(p):
            return jnp.where(wraps, ptn_ref[0, 0, ahead_base + p], ptc_ref[0, 0, ahead_base + p])

        kbs, scores = [], []
        for i in range(DEC_PAGES // DEC_SUB):
            kb = ckv_buf[slot, i * DEC_SUB * PAGE:(i + 1) * DEC_SUB * PAGE, :].astype(BF16)
            krt = jnp.concatenate([kr_buf[slot, i * DEC_SUB + p] for p in range(DEC_SUB)],
                                  axis=1).astype(BF16)
            kbs.append(kb)
            scores.append(
                (lax.dot_general(ql, kb, (((1,), (1,)), ((), ())), preferred_element_type=F32)
                 + jnp.dot(qr, krt, preferred_element_type=F32)) * MLA_SCALE)
            for p in range(i * DEC_SUB, (i + 1) * DEC_SUB):
                start_page(ahead_page(p), ahead_slot, p)
        for kb, s in zip(kbs, scores):
            m_new = jnp.maximum(m, jnp.max(s, axis=1, keepdims=True))
            alpha = jnp.exp(m - m_new)
            pr = jnp.exp(s - m_new)
            l = alpha * l + jnp.sum(pr, axis=1, keepdims=True)
            acc = alpha * acc + jnp.dot(pr.astype(BF16), kb, preferred_element_type=F32)
            m = m_new
        return m, l, acc

    init = (jnp.full((MLA_HEADS, 1), NEG, F32), jnp.zeros((MLA_HEADS, 1), F32),
            jnp.zeros((MLA_HEADS, MLA_KV_LORA), F32))
    m, l, acc = lax.fori_loop(0, n_chunks, chunk_body, init)

    @pl.when(b == n_seq - 1)
    def _():
        for a in range(DEC_AHEAD):
            wait_chunk((n_seq * n_chunks + a) % DEC_SLOTS)

    cn = ckvn_ref[0].astype(BF16).astype(F32)
    kn = krn_ref[0].astype(BF16).astype(F32)
    s_new = (jnp.sum(ql.astype(F32) * cn, axis=1, keepdims=True)
             + jnp.sum(qr.astype(F32) * kn, axis=1, keepdims=True)) * MLA_SCALE
    m_new = jnp.maximum(m, s_new)
    alpha = jnp.exp(m - m_new)
    p_new = jnp.exp(s_new - m_new)
    l = alpha * l + p_new
    acc = alpha * acc + p_new.astype(BF16).astype(F32) * cn
    o_ref[0] = acc / l


def _mla_decode(page_table, q_lat, q_rope, ckv_new, kr_new, cache_ckv, cache_kr):
    n_seq, n_pages = page_table.shape
    n_chunks = n_pages // DEC_PAGES
    assert n_pages % DEC_PAGES == 0 and n_chunks >= DEC_AHEAD
    pt = page_table.reshape(n_seq, 1, n_pages)
    smem_row = lambda f: pl.BlockSpec((1, 1, n_pages), f, memory_space=pltpu.SMEM)
    per_seq = lambda a: pl.BlockSpec((1,) + a.shape[1:], lambda b: (b, 0, 0))
    kern = functools.partial(_mla_decode_kernel, n_seq=n_seq, n_chunks=n_chunks)
    return pl.pallas_call(
        kern,
        grid=(n_seq,),
        in_specs=[smem_row(lambda b: (b, 0, 0)),
                  smem_row(lambda b: (jnp.minimum(b + 1, n_seq - 1), 0, 0)),
                  per_seq(q_lat), per_seq(q_rope), per_seq(ckv_new), per_seq(kr_new),
                  pl.BlockSpec(memory_space=pl.ANY), pl.BlockSpec(memory_space=pl.ANY)],
        out_specs=pl.BlockSpec((1, MLA_HEADS, MLA_KV_LORA), lambda b: (b, 0, 0)),
        out_shape=jax.ShapeDtypeStruct((n_seq, MLA_HEADS, MLA_KV_LORA), F32),
        scratch_shapes=[pltpu.VMEM((DEC_SLOTS, DEC_KEYS, MLA_KV_LORA), F32),
                        pltpu.VMEM((DEC_SLOTS, DEC_PAGES, MLA_ROPE, PAGE), F32),
                        pltpu.SemaphoreType.DMA((2, DEC_SLOTS))],
        compiler_params=_cparams(("arbitrary",)),
        name="mla_paged_decode",
    )(pt, pt, q_lat, q_rope, ckv_new, kr_new, cache_ckv, cache_kr)


def _odd_proj_kernel(x_ref, c_ref, sa_ref, sb_ref, nm_ref, win_ref, q_ref, k_ref, v_ref):
    h = _rms(x_ref[...], nm_ref[...]).astype(BF16)
    z = jnp.dot(h, win_ref[...], preferred_element_type=F32)
    qk = _rope_lanes(z[:, :QK_ODD], c_ref[...], sa_ref[...], sb_ref[...], ROT_DIM // 2)
    q_ref[...] = qk[:, :SWA_HEADS * SWA_HD].astype(BF16)
    k_ref[...] = qk[:, SWA_HEADS * SWA_HD:]
    v_ref[...] = z[:, QK_ODD:]


def _odd_proj(x, tabs, nm, win, *, tm):
    rows = x.shape[0]
    nt = tabs[0].shape[0] // tm
    row = lambda w: pl.BlockSpec((tm, w), lambda i: (i, 0))
    tab = pl.BlockSpec((tm, LANE), lambda i: (i % nt, 0))
    kvw = SWA_KV_HEADS * SWA_HD
    return pl.pallas_call(
        _odd_proj_kernel,
        grid=(rows // tm,),
        in_specs=[row(D_MODEL), tab, tab, tab, _const_spec(nm.shape), _const_spec(win.shape)],
        out_specs=[row(SWA_HEADS * SWA_HD), row(kvw), row(kvw)],
        out_shape=[jax.ShapeDtypeStruct((rows, SWA_HEADS * SWA_HD), BF16),
                   jax.ShapeDtypeStruct((rows, kvw), F32),
                   jax.ShapeDtypeStruct((rows, kvw), F32)],
        compiler_params=_cparams(("arbitrary",)),
        name="odd_proj",
    )(x, *tabs, nm, win)


def _swa_attn_kernel(sink_ref, q_ref, kp_ref, kc_ref, vp_ref, vc_ref, o_ref):
    b = pl.program_id(1)
    kk = jnp.concatenate([kp_ref[0], kc_ref[0]], axis=0).astype(BF16)
    vv = jnp.concatenate([vp_ref[0], vc_ref[0]], axis=0).astype(BF16)
    rows = SWA_GQ * WINDOW
    r = lax.broadcasted_iota(jnp.int32, (rows, 2 * WINDOW), 0)
    j = lax.broadcasted_iota(jnp.int32, (rows, 2 * WINDOW), 1)
    rel = jnp.bitwise_and(r, WINDOW - 1) + WINDOW - j
    valid = (rel >= 0) & (rel <= WINDOW) & ((j >= WINDOW) | (b > 0))
    rcol = lax.broadcasted_iota(jnp.int32, (rows, 1), 0)
    outs = []
    scores = []
    for kh in range(SWA_KV_HEADS):
        qg = jnp.concatenate(
            [q_ref[0, :, (kh * SWA_GQ + g) * SWA_HD:(kh * SWA_GQ + g + 1) * SWA_HD]
             for g in range(SWA_GQ)], axis=0)
        khd = kk[:, kh * SWA_HD:(kh + 1) * SWA_HD]
        scores.append(lax.dot_general(qg, khd, (((1,), (1,)), ((), ())),
                                      preferred_element_type=F32))
    for kh in range(SWA_KV_HEADS):
        vhd = vv[:, kh * SWA_HD:(kh + 1) * SWA_HD]
        s = jnp.where(valid, scores[kh] * SWA_SCALE, NEG)
        sink = jnp.full((rows, 1), sink_ref[kh * SWA_GQ + SWA_GQ - 1], F32)
        for g in range(SWA_GQ - 2, -1, -1):
            sink = jnp.where(rcol < (g + 1) * WINDOW, sink_ref[kh * SWA_GQ + g], sink)
        m = jnp.maximum(jnp.max(s, axis=1, keepdims=True), sink)
        p = jnp.exp(s - m)
        l = jnp.sum(p, axis=1, keepdims=True) + jnp.exp(sink - m)
        o = jnp.dot(p.astype(BF16), vhd, preferred_element_type=F32) / l
        outs += [o[g * WINDOW:(g + 1) * WINDOW] for g in range(SWA_GQ)]
    o_ref[0] = jnp.concatenate(outs, axis=1).astype(o_ref.dtype)


def _swa_attn(sinks, q, k, v):
    n, t, _ = q.shape
    kvw = SWA_KV_HEADS * SWA_HD
    prev = pl.BlockSpec((1, WINDOW, kvw), lambda a, i: (a, jnp.maximum(i - 1, 0), 0))
    cur = pl.BlockSpec((1, WINDOW, kvw), lambda a, i: (a, i, 0))
    qspec = pl.BlockSpec((1, WINDOW, SWA_HEADS * SWA_HD), lambda a, i: (a, i, 0))
    return pl.pallas_call(
        _swa_attn_kernel,
        grid=(n, t // WINDOW),
        in_specs=[pl.BlockSpec(memory_space=pltpu.SMEM), qspec, prev, cur, prev, cur],
        out_specs=qspec,
        out_shape=jax.ShapeDtypeStruct((n, t, SWA_HEADS * SWA_HD), BF16),
        compiler_params=_cparams(("arbitrary", "arbitrary")),
        name="swa_prompt_attn",
    )(sinks, q, k, k, v, v)


SWA_DEC_SEQS = 8


def _swa_decode_kernel(sink_ref, q_ref, kn_ref, vn_ref, ck_ref, cv_ref, o_ref):
    q = q_ref[...]
    hgrp = lax.broadcasted_iota(jnp.int32, (1, SWA_HEADS, 1), 1) // SWA_GQ
    hidx = lax.broadcasted_iota(jnp.int32, (1, SWA_HEADS, 1), 1)
    sink = jnp.zeros((1, SWA_HEADS, 1), F32)
    for h in range(SWA_HEADS):
        sink = jnp.where(hidx == h, sink_ref[h], sink)
    s = jnp.zeros((q.shape[0], SWA_HEADS, WINDOW), F32)
    for kh in range(SWA_KV_HEADS):
        skh = jnp.einsum('nhd,ndj->nhj', q, ck_ref[:, kh].astype(BF16),
                         preferred_element_type=F32)
        s = jnp.where(hgrp == kh, skh, s)
    s = s * SWA_SCALE
    qf = q.astype(F32)
    kn = kn_ref[...].astype(BF16).astype(F32)
    vn = vn_ref[...].astype(BF16).astype(F32)
    s_new = jnp.sum(qf * kn, axis=2, keepdims=True) * SWA_SCALE
    m = jnp.maximum(jnp.maximum(jnp.max(s, axis=2, keepdims=True), s_new), sink)
    p = jnp.exp(s - m)
    p_new = jnp.exp(s_new - m)
    l = jnp.sum(p, axis=2, keepdims=True) + p_new + jnp.exp(sink - m)
    o = p_new.astype(BF16).astype(F32) * vn
    for kh in range(SWA_KV_HEADS):
        pk = jnp.where(hgrp == kh, p, 0.0).astype(BF16)
        o = o + jnp.einsum('nhj,ndj->nhd', pk, cv_ref[:, kh].astype(BF16),
                           preferred_element_type=F32)
    o_ref[...] = (o / l).astype(o_ref.dtype)


def _swa_decode(sinks, q, k_new, v_new, cache_k, cache_v):
    n = q.shape[0]
    nb = SWA_DEC_SEQS
    seq = pl.BlockSpec((nb, SWA_HEADS, SWA_HD), lambda i: (i, 0, 0))
    cache = pl.BlockSpec((nb, SWA_KV_HEADS, SWA_HD, WINDOW), lambda i: (i, 0, 0, 0))
    return pl.pallas_call(
        _swa_decode_kernel,
        grid=(n // nb,),
        in_specs=[pl.BlockSpec(memory_space=pltpu.SMEM), seq, seq, seq, cache, cache],
        out_specs=seq,
        out_shape=jax.ShapeDtypeStruct((n, SWA_HEADS, SWA_HD), BF16),
        compiler_params=_cparams(("arbitrary",)),
        name="swa_decode_attn",
    )(sinks, q, k_new, v_new, cache_k, cache_v)


def _post_kernel(*refs, n_mix, decode, final, tm):
    it = iter(refs)
    x_ref = next(it)
    mix_refs = [next(it) for _ in range(n_mix)]
    wo_refs = [next(it) for _ in range(n_mix)]
    nf_ref, wg_ref, wu_ref, cw_ref, cb_ref, wd_ref = (next(it) for _ in range(6))
    if decode:
        buf0_ref, buf1_ref = next(it), next(it)
    fn_ref = next(it) if final else None
    y_ref = next(it)
    g_ref = next(it)
    act_ref = next(it)
    carry_ref = None if decode else next(it)

    x1 = x_ref[...]
    for a_ref, w_ref in zip(mix_refs, wo_refs):
        x1 = x1 + jnp.dot(a_ref[...], w_ref[...], preferred_element_type=F32)
    h2 = _rms(x1, nf_ref[...]).astype(BF16)

    if not decode:
        @pl.when(pl.program_id(1) == 0)
        def _():
            carry_ref[...] = jnp.zeros_like(carry_ref)
        row = lax.broadcasted_iota(jnp.int32, (tm, FF_CHUNK), 0)

    for c in range(D_FF // FF_CHUNK):
        sl = slice(c * FF_CHUNK, (c + 1) * FF_CHUNK)
        g = jnp.dot(h2, wg_ref[:, sl], preferred_element_type=F32)
        u = jnp.dot(h2, wu_ref[:, sl], preferred_element_type=F32)
        if decode:
            gm2, gm1 = buf0_ref[:, sl], buf1_ref[:, sl]
            g_ref[:, sl] = g
        else:
            prev = carry_ref[:, sl]
            p6, p7 = prev[SUBLANE - 2:SUBLANE - 1, :], prev[SUBLANE - 1:SUBLANE, :]
            gm1 = jnp.where(row == 0, p7, pltpu.roll(g, 1, 0))
            gm2 = jnp.where(row == 0, p6, jnp.where(row == 1, p7, pltpu.roll(g, 2, 0)))
            last = g[tm - SUBLANE:tm, :]
            carry_ref[:, sl] = last
            g_ref[0, :, sl] = last
        cc = cb_ref[:, sl] + cw_ref[0:1, sl] * gm2 + cw_ref[1:2, sl] * gm1 + cw_ref[2:3, sl] * g
        act_ref[:, sl] = (jax.nn.gelu(cc) * u).astype(BF16)

    x2 = x1 + jnp.dot(act_ref[...], wd_ref[...], preferred_element_type=F32)
    y_ref[...] = _rms(x2, fn_ref[...]) if final else x2


def _post(x, mixes, wos, nf, wg, wu, cw, cb, wd, *, n_seq, tm, conv_bufs=None, final_norm=None):
    rows = x.shape[0]
    decode = conv_bufs is not None
    nt = rows // n_seq // tm if not decode else rows // tm
    grid = (1, nt) if decode else (n_seq, nt)
    row = lambda w: pl.BlockSpec((tm, w), lambda a, i: (a * nt + i, 0))
    args = [x] + list(mixes) + list(wos) + [nf, wg, wu, cw, cb, wd]
    in_specs = ([row(D_MODEL)] + [row(m.shape[1]) for m in mixes]
                + [_single_spec(w.shape) for w in wos]
                + [_const_spec(nf.shape), _single_spec(wg.shape), _single_spec(wu.shape),
                   _const_spec(cw.shape), _const_spec(cb.shape), _single_spec(wd.shape)])
    if decode:
        args += list(conv_bufs)
        in_specs += [row(D_FF), row(D_FF)]
    if final_norm is not None:
        args.append(final_norm)
        in_specs.append(_const_spec(final_norm.shape))
    if decode:
        g_shape = jax.ShapeDtypeStruct((rows, D_FF), F32)
        g_spec = row(D_FF)
    else:
        g_shape = jax.ShapeDtypeStruct((n_seq, SUBLANE, D_FF), F32)
        g_spec = pl.BlockSpec((1, SUBLANE, D_FF), lambda a, i: (a, 0, 0))
    scratch = [pltpu.VMEM((tm, D_FF), BF16)]
    if not decode:
        scratch.append(pltpu.VMEM((SUBLANE, D_FF), F32))
    kern = functools.partial(_post_kernel, n_mix=len(mixes), decode=decode,
                             final=final_norm is not None, tm=tm)
    return pl.pallas_call(
        kern,
        grid=grid,
        in_specs=in_specs,
        out_specs=[row(D_MODEL), g_spec],
        out_shape=[jax.ShapeDtypeStruct((rows, D_MODEL), F32), g_shape],
        scratch_shapes=scratch,
        compiler_params=_cparams(("arbitrary", "arbitrary")),
        name="post_decode" if decode else "post_prompt",
    )(*args)


def _prep_even(e_w_in, e_w_uq, e_w_uk, e_w_uv, e_s5_b_re, e_s5_b_im, e_s5_c_re, e_s5_c_im):
    o1 = S5_WIDTH
    o2 = o1 + MLA_Q_LORA
    o3 = o2 + MLA_KV_LORA
    zpad = lambda n: jnp.zeros((D_MODEL, n), F32)
    win = jnp.concatenate([e_w_in[:, :o3], zpad(MLA_NOPE), e_w_in[:, o3:],
                           zpad(HEAD_PAD - MLA_NOPE - MLA_ROPE)], axis=1).astype(BF16)
    pad_last = lambda w, n: jnp.pad(w, ((0, 0), (0, 0), (0, n - w.shape[2])))
    wuq = pad_last(e_w_uq, HEAD_PAD).reshape(MLA_Q_LORA, MLA_HEADS * HEAD_PAD).astype(BF16)
    wuk = pad_last(e_w_uk, HEAD_PAD).reshape(MLA_KV_LORA, MLA_HEADS * HEAD_PAD).astype(BF16)
    wuv = e_w_uv.reshape(MLA_KV_LORA, MLA_HEADS * MLA_V).astype(BF16)
    wk_t = jnp.pad(jnp.transpose(e_w_uk, (1, 2, 0)),
                   ((0, 0), (0, HEAD_PAD - MLA_NOPE), (0, 0))).astype(BF16)
    wv4 = e_w_uv.reshape(MLA_KV_LORA, MLA_HEADS // 2, 2, MLA_V)
    zv = jnp.zeros_like(wv4[:, :, 0])
    wv_big = jnp.stack([jnp.concatenate([wv4[:, :, 0], zv], axis=-1),
                        jnp.concatenate([zv, wv4[:, :, 1]], axis=-1)], axis=2)
    wv_big = wv_big.reshape(MLA_KV_LORA, MLA_HEADS * LANE).astype(BF16)
    gs = S5_STRIP // S5_STATE
    ns = S5_GROUPS // gs
    eye = jnp.eye(gs, dtype=F32)
    bd_in = lambda b: jnp.einsum('sgph,gk->sghkp', b.reshape(ns, gs, S5_STATE, S5_GROUP),
                                 eye).reshape(ns, gs * S5_GROUP, S5_STRIP)
    bd_out = lambda c: jnp.einsum('sghp,gk->sgpkh', c.reshape(ns, gs, S5_GROUP, S5_STATE),
                                  eye).reshape(ns, S5_STRIP, gs * S5_GROUP)
    wb = jnp.concatenate([bd_in(e_s5_b_re), bd_in(e_s5_b_im)], axis=2).astype(BF16)
    wcr = bd_out(e_s5_c_re).astype(BF16)
    wci = (-bd_out(e_s5_c_im)).astype(BF16)
    return win, wuq, wuk, wuv, wk_t, wv_big, wb, wcr, wci


def kernel(x_prompt, x_sample, page_table, state_s5_re, state_s5_im, cache_mla_ckv, cache_mla_krope,
           cache_swa_k, cache_swa_v, state_ffn_conv, norm_mix, norm_ffn, final_norm, e_w_in,
           e_s5_lam_re, e_s5_lam_im, e_s5_log_dt, e_s5_b_re, e_s5_b_im, e_s5_c_re, e_s5_c_im, e_s5_d,
           e_s5_w_glu, e_s5_b_glu, e_q_norm, e_w_uq, e_kv_norm, e_w_uk, e_w_uv, e_w_out, o_w_in,
           o_sinks, o_w_out, f_w_gate, f_w_up, f_conv_w, f_conv_b, f_w_down):
    n_p, t_p, _ = x_prompt.shape
    n_s, t_s, _ = x_sample.shape
    assert t_s == 1
    past_len = page_table.shape[1] * PAGE
    row2 = lambda v: v.reshape(1, -1)

    (win, wuq, wuk, wuv, wk_t, wv_big, wb, wcr, wci) = _prep_even(
        e_w_in[0], e_w_uq[0], e_w_uk[0], e_w_uv[0], e_s5_b_re[0], e_s5_b_im[0],
        e_s5_c_re[0], e_s5_c_im[0])
    lamr, lami = row2(e_s5_lam_re[0]), row2(e_s5_lam_im[0])
    ldt = row2(jnp.repeat(e_s5_log_dt[0], S5_STATE))
    s5_consts = (lamr, lami, ldt, wb, wcr, wci, row2(e_s5_d[0]), e_s5_w_glu[0].astype(BF16),
                 row2(e_s5_b_glu[0]))
    wout_s5 = e_w_out[0][:S5_WIDTH].astype(BF16)
    wout_mla = e_w_out[0][S5_WIDTH:].astype(BF16)
    owin = o_w_in[0].astype(BF16)
    owout = o_w_out[0].astype(BF16)
    ffn = [(row2(norm_ffn[l]), f_w_gate[l].astype(BF16), f_w_up[l].astype(BF16), f_conv_w[l],
            row2(f_conv_b[l]), f_w_down[l].astype(BF16)) for l in range(2)]
    nm0, nm1 = row2(norm_mix[0]), row2(norm_mix[1])
    qnorm, kvnorm = row2(e_q_norm[0]), row2(e_kv_norm[0])
    fnorm = row2(final_norm)
    sinks = o_sinks[0]

    xp = x_prompt.reshape(n_p * t_p, D_MODEL)
    pos_p = np.arange(t_p)
    u, qt_p, ckv_p, kr_p, k_big, vt_p = _even_proj(
        xp, _mla_rope_tables(pos_p), nm0, win, qnorm, wuq.T, kvnorm, wuk, wuv.T, tm=256, n_seq=n_p,
        with_kv=True)
    zeros_state = jnp.zeros((n_p, S5_NSTATE), F32)
    y_s5, p_hr, p_hi = _s5(u.reshape(n_p, t_p, S5_WIDTH), zeros_state, zeros_state, *s5_consts,
                           nb=n_p, tt=64, transpose_io=True)
    o_mla = _mla_attn(qt_p, k_big.reshape(n_p, t_p, -1), vt_p)
    x1, conv0 = _post(xp, [y_s5.reshape(n_p * t_p, -1), o_mla.reshape(n_p * t_p, -1)],
                      [wout_s5, wout_mla], *ffn[0], n_seq=n_p, tm=512)
    q1, k1, v1 = _odd_proj(x1, _swa_rope_tables(pos_p), nm1, owin, tm=256)
    kvw = SWA_KV_HEADS * SWA_HD
    o_swa = _swa_attn(sinks, q1.reshape(n_p, t_p, -1), k1.reshape(n_p, t_p, kvw),
                      v1.reshape(n_p, t_p, kvw))
    y_p, conv1 = _post(x1, [o_swa.reshape(n_p * t_p, -1)], [owout], *ffn[1], n_seq=n_p, tm=512,
                       final_norm=fnorm)

    y_prompt = y_p.reshape(n_p, t_p, D_MODEL)
    p_s5_re = p_hr.reshape(1, n_p, S5_GROUPS, S5_STATE)
    p_s5_im = p_hi.reshape(1, n_p, S5_GROUPS, S5_STATE)
    p_ckv = ckv_p.reshape(1, n_p, t_p, MLA_KV_LORA)
    p_krope = kr_p[:, MLA_NOPE:MLA_NOPE + MLA_ROPE].reshape(1, n_p, t_p, MLA_ROPE)
    w_keep = min(WINDOW, t_p)
    p_swa_k = k1.reshape(n_p, t_p, SWA_KV_HEADS, SWA_HD)[None, :, t_p - w_keep:]
    p_swa_v = v1.reshape(n_p, t_p, SWA_KV_HEADS, SWA_HD)[None, :, t_p - w_keep:]
    p_conv = jnp.stack([conv0[:, SUBLANE - 2:], conv1[:, SUBLANE - 2:]])

    xs = x_sample.reshape(n_s, D_MODEL)
    pos_s = np.full((n_s,), past_len)
    u_s, qs_big, ckv_s, kr_s = _even_proj(
        xs, _mla_rope_tables(pos_s), nm0, win, qnorm, wuq, kvnorm, wuk, wuv, tm=n_s, n_seq=n_s,
        with_kv=False)
    ys_s5, s_hr, s_hi = _s5(u_s.reshape(1, n_s, S5_WIDTH), state_s5_re[0].reshape(n_s, S5_NSTATE),
                            state_s5_im[0].reshape(n_s, S5_NSTATE), *s5_consts,
                            nb=n_s, tt=1, transpose_io=False)
    q_lat = _qlat(qs_big, wk_t).reshape(n_s, MLA_HEADS, MLA_KV_LORA)
    q_rope = qs_big.reshape(n_s, MLA_HEADS, HEAD_PAD)[:, :, MLA_NOPE:MLA_NOPE + MLA_ROPE]
    kr_s32 = kr_s[:, MLA_NOPE:MLA_NOPE + MLA_ROPE]
    o_lat = _mla_decode(page_table, q_lat, q_rope, ckv_s.reshape(n_s, 1, MLA_KV_LORA),
                        kr_s32.reshape(n_s, 1, MLA_ROPE), cache_mla_ckv,
                        jnp.swapaxes(cache_mla_krope, 2, 3))
    os_mla = _olat(o_lat.reshape(n_s, MLA_HEADS * MLA_KV_LORA), wv_big)
    xs1, g0 = _post(xs, [ys_s5.reshape(n_s, -1), os_mla], [wout_s5, wout_mla], *ffn[0],
                    n_seq=n_s, tm=n_s, conv_bufs=(state_ffn_conv[0, :, 0], state_ffn_conv[0, :, 1]))
    qs1, ks1, vs1 = _odd_proj(xs1, _swa_rope_tables(pos_s), nm1, owin, tm=n_s)
    expand = lambda a: jnp.repeat(a.reshape(n_s, SWA_KV_HEADS, SWA_HD), SWA_GQ, axis=1)
    os_swa = _swa_decode(sinks, qs1.reshape(n_s, SWA_HEADS, SWA_HD), expand(ks1), expand(vs1),
                         jnp.transpose(cache_swa_k[0], (0, 2, 3, 1)),
                         jnp.transpose(cache_swa_v[0], (0, 2, 3, 1)))
    ys, g1 = _post(xs1, [os_swa.reshape(n_s, -1)], [owout], *ffn[1], n_seq=n_s, tm=n_s,
                   conv_bufs=(state_ffn_conv[1, :, 0], state_ffn_conv[1, :, 1]), final_norm=fnorm)

    y_sample = ys.reshape(n_s, 1, D_MODEL)
    s_s5_re = s_hr.reshape(1, n_s, S5_GROUPS, S5_STATE)
    s_s5_im = s_hi.reshape(1, n_s, S5_GROUPS, S5_STATE)
    s_ckv = ckv_s.reshape(1, n_s, 1, MLA_KV_LORA)
    s_krope = kr_s32.reshape(1, n_s, 1, MLA_ROPE)
    s_swa_k = ks1.reshape(1, n_s, 1, SWA_KV_HEADS, SWA_HD)
    s_swa_v = vs1.reshape(1, n_s, 1, SWA_KV_HEADS, SWA_HD)
    s_conv = jnp.stack([jnp.stack([state_ffn_conv[0, :, 1], g0], axis=1),
                        jnp.stack([state_ffn_conv[1, :, 1], g1], axis=1)])

    return (y_prompt, y_sample, p_s5_re, p_s5_im, p_ckv, p_krope, p_swa_k, p_swa_v, p_conv,
            s_s5_re, s_s5_im, s_ckv, s_krope, s_swa_k, s_swa_v, s_conv)
```

```python
import functools
import math

import numpy as np
import jax
import jax.numpy as jnp
from jax import lax
from jax.experimental import pallas as pl
from jax.experimental.pallas import tpu as pltpu

F32 = jnp.float32
BF16 = jnp.bfloat16

D_MODEL = 1024
S5_WIDTH = 512
S5_GROUPS = 32
S5_GROUP = 16
S5_STATE = 64
S5_NSTATE = S5_GROUPS * S5_STATE
MLA_HEADS = 16
MLA_NOPE = 64
MLA_ROPE = 32
MLA_V = 64
MLA_Q_LORA = 384
MLA_KV_LORA = 256
MLA_THETA = 10000.0
MLA_SCALE = 1.0 / math.sqrt(MLA_NOPE + MLA_ROPE)
SWA_HEADS = 16
SWA_KV_HEADS = 4
SWA_HD = 64
SWA_GQ = SWA_HEADS // SWA_KV_HEADS
WINDOW = 128
ROT_DIM = SWA_HD // 4
ROPE_THETA = 500000.0
SWA_SCALE = 1.0 / math.sqrt(SWA_HD)
D_FF = 2816
PAGE = 128
EPS = 1e-6
NEG = -1e30
LOG2E = math.log2(math.e)

LANE = 128
SUBLANE = 8
HEAD_PAD = LANE
Z_EVEN = S5_WIDTH + MLA_Q_LORA + MLA_KV_LORA + LANE
QK_ODD = (SWA_HEADS + SWA_KV_HEADS) * SWA_HD
FF_CHUNK = 256
VMEM_LIMIT = 48 * 1024 * 1024


def _cparams(sem):
    return pltpu.CompilerParams(dimension_semantics=sem, vmem_limit_bytes=VMEM_LIMIT)


def _rms(x, g):
    return x * lax.rsqrt(jnp.mean(x * x, axis=-1, keepdims=True) + EPS) * g


def _const_spec(shape):
    nd = len(shape)
    return pl.BlockSpec(shape, lambda *_: (0,) * nd)


def _single_spec(shape):
    nd = len(shape)
    return pl.BlockSpec(shape, lambda *_: (0,) * nd, pipeline_mode=pl.Buffered(1))


def _mla_rope_tables(pos):
    half = MLA_ROPE // 2
    inv = MLA_THETA ** (-np.arange(half, dtype=np.float64) * 2.0 / MLA_ROPE)
    ang = np.asarray(pos, np.float64)[:, None] * inv[None, :]
    cos, sin = np.cos(ang), np.sin(ang)
    p = ang.shape[0]
    c = np.zeros((p, LANE)); sa = np.zeros((p, LANE)); sb = np.zeros((p, LANE))
    c[:, :MLA_NOPE] = 1.0
    c[:, MLA_NOPE:MLA_NOPE + half] = cos
    c[:, MLA_NOPE + half:MLA_NOPE + 2 * half] = cos
    sa[:, MLA_NOPE + half:MLA_NOPE + 2 * half] = sin
    sb[:, MLA_NOPE:MLA_NOPE + half] = -sin
    return tuple(jnp.asarray(t, F32) for t in (c, sa, sb))


def _swa_rope_tables(pos):
    half = ROT_DIM // 2
    inv = ROPE_THETA ** (-np.arange(half, dtype=np.float64) * 2.0 / ROT_DIM)
    ang = np.asarray(pos, np.float64)[:, None] * inv[None, :]
    cos, sin = np.cos(ang), np.sin(ang)
    p = ang.shape[0]
    c = np.ones((p, LANE)); sa = np.zeros((p, LANE)); sb = np.zeros((p, LANE))
    for o in (0, SWA_HD):
        c[:, o:o + half] = cos
        c[:, o + half:o + 2 * half] = cos
        sa[:, o + half:o + 2 * half] = sin
        sb[:, o:o + half] = -sin
    return tuple(jnp.asarray(t, F32) for t in (c, sa, sb))


def _rope_lanes(x, c, sa, sb, half):
    width = x.shape[1]
    reps = width // LANE
    if reps > 1:
        c = jnp.concatenate([c] * reps, axis=1)
        sa = jnp.concatenate([sa] * reps, axis=1)
        sb = jnp.concatenate([sb] * reps, axis=1)
    return x * c + pltpu.roll(x, half, 1) * sa + pltpu.roll(x, width - half, 1) * sb


def _rope_sublanes(xt, ct, sat, sbt, half):
    feats = xt.shape[0]
    reps = feats // LANE
    ct = jnp.concatenate([ct] * reps, axis=0)
    sat = jnp.concatenate([sat] * reps, axis=0)
    sbt = jnp.concatenate([sbt] * reps, axis=0)
    return xt * ct + pltpu.roll(xt, half, 0) * sat + pltpu.roll(xt, feats - half, 0) * sbt


def _even_proj_kernel(x_ref, c_ref, sa_ref, sb_ref, ct_ref, sat_ref, sbt_ref, nm_ref, win_ref, qn_ref,
                      wuq_ref, kvn_ref, wuk_ref, wuv_ref, u_ref, q_ref, ckv_ref, kr_ref, *kv_refs):
    h = _rms(x_ref[...], nm_ref[...]).astype(BF16)
    z = jnp.dot(h, win_ref[...], preferred_element_type=F32)
    u_ref[...] = z[:, :S5_WIDTH]
    c, sa, sb = c_ref[...], sa_ref[...], sb_ref[...]
    o1 = S5_WIDTH + MLA_Q_LORA
    o2 = o1 + MLA_KV_LORA
    qn = _rms(z[:, S5_WIDTH:o1], qn_ref[...]).astype(BF16)
    ckv = _rms(z[:, o1:o2], kvn_ref[...])
    ckv_ref[...] = ckv
    kr = _rope_lanes(z[:, o2:], c, sa, sb, MLA_ROPE // 2)
    kr_ref[...] = kr
    nt_dims = (((1,), (1,)), ((), ()))
    if kv_refs:
        k_ref, vt_ref = kv_refs
        qt = lax.dot_general(wuq_ref[...], qn, nt_dims, preferred_element_type=F32)
        qt = _rope_sublanes(qt, ct_ref[...], sat_ref[...], sbt_ref[...], MLA_ROPE // 2)
        q_ref[0, 0] = qt.astype(BF16)
        ckvb = ckv.astype(BF16)
        k = jnp.dot(ckvb, wuk_ref[...], preferred_element_type=F32)
        k_ref[...] = (k + jnp.concatenate([kr] * MLA_HEADS, axis=1)).astype(BF16)
        vt_ref[0, 0] = lax.dot_general(wuv_ref[...], ckvb, nt_dims,
                                       preferred_element_type=F32).astype(BF16)
    else:
        q = jnp.dot(qn, wuq_ref[...], preferred_element_type=F32)
        q_ref[...] = _rope_lanes(q, c, sa, sb, MLA_ROPE // 2).astype(BF16)


def _even_proj(x, tabs, nm, win, qnorm, wuq, kvnorm, wuk, wuv, *, tm, n_seq, with_kv):
    rows = x.shape[0]
    nt = tabs[0].shape[0] // tm
    tabs_t = tuple(t.T for t in tabs)
    row = lambda w: pl.BlockSpec((tm, w), lambda i: (i, 0))
    tab = pl.BlockSpec((tm, LANE), lambda i: (i % nt, 0))
    tab_t = pl.BlockSpec((LANE, tm), lambda i: (0, i % nt))
    tblock = lambda w: pl.BlockSpec((1, 1, w, tm), lambda i: (i // nt, i % nt, 0, 0))
    qw = MLA_HEADS * HEAD_PAD
    vw = MLA_HEADS * MLA_V
    if with_kv:
        q_shape, q_spec = jax.ShapeDtypeStruct((n_seq, nt, qw, tm), BF16), tblock(qw)
    else:
        q_shape, q_spec = jax.ShapeDtypeStruct((rows, qw), BF16), row(qw)
    out_shape = [jax.ShapeDtypeStruct((rows, S5_WIDTH), F32), q_shape,
                 jax.ShapeDtypeStruct((rows, MLA_KV_LORA), F32),
                 jax.ShapeDtypeStruct((rows, LANE), F32)]
    out_specs = [row(S5_WIDTH), q_spec, row(MLA_KV_LORA), row(LANE)]
    if with_kv:
        out_shape += [jax.ShapeDtypeStruct((rows, qw), BF16),
                      jax.ShapeDtypeStruct((n_seq, nt, vw, tm), BF16)]
        out_specs += [row(qw), tblock(vw)]
    return pl.pallas_call(
        _even_proj_kernel,
        grid=(rows // tm,),
        in_specs=[row(D_MODEL), tab, tab, tab, tab_t, tab_t, tab_t, _const_spec(nm.shape),
                  _const_spec(win.shape), _const_spec(qnorm.shape), _const_spec(wuq.shape),
                  _const_spec(kvnorm.shape), _const_spec(wuk.shape), _const_spec(wuv.shape)],
        out_specs=out_specs,
        out_shape=out_shape,
        compiler_params=_cparams(("arbitrary",)),
        name="even_proj",
    )(x, *tabs, *tabs_t, nm, win, qnorm, wuq, kvnorm, wuk, wuv)


S5_STRIP = 512


def _s5_kernel(u_ref, h0r_ref, h0i_ref, lamr_ref, lami_ref, ldt_ref, wb_ref, wcr_ref, wci_ref,
               d_ref, wglu_ref, bglu_ref, y_ref, hr_ref, hi_ref, ut_ref, xr_ref, xi_ref, yt_ref,
               *, nb, tt, transpose_io):
    @pl.when(pl.program_id(0) == 0)
    def _():
        xr_ref[0:nb, :] = h0r_ref[...]
        xi_ref[0:nb, :] = h0i_ref[...]

    if transpose_io:
        for t in range(tt):
            ut_ref[t * nb:(t + 1) * nb, :] = u_ref[:, t, :]
    else:
        ut_ref[...] = u_ref[0]
    ub = ut_ref[...].astype(BF16)

    lr = jnp.minimum(lamr_ref[...], -1e-4)
    li = lami_ref[...]
    dt = jnp.exp(ldt_ref[...])
    mag = jnp.exp(lr * dt)
    ar = mag * jnp.cos(li * dt)
    ai = mag * jnp.sin(li * dt)
    den = lr * lr + li * li
    zr = ((ar - 1.0) * lr + ai * li) / den
    zi = (ai * lr - (ar - 1.0) * li) / den

    ucols = S5_STRIP // S5_STATE * S5_GROUP
    ys = []
    for s in range(S5_NSTATE // S5_STRIP):
        cols = slice(s * S5_STRIP, (s + 1) * S5_STRIP)
        us = ub[:, s * ucols:(s + 1) * ucols]
        br = jnp.dot(us, wb_ref[s, :, :S5_STRIP], preferred_element_type=F32)
        bi = jnp.dot(us, wb_ref[s, :, S5_STRIP:], preferred_element_type=F32)
        zrs, zis = zr[:, cols], zi[:, cols]
        xr_ref[nb:, cols] = zrs * br - zis * bi
        xi_ref[nb:, cols] = zrs * bi + zis * br
        ars = jnp.broadcast_to(ar[:, cols], (nb, S5_STRIP))
        ais = jnp.broadcast_to(ai[:, cols], (nb, S5_STRIP))
        hr, hi = xr_ref[0:nb, cols], xi_ref[0:nb, cols]
        for t in range(tt):
            r = slice((t + 1) * nb, (t + 2) * nb)
            hr, hi = (ars * hr - ais * hi + xr_ref[r, cols],
                      ars * hi + ais * hr + xi_ref[r, cols])
            xr_ref[r, cols] = hr
            xi_ref[r, cols] = hi
        xr_ref[0:nb, cols] = hr
        xi_ref[0:nb, cols] = hi
        ys.append(jnp.dot(xr_ref[nb:, cols].astype(BF16), wcr_ref[s], preferred_element_type=F32)
                  + jnp.dot(xi_ref[nb:, cols].astype(BF16), wci_ref[s],
                            preferred_element_type=F32))

    hr_ref[...] = xr_ref[0:nb, :]
    hi_ref[...] = xi_ref[0:nb, :]
    y = jnp.concatenate(ys, axis=1) + d_ref[...] * ut_ref[...]
    y = jax.nn.gelu(y)
    gate = jnp.dot(y.astype(BF16), wglu_ref[...], preferred_element_type=F32) + bglu_ref[...]
    y = y * jax.nn.sigmoid(gate)
    if transpose_io:
        groups = S5_WIDTH // LANE
        for c in range(groups):
            yt_ref[c] = y[:, c * LANE:(c + 1) * LANE]
        for n in range(nb):
            y_ref[n] = jnp.concatenate(
                [yt_ref[c, pl.ds(n, tt, stride=nb), :] for c in range(groups)],
                axis=1).astype(y_ref.dtype)
    else:
        y_ref[0] = y.astype(y_ref.dtype)


def _s5(u, h0r, h0i, lamr, lami, ldt, wb, wcr, wci, d, wglu, bglu, *, nb, tt, transpose_io):
    t_total = u.shape[1] if transpose_io else u.shape[0]
    if transpose_io:
        u_spec = pl.BlockSpec((nb, tt, S5_WIDTH), lambda i: (0, i, 0))
        y_shape = (nb, t_total, S5_WIDTH)
    else:
        u_spec = pl.BlockSpec((tt, nb, S5_WIDTH), lambda i: (i, 0, 0))
        y_shape = (t_total, nb, S5_WIDTH)
    consts = (h0r, h0i, lamr, lami, ldt, wb, wcr, wci, d, wglu, bglu)
    kern = functools.partial(_s5_kernel, nb=nb, tt=tt, transpose_io=transpose_io)
    return pl.pallas_call(
        kern,
        grid=(t_total // tt,),
        in_specs=[u_spec] + [_const_spec(c.shape) for c in consts],
        out_specs=[u_spec, _const_spec((nb, S5_NSTATE)), _const_spec((nb, S5_NSTATE))],
        out_shape=[jax.ShapeDtypeStruct(y_shape, BF16),
                   jax.ShapeDtypeStruct((nb, S5_NSTATE), F32),
                   jax.ShapeDtypeStruct((nb, S5_NSTATE), F32)],
        scratch_shapes=[pltpu.VMEM((nb * tt, S5_WIDTH), F32),
                        pltpu.VMEM((nb * (tt + 1), S5_NSTATE), F32),
                        pltpu.VMEM((nb * (tt + 1), S5_NSTATE), F32),
                        pltpu.VMEM((S5_WIDTH // LANE, nb * tt, LANE), F32)],
        compiler_params=_cparams(("arbitrary",)),
        name="s5_mixer",
    )(u, *consts)


MLA_LOOKAHEAD = 3


def _mla_attn_kernel(qt_ref, k_ref, vt_ref, o_ref, m_ref, l_ref, acc_ref, *, tq):
    qi = pl.program_id(1)
    m_ref[...] = jnp.full(m_ref.shape, NEG, F32)
    l_ref[...] = jnp.zeros(l_ref.shape, F32)
    acc_ref[...] = jnp.zeros(acc_ref.shape, F32)

    def causal_mask(nblk):
        key = lax.broadcasted_iota(jnp.int32, (nblk * tq, tq), 0)
        qry = lax.broadcasted_iota(jnp.int32, (nblk * tq, tq), 1)
        return key <= qry + (nblk - 1) * tq

    def process(kb0, nblk, mask):
        keys = pl.ds(pl.multiple_of(kb0 * tq, tq), nblk * tq)

        def scores(h):
            hl = slice(h * HEAD_PAD, (h + 1) * HEAD_PAD)
            return jnp.dot(k_ref[0, keys, hl], qt_ref[0, 0, hl, :],
                           preferred_element_type=F32)

        pending = [scores(h) for h in range(MLA_LOOKAHEAD)]
        for h in range(MLA_HEADS):
            st = pending.pop(0) * (MLA_SCALE * LOG2E)
            if h + MLA_LOOKAHEAD < MLA_HEADS:
                pending.append(scores(h + MLA_LOOKAHEAD))
            if mask is not None:
                st = jnp.where(mask, st, NEG)
            j = h // 2
            vts = [vt_ref[0, kb0 + i, j * LANE:(j + 1) * LANE, :] for i in range(nblk)]
            vt = vts[0] if nblk == 1 else jnp.concatenate(vts, axis=1)
            m_old = m_ref[h]
            m_new = jnp.maximum(m_old, jnp.max(st, axis=0, keepdims=True))
            alpha = jnp.exp2(m_old - m_new)
            pt = jnp.exp2(st - m_new)
            l_ref[h] = alpha * l_ref[h] + jnp.sum(pt, axis=0, keepdims=True)
            m_ref[h] = m_new
            acc_ref[h] = alpha * acc_ref[h] + jnp.dot(vt, pt.astype(BF16),
                                                      preferred_element_type=F32)

    n_wide = qi // 2

    def wide_step(i, carry):
        process(i * 2, 2, None)
        return carry

    lax.fori_loop(0, n_wide, wide_step, 0)

    @pl.when(qi % 2 == 0)
    def _():
        process(qi, 1, causal_mask(1))

    @pl.when(qi % 2 == 1)
    def _():
        process(qi - 1, 2, causal_mask(2))

    half = lax.broadcasted_iota(jnp.int32, (LANE, tq), 0) < MLA_V
    for j in range(MLA_HEADS // 2):
        ot = jnp.where(half, acc_ref[2 * j] / l_ref[2 * j], acc_ref[2 * j + 1] / l_ref[2 * j + 1])
        o_ref[0, :, j * LANE:(j + 1) * LANE] = ot.T.astype(o_ref.dtype)


def _mla_attn(qt, k, vt):
    n, nblocks, qw, tq = qt.shape
    t = k.shape[1]
    kern = functools.partial(_mla_attn_kernel, tq=tq)
    return pl.pallas_call(
        kern,
        grid=(n, nblocks),
        in_specs=[pl.BlockSpec((1, 1, qw, tq), lambda b, i: (b, i, 0, 0)),
                  pl.BlockSpec((1, t, k.shape[2]), lambda b, i: (b, 0, 0)),
                  pl.BlockSpec((1, nblocks, vt.shape[2], tq), lambda b, i: (b, 0, 0, 0))],
        out_specs=pl.BlockSpec((1, tq, MLA_HEADS * MLA_V), lambda b, i: (b, i, 0)),
        out_shape=jax.ShapeDtypeStruct((n, t, MLA_HEADS * MLA_V), BF16),
        scratch_shapes=[pltpu.VMEM((MLA_HEADS, 1, tq), F32),
                        pltpu.VMEM((MLA_HEADS, 1, tq), F32),
                        pltpu.VMEM((MLA_HEADS, LANE, tq), F32)],
        compiler_params=_cparams(("arbitrary", "arbitrary")),
        name="mla_prompt_attn",
    )(qt, k, vt)


DEC_PAGES = 32
DEC_SUB = 4
DEC_AHEAD = 2
DEC_SLOTS = DEC_AHEAD + 1
DEC_KEYS = DEC_PAGES * PAGE


def _qlat_kernel(q_ref, wk_ref, o_ref):
    for h in range(MLA_HEADS):
        o_ref[:, h * MLA_KV_LORA:(h + 1) * MLA_KV_LORA] = jnp.dot(
            q_ref[:, h * HEAD_PAD:(h + 1) * HEAD_PAD], wk_ref[h], preferred_element_type=F32)


def _qlat(q_big, wk_t):
    rows = q_big.shape[0]
    return pl.pallas_call(
        _qlat_kernel,
        grid=(1,),
        in_specs=[_const_spec(q_big.shape), _const_spec(wk_t.shape)],
        out_specs=_const_spec((rows, MLA_HEADS * MLA_KV_LORA)),
        out_shape=jax.ShapeDtypeStruct((rows, MLA_HEADS * MLA_KV_LORA), F32),
        compiler_params=_cparams(("arbitrary",)),
        name="mla_q_absorb",
    )(q_big, wk_t)


def _olat_kernel(o_ref, wv_ref, out_ref):
    for j in range(MLA_HEADS // 2):
        acc = None
        for h in (2 * j, 2 * j + 1):
            part = jnp.dot(o_ref[:, h * MLA_KV_LORA:(h + 1) * MLA_KV_LORA].astype(BF16),
                           wv_ref[:, h * LANE:(h + 1) * LANE], preferred_element_type=F32)
            acc = part if acc is None else acc + part
        out_ref[:, j * LANE:(j + 1) * LANE] = acc.astype(out_ref.dtype)


def _olat(o_lat, wv_big):
    rows = o_lat.shape[0]
    return pl.pallas_call(
        _olat_kernel,
        grid=(1,),
        in_specs=[_const_spec(o_lat.shape), _const_spec(wv_big.shape)],
        out_specs=_const_spec((rows, MLA_HEADS * MLA_V)),
        out_shape=jax.ShapeDtypeStruct((rows, MLA_HEADS * MLA_V), BF16),
        compiler_params=_cparams(("arbitrary",)),
        name="mla_v_expand",
    )(o_lat, wv_big)


def _page_copies(cache_ckv, cache_kr, ckv_buf, kr_buf, sem, slot, page, p):
    rows = pl.ds(p * PAGE, PAGE)
    return (pltpu.make_async_copy(cache_ckv.at[0, page], ckv_buf.at[slot, rows], sem.at[0, slot]),
            pltpu.make_async_copy(cache_kr.at[0, page], kr_buf.at[slot, p], sem.at[1, slot]))


def _mla_decode_kernel(ptc_ref, ptn_ref, ql_ref, qr_ref, ckvn_ref, krn_ref, cache_ckv, cache_kr,
                       o_ref, ckv_buf, kr_buf, sem, *, n_seq, n_chunks):
    b = pl.program_id(0)
    g0 = b * n_chunks

    def start_page(page, slot, p):
        for cp in _page_copies(cache_ckv, cache_kr, ckv_buf, kr_buf, sem, slot, page, p):
            cp.start()

    def wait_chunk(slot):
        for p in range(DEC_PAGES):
            for cp in _page_copies(cache_ckv, cache_kr, ckv_buf, kr_buf, sem, slot, 0, p):
                cp.wait()

    @pl.when(b == 0)
    def _():
        for a in range(DEC_AHEAD):
            for p in range(DEC_PAGES):
                start_page(ptc_ref[0, 0, a * DEC_PAGES + p], a, p)

    ql = ql_ref[0].astype(BF16)
    qr = qr_ref[0].astype(BF16)

    def chunk_body(c, carry):
        m, l, acc = carry
        slot = lax.rem(g0 + c, DEC_SLOTS)
        ahead_slot = lax.rem(g0 + c + DEC_AHEAD, DEC_SLOTS)
        wait_chunk(slot)
        wraps = c + DEC_AHEAD >= n_chunks
        ahead_base = jnp.where(wraps, c + DEC_AHEAD - n_chunks, c + DEC_AHEAD) * DEC_PAGES

        def ahead_page(p):
            return jnp.where(wraps, ptn_ref[0, 0, ahead_base + p], ptc_ref[0, 0, ahead_base + p])

        kbs, scores = [], []
        for i in range(DEC_PAGES // DEC_SUB):
            kb = ckv_buf[slot, i * DEC_SUB * PAGE:(i + 1) * DEC_SUB * PAGE, :].astype(BF16)
            krt = jnp.concatenate([kr_buf[slot, i * DEC_SUB + p] for p in range(DEC_SUB)],
                                  axis=1).astype(BF16)
            kbs.append(kb)
            scores.append(
                (lax.dot_general(ql, kb, (((1,), (1,)), ((), ())), preferred_element_type=F32)
                 + jnp.dot(qr, krt, preferred_element_type=F32)) * MLA_SCALE)
            for p in range(i * DEC_SUB, (i + 1) * DEC_SUB):
                start_page(ahead_page(p), ahead_slot, p)
        for kb, s in zip(kbs, scores):
            m_new = jnp.maximum(m, jnp.max(s, axis=1, keepdims=True))
            alpha = jnp.exp(m - m_new)
            pr = jnp.exp(s - m_new)
            l = alpha * l + jnp.sum(pr, axis=1, keepdims=True)
            acc = alpha * acc + jnp.dot(pr.astype(BF16), kb, preferred_element_type=F32)
            m = m_new
        return m, l, acc

    init = (jnp.full((MLA_HEADS, 1), NEG, F32), jnp.zeros((MLA_HEADS, 1), F32),
            jnp.zeros((MLA_HEADS, MLA_KV_LORA), F32))
    m, l, acc = lax.fori_loop(0, n_chunks, chunk_body, init)

    @pl.when(b == n_seq - 1)
    def _():
        for a in range(DEC_AHEAD):
            wait_chunk((n_seq * n_chunks + a) % DEC_SLOTS)

    cn = ckvn_ref[0].astype(BF16).astype(F32)
    kn = krn_ref[0].astype(BF16).astype(F32)
    s_new = (jnp.sum(ql.astype(F32) * cn, axis=1, keepdims=True)
             + jnp.sum(qr.astype(F32) * kn, axis=1, keepdims=True)) * MLA_SCALE
    m_new = jnp.maximum(m, s_new)
    alpha = jnp.exp(m - m_new)
    p_new = jnp.exp(s_new - m_new)
    l = alpha * l + p_new
    acc = alpha * acc + p_new.astype(BF16).astype(F32) * cn
    o_ref[0] = acc / l


def _mla_decode(page_table, q_lat, q_rope, ckv_new, kr_new, cache_ckv, cache_kr):
    n_seq, n_pages = page_table.shape
    n_chunks = n_pages // DEC_PAGES
    assert n_pages % DEC_PAGES == 0 and n_chunks >= DEC_AHEAD
    pt = page_table.reshape(n_seq, 1, n_pages)
    smem_row = lambda f: pl.BlockSpec((1, 1, n_pages), f, memory_space=pltpu.SMEM)
    per_seq = lambda a: pl.BlockSpec((1,) + a.shape[1:], lambda b: (b, 0, 0))
    kern = functools.partial(_mla_decode_kernel, n_seq=n_seq, n_chunks=n_chunks)
    return pl.pallas_call(
        kern,
        grid=(n_seq,),
        in_specs=[smem_row(lambda b: (b, 0, 0)),
                  smem_row(lambda b: (jnp.minimum(b + 1, n_seq - 1), 0, 0)),
                  per_seq(q_lat), per_seq(q_rope), per_seq(ckv_new), per_seq(kr_new),
                  pl.BlockSpec(memory_space=pl.ANY), pl.BlockSpec(memory_space=pl.ANY)],
        out_specs=pl.BlockSpec((1, MLA_HEADS, MLA_KV_LORA), lambda b: (b, 0, 0)),
        out_shape=jax.ShapeDtypeStruct((n_seq, MLA_HEADS, MLA_KV_LORA), F32),
        scratch_shapes=[pltpu.VMEM((DEC_SLOTS, DEC_KEYS, MLA_KV_LORA), F32),
                        pltpu.VMEM((DEC_SLOTS, DEC_PAGES, MLA_ROPE, PAGE), F32),
                        pltpu.SemaphoreType.DMA((2, DEC_SLOTS))],
        compiler_params=_cparams(("arbitrary",)),
        name="mla_paged_decode",
    )(pt, pt, q_lat, q_rope, ckv_new, kr_new, cache_ckv, cache_kr)


def _odd_proj_kernel(x_ref, c_ref, sa_ref, sb_ref, nm_ref, win_ref, q_ref, k_ref, v_ref):
    h = _rms(x_ref[...], nm_ref[...]).astype(BF16)
    z = jnp.dot(h, win_ref[...], preferred_element_type=F32)
    qk = _rope_lanes(z[:, :QK_ODD], c_ref[...], sa_ref[...], sb_ref[...], ROT_DIM // 2)
    q_ref[...] = qk[:, :SWA_HEADS * SWA_HD].astype(BF16)
    k_ref[...] = qk[:, SWA_HEADS * SWA_HD:]
    v_ref[...] = z[:, QK_ODD:]


def _odd_proj(x, tabs, nm, win, *, tm):
    rows = x.shape[0]
    nt = tabs[0].shape[0] // tm
    row = lambda w: pl.BlockSpec((tm, w), lambda i: (i, 0))
    tab = pl.BlockSpec((tm, LANE), lambda i: (i % nt, 0))
    kvw = SWA_KV_HEADS * SWA_HD
    return pl.pallas_call(
        _odd_proj_kernel,
        grid=(rows // tm,),
        in_specs=[row(D_MODEL), tab, tab, tab, _const_spec(nm.shape), _const_spec(win.shape)],
        out_specs=[row(SWA_HEADS * SWA_HD), row(kvw), row(kvw)],
        out_shape=[jax.ShapeDtypeStruct((rows, SWA_HEADS * SWA_HD), BF16),
                   jax.ShapeDtypeStruct((rows, kvw), F32),
                   jax.ShapeDtypeStruct((rows, kvw), F32)],
        compiler_params=_cparams(("arbitrary",)),
        name="odd_proj",
    )(x, *tabs, nm, win)


def _odd_proj_t_kernel(x_ref, ct_ref, sat_ref, sbt_ref, nm_ref, wt_ref, qt_ref, k_ref, vt_ref,
                       ktail_ref, vtail_ref, *, tm):
    h = _rms(x_ref[...], nm_ref[...]).astype(BF16)
    zt = lax.dot_general(wt_ref[...], h, (((1,), (1,)), ((), ())),
                         preferred_element_type=F32)
    qkt = _rope_sublanes(zt[:QK_ODD], ct_ref[...], sat_ref[...], sbt_ref[...], ROT_DIM // 2)
    nq = SWA_HEADS * SWA_HD
    for i in range(tm // WINDOW):
        cols = slice(i * WINDOW, (i + 1) * WINDOW)
        qt_ref[0, i] = qkt[:nq, cols].astype(BF16)
        vt_ref[0, i] = zt[QK_ODD:, cols].astype(BF16)
    k = qkt[nq:].T
    k_ref[...] = k.astype(BF16)
    ktail_ref[0] = k[tm - WINDOW:]
    vtail_ref[0] = zt[QK_ODD:, tm - WINDOW:].T


def _odd_proj_t(x, tabs, nm, win_t, *, tm, n_seq):
    rows = x.shape[0]
    t = rows // n_seq
    nt = t // tm
    nb = tm // WINDOW
    kvw = SWA_KV_HEADS * SWA_HD
    qw = SWA_HEADS * SWA_HD
    tabs_t = tuple(a.T for a in tabs)
    row = lambda w: pl.BlockSpec((tm, w), lambda i: (i, 0))
    tab_t = pl.BlockSpec((LANE, tm), lambda i: (0, i % nt))
    tblock = lambda w: pl.BlockSpec((1, nb, w, WINDOW), lambda i: (i // nt, i % nt, 0, 0))
    tail = pl.BlockSpec((1, WINDOW, kvw), lambda i: (i // nt, 0, 0))
    return pl.pallas_call(
        functools.partial(_odd_proj_t_kernel, tm=tm),
        grid=(rows // tm,),
        in_specs=[row(D_MODEL), tab_t, tab_t, tab_t, _const_spec(nm.shape),
                  _const_spec(win_t.shape)],
        out_specs=[tblock(qw), row(kvw), tblock(kvw), tail, tail],
        out_shape=[jax.ShapeDtypeStruct((n_seq, t // WINDOW, qw, WINDOW), BF16),
                   jax.ShapeDtypeStruct((rows, kvw), BF16),
                   jax.ShapeDtypeStruct((n_seq, t // WINDOW, kvw, WINDOW), BF16),
                   jax.ShapeDtypeStruct((n_seq, WINDOW, kvw), F32),
                   jax.ShapeDtypeStruct((n_seq, WINDOW, kvw), F32)],
        compiler_params=_cparams(("arbitrary",)),
        name="odd_proj_t",
    )(x, *tabs_t, nm, win_t)


SWA_QBLOCKS = 2


def _swa_attn_kernel(sink_ref, qt_ref, kp_ref, kc_ref, vtp_ref, vtc_ref, o_ref):
    step = pl.program_id(1)
    cols = SWA_GQ * WINDOW
    key = lax.broadcasted_iota(jnp.int32, (2 * WINDOW, cols), 0)
    qry = jnp.bitwise_and(lax.broadcasted_iota(jnp.int32, (2 * WINDOW, cols), 1), WINDOW - 1)
    rel = qry + WINDOW - key
    band = (rel >= 0) & (rel <= WINDOW)
    lane = lax.broadcasted_iota(jnp.int32, (1, cols), 1)
    sinks = []
    for kh in range(SWA_KV_HEADS):
        row = jnp.full((1, cols), sink_ref[kh * SWA_GQ + SWA_GQ - 1], F32)
        for g in range(SWA_GQ - 2, -1, -1):
            row = jnp.where(lane < (g + 1) * WINDOW, sink_ref[kh * SWA_GQ + g], row)
        sinks.append(row * LOG2E)
    for sub in range(SWA_QBLOCKS):
        if sub == 0:
            k_prev, vt_prev = kp_ref[0], vtp_ref[0, 0]
            valid = band & ((key >= WINDOW) | (step > 0))
        else:
            k_prev, vt_prev = kc_ref[0, (sub - 1) * WINDOW:sub * WINDOW], vtc_ref[0, sub - 1]
            valid = band
        kk = jnp.concatenate([k_prev, kc_ref[0, sub * WINDOW:(sub + 1) * WINDOW]], axis=0)
        vvt = jnp.concatenate([vt_prev, vtc_ref[0, sub]], axis=1)
        scores = []
        for kh in range(SWA_KV_HEADS):
            qg = jnp.concatenate(
                [qt_ref[0, sub, (kh * SWA_GQ + g) * SWA_HD:(kh * SWA_GQ + g + 1) * SWA_HD, :]
                 for g in range(SWA_GQ)], axis=1)
            scores.append(jnp.dot(kk[:, kh * SWA_HD:(kh + 1) * SWA_HD], qg,
                                  preferred_element_type=F32))
        outs = []
        for kh in range(SWA_KV_HEADS):
            st = jnp.where(valid, scores[kh] * (SWA_SCALE * LOG2E), NEG)
            m = jnp.maximum(jnp.max(st, axis=0, keepdims=True), sinks[kh])
            p = jnp.exp2(st - m)
            l = jnp.sum(p, axis=0, keepdims=True) + jnp.exp2(sinks[kh] - m)
            ot = jnp.dot(vvt[kh * SWA_HD:(kh + 1) * SWA_HD, :], p.astype(BF16),
                         preferred_element_type=F32) / l
            outs += [ot[:, g * WINDOW:(g + 1) * WINDOW] for g in range(SWA_GQ)]
        o_ref[0, sub * WINDOW:(sub + 1) * WINDOW, :] = (
            jnp.concatenate(outs, axis=0).T.astype(o_ref.dtype))


def _swa_attn(sinks, qt, k, vt):
    n, nblocks, qw, _ = qt.shape
    t = k.shape[1]
    kvw = SWA_KV_HEADS * SWA_HD
    qb = SWA_QBLOCKS
    before = lambda i: jnp.maximum(i * qb - 1, 0)
    return pl.pallas_call(
        _swa_attn_kernel,
        grid=(n, nblocks // qb),
        in_specs=[pl.BlockSpec(memory_space=pltpu.SMEM),
                  pl.BlockSpec((1, qb, qw, WINDOW), lambda a, i: (a, i, 0, 0)),
                  pl.BlockSpec((1, WINDOW, kvw), lambda a, i: (a, before(i), 0)),
                  pl.BlockSpec((1, qb * WINDOW, kvw), lambda a, i: (a, i, 0)),
                  pl.BlockSpec((1, 1, kvw, WINDOW), lambda a, i: (a, before(i), 0, 0)),
                  pl.BlockSpec((1, qb, kvw, WINDOW), lambda a, i: (a, i, 0, 0))],
        out_specs=pl.BlockSpec((1, qb * WINDOW, qw), lambda a, i: (a, i, 0)),
        out_shape=jax.ShapeDtypeStruct((n, t, qw), BF16),
        compiler_params=_cparams(("arbitrary", "arbitrary")),
        name="swa_prompt_attn",
    )(sinks, qt, k, k, vt, vt)


SWA_DEC_SEQS = 8


def _swa_decode_kernel(sink_ref, q_ref, kn_ref, vn_ref, ck_ref, cv_ref, o_ref):
    q = q_ref[...]
    hgrp = lax.broadcasted_iota(jnp.int32, (1, SWA_HEADS, 1), 1) // SWA_GQ
    hidx = lax.broadcasted_iota(jnp.int32, (1, SWA_HEADS, 1), 1)
    sink = jnp.zeros((1, SWA_HEADS, 1), F32)
    for h in range(SWA_HEADS):
        sink = jnp.where(hidx == h, sink_ref[h], sink)
    s = jnp.zeros((q.shape[0], SWA_HEADS, WINDOW), F32)
    for kh in range(SWA_KV_HEADS):
        skh = jnp.einsum('nhd,ndj->nhj', q, ck_ref[:, kh].astype(BF16),
                         preferred_element_type=F32)
        s = jnp.where(hgrp == kh, skh, s)
    s = s * SWA_SCALE
    qf = q.astype(F32)
    kn = kn_ref[...].astype(BF16).astype(F32)
    vn = vn_ref[...].astype(BF16).astype(F32)
    s_new = jnp.sum(qf * kn, axis=2, keepdims=True) * SWA_SCALE
    m = jnp.maximum(jnp.maximum(jnp.max(s, axis=2, keepdims=True), s_new), sink)
    p = jnp.exp(s - m)
    p_new = jnp.exp(s_new - m)
    l = jnp.sum(p, axis=2, keepdims=True) + p_new + jnp.exp(sink - m)
    o = p_new.astype(BF16).astype(F32) * vn
    for kh in range(SWA_KV_HEADS):
        pk = jnp.where(hgrp == kh, p, 0.0).astype(BF16)
        o = o + jnp.einsum('nhj,ndj->nhd', pk, cv_ref[:, kh].astype(BF16),
                           preferred_element_type=F32)
    o_ref[...] = (o / l).astype(o_ref.dtype)


def _swa_decode(sinks, q, k_new, v_new, cache_k, cache_v):
    n = q.shape[0]
    nb = SWA_DEC_SEQS
    seq = pl.BlockSpec((nb, SWA_HEADS, SWA_HD), lambda i: (i, 0, 0))
    cache = pl.BlockSpec((nb, SWA_KV_HEADS, SWA_HD, WINDOW), lambda i: (i, 0, 0, 0))
    return pl.pallas_call(
        _swa_decode_kernel,
        grid=(n // nb,),
        in_specs=[pl.BlockSpec(memory_space=pltpu.SMEM), seq, seq, seq, cache, cache],
        out_specs=seq,
        out_shape=jax.ShapeDtypeStruct((n, SWA_HEADS, SWA_HD), BF16),
        compiler_params=_cparams(("arbitrary",)),
        name="swa_decode_attn",
    )(sinks, q, k_new, v_new, cache_k, cache_v)


def _post_kernel(*refs, n_mix, decode, final, tm):
    it = iter(refs)
    x_ref = next(it)
    mix_refs = [next(it) for _ in range(n_mix)]
    wo_refs = [next(it) for _ in range(n_mix)]
    nf_ref, wg_ref, wu_ref, cw_ref, cb_ref, wd_ref = (next(it) for _ in range(6))
    if decode:
        buf0_ref, buf1_ref = next(it), next(it)
    fn_ref = next(it) if final else None
    y_ref = next(it)
    g_ref = next(it)
    act_ref = next(it)
    carry_ref = None if decode else next(it)

    x1 = x_ref[...]
    for a_ref, w_ref in zip(mix_refs, wo_refs):
        x1 = x1 + jnp.dot(a_ref[...], w_ref[...], preferred_element_type=F32)
    h2 = _rms(x1, nf_ref[...]).astype(BF16)

    if not decode:
        @pl.when(pl.program_id(1) == 0)
        def _():
            carry_ref[...] = jnp.zeros_like(carry_ref)
        row = lax.broadcasted_iota(jnp.int32, (tm, FF_CHUNK), 0)

    for c in range(D_FF // FF_CHUNK):
        sl = slice(c * FF_CHUNK, (c + 1) * FF_CHUNK)
        g = jnp.dot(h2, wg_ref[:, sl], preferred_element_type=F32)
        u = jnp.dot(h2, wu_ref[:, sl], preferred_element_type=F32)
        if decode:
            gm2, gm1 = buf0_ref[:, sl], buf1_ref[:, sl]
            g_ref[:, sl] = g
        else:
            prev = carry_ref[:, sl]
            p6, p7 = prev[SUBLANE - 2:SUBLANE - 1, :], prev[SUBLANE - 1:SUBLANE, :]
            gm1 = jnp.where(row == 0, p7, pltpu.roll(g, 1, 0))
            gm2 = jnp.where(row == 0, p6, jnp.where(row == 1, p7, pltpu.roll(g, 2, 0)))
            last = g[tm - SUBLANE:tm, :]
            carry_ref[:, sl] = last
            g_ref[0, :, sl] = last
        cc = cb_ref[:, sl] + cw_ref[0:1, sl] * gm2 + cw_ref[1:2, sl] * gm1 + cw_ref[2:3, sl] * g
        act_ref[:, sl] = (jax.nn.gelu(cc) * u).astype(BF16)

    x2 = x1 + jnp.dot(act_ref[...], wd_ref[...], preferred_element_type=F32)
    y_ref[...] = _rms(x2, fn_ref[...]) if final else x2


def _post(x, mixes, wos, nf, wg, wu, cw, cb, wd, *, n_seq, tm, conv_bufs=None, final_norm=None):
    rows = x.shape[0]
    decode = conv_bufs is not None
    nt = rows // n_seq // tm if not decode else rows // tm
    grid = (1, nt) if decode else (n_seq, nt)
    row = lambda w: pl.BlockSpec((tm, w), lambda a, i: (a * nt + i, 0))
    args = [x] + list(mixes) + list(wos) + [nf, wg, wu, cw, cb, wd]
    in_specs = ([row(D_MODEL)] + [row(m.shape[1]) for m in mixes]
                + [_single_spec(w.shape) for w in wos]
                + [_const_spec(nf.shape), _single_spec(wg.shape), _single_spec(wu.shape),
                   _const_spec(cw.shape), _const_spec(cb.shape), _single_spec(wd.shape)])
    if decode:
        args += list(conv_bufs)
        in_specs += [row(D_FF), row(D_FF)]
    if final_norm is not None:
        args.append(final_norm)
        in_specs.append(_const_spec(final_norm.shape))
    if decode:
        g_shape = jax.ShapeDtypeStruct((rows, D_FF), F32)
        g_spec = row(D_FF)
    else:
        g_shape = jax.ShapeDtypeStruct((n_seq, SUBLANE, D_FF), F32)
        g_spec = pl.BlockSpec((1, SUBLANE, D_FF), lambda a, i: (a, 0, 0))
    scratch = [pltpu.VMEM((tm, D_FF), BF16)]
    if not decode:
        scratch.append(pltpu.VMEM((SUBLANE, D_FF), F32))
    kern = functools.partial(_post_kernel, n_mix=len(mixes), decode=decode,
                             final=final_norm is not None, tm=tm)
    return pl.pallas_call(
        kern,
        grid=grid,
        in_specs=in_specs,
        out_specs=[row(D_MODEL), g_spec],
        out_shape=[jax.ShapeDtypeStruct((rows, D_MODEL), F32), g_shape],
        scratch_shapes=scratch,
        compiler_params=_cparams(("arbitrary", "arbitrary")),
        name="post_decode" if decode else "post_prompt",
    )(*args)


def _prep_even(e_w_in, e_w_uq, e_w_uk, e_w_uv, e_s5_b_re, e_s5_b_im, e_s5_c_re, e_s5_c_im):
    o1 = S5_WIDTH
    o2 = o1 + MLA_Q_LORA
    o3 = o2 + MLA_KV_LORA
    zpad = lambda n: jnp.zeros((D_MODEL, n), F32)
    win = jnp.concatenate([e_w_in[:, :o3], zpad(MLA_NOPE), e_w_in[:, o3:],
                           zpad(HEAD_PAD - MLA_NOPE - MLA_ROPE)], axis=1).astype(BF16)
    pad_last = lambda w, n: jnp.pad(w, ((0, 0), (0, 0), (0, n - w.shape[2])))
    wuq = pad_last(e_w_uq, HEAD_PAD).reshape(MLA_Q_LORA, MLA_HEADS * HEAD_PAD).astype(BF16)
    wuk = pad_last(e_w_uk, HEAD_PAD).reshape(MLA_KV_LORA, MLA_HEADS * HEAD_PAD).astype(BF16)
    wuv = e_w_uv.reshape(MLA_KV_LORA, MLA_HEADS * MLA_V).astype(BF16)
    wk_t = jnp.pad(jnp.transpose(e_w_uk, (1, 2, 0)),
                   ((0, 0), (0, HEAD_PAD - MLA_NOPE), (0, 0))).astype(BF16)
    wv4 = e_w_uv.reshape(MLA_KV_LORA, MLA_HEADS // 2, 2, MLA_V)
    zv = jnp.zeros_like(wv4[:, :, 0])
    wv_big = jnp.stack([jnp.concatenate([wv4[:, :, 0], zv], axis=-1),
                        jnp.concatenate([zv, wv4[:, :, 1]], axis=-1)], axis=2)
    wv_big = wv_big.reshape(MLA_KV_LORA, MLA_HEADS * LANE).astype(BF16)
    gs = S5_STRIP // S5_STATE
    ns = S5_GROUPS // gs
    eye = jnp.eye(gs, dtype=F32)
    bd_in = lambda b: jnp.einsum('sgph,gk->sghkp', b.reshape(ns, gs, S5_STATE, S5_GROUP),
                                 eye).reshape(ns, gs * S5_GROUP, S5_STRIP)
    bd_out = lambda c: jnp.einsum('sghp,gk->sgpkh', c.reshape(ns, gs, S5_GROUP, S5_STATE),
                                  eye).reshape(ns, S5_STRIP, gs * S5_GROUP)
    wb = jnp.concatenate([bd_in(e_s5_b_re), bd_in(e_s5_b_im)], axis=2).astype(BF16)
    wcr = bd_out(e_s5_c_re).astype(BF16)
    wci = (-bd_out(e_s5_c_im)).astype(BF16)
    return win, wuq, wuk, wuv, wk_t, wv_big, wb, wcr, wci


def kernel(x_prompt, x_sample, page_table, state_s5_re, state_s5_im, cache_mla_ckv, cache_mla_krope,
           cache_swa_k, cache_swa_v, state_ffn_conv, norm_mix, norm_ffn, final_norm, e_w_in,
           e_s5_lam_re, e_s5_lam_im, e_s5_log_dt, e_s5_b_re, e_s5_b_im, e_s5_c_re, e_s5_c_im, e_s5_d,
           e_s5_w_glu, e_s5_b_glu, e_q_norm, e_w_uq, e_kv_norm, e_w_uk, e_w_uv, e_w_out, o_w_in,
           o_sinks, o_w_out, f_w_gate, f_w_up, f_conv_w, f_conv_b, f_w_down):
    n_p, t_p, _ = x_prompt.shape
    n_s, t_s, _ = x_sample.shape
    assert t_s == 1
    past_len = page_table.shape[1] * PAGE
    row2 = lambda v: v.reshape(1, -1)

    (win, wuq, wuk, wuv, wk_t, wv_big, wb, wcr, wci) = _prep_even(
        e_w_in[0], e_w_uq[0], e_w_uk[0], e_w_uv[0], e_s5_b_re[0], e_s5_b_im[0],
        e_s5_c_re[0], e_s5_c_im[0])
    lamr, lami = row2(e_s5_lam_re[0]), row2(e_s5_lam_im[0])
    ldt = row2(jnp.repeat(e_s5_log_dt[0], S5_STATE))
    s5_consts = (lamr, lami, ldt, wb, wcr, wci, row2(e_s5_d[0]), e_s5_w_glu[0].astype(BF16),
                 row2(e_s5_b_glu[0]))
    wout_s5 = e_w_out[0][:S5_WIDTH].astype(BF16)
    wout_mla = e_w_out[0][S5_WIDTH:].astype(BF16)
    owin = o_w_in[0].astype(BF16)
    owout = o_w_out[0].astype(BF16)
    ffn = [(row2(norm_ffn[l]), f_w_gate[l].astype(BF16), f_w_up[l].astype(BF16), f_conv_w[l],
            row2(f_conv_b[l]), f_w_down[l].astype(BF16)) for l in range(2)]
    nm0, nm1 = row2(norm_mix[0]), row2(norm_mix[1])
    qnorm, kvnorm = row2(e_q_norm[0]), row2(e_kv_norm[0])
    fnorm = row2(final_norm)
    sinks = o_sinks[0]

    xp = x_prompt.reshape(n_p * t_p, D_MODEL)
    pos_p = np.arange(t_p)
    u, qt_p, ckv_p, kr_p, k_big, vt_p = _even_proj(
        xp, _mla_rope_tables(pos_p), nm0, win, qnorm, wuq.T, kvnorm, wuk, wuv.T, tm=256, n_seq=n_p,
        with_kv=True)
    zeros_state = jnp.zeros((n_p, S5_NSTATE), F32)
    y_s5, p_hr, p_hi = _s5(u.reshape(n_p, t_p, S5_WIDTH), zeros_state, zeros_state, *s5_consts,
                           nb=n_p, tt=64, transpose_io=True)
    o_mla = _mla_attn(qt_p, k_big.reshape(n_p, t_p, -1), vt_p)
    x1, conv0 = _post(xp, [y_s5.reshape(n_p * t_p, -1), o_mla.reshape(n_p * t_p, -1)],
                      [wout_s5, wout_mla], *ffn[0], n_seq=n_p, tm=512)
    qt1, k1, vt1, k_tail, v_tail = _odd_proj_t(x1, _swa_rope_tables(pos_p), nm1, owin.T, tm=512,
                                               n_seq=n_p)
    kvw = SWA_KV_HEADS * SWA_HD
    o_swa = _swa_attn(sinks, qt1, k1.reshape(n_p, t_p, kvw), vt1)
    y_p, conv1 = _post(x1, [o_swa.reshape(n_p * t_p, -1)], [owout], *ffn[1], n_seq=n_p, tm=512,
                       final_norm=fnorm)

    y_prompt = y_p.reshape(n_p, t_p, D_MODEL)
    p_s5_re = p_hr.reshape(1, n_p, S5_GROUPS, S5_STATE)
    p_s5_im = p_hi.reshape(1, n_p, S5_GROUPS, S5_STATE)
    p_ckv = ckv_p.reshape(1, n_p, t_p, MLA_KV_LORA)
    p_krope = kr_p[:, MLA_NOPE:MLA_NOPE + MLA_ROPE].reshape(1, n_p, t_p, MLA_ROPE)
    assert t_p >= WINDOW
    p_swa_k = k_tail.reshape(1, n_p, WINDOW, SWA_KV_HEADS, SWA_HD)
    p_swa_v = v_tail.reshape(1, n_p, WINDOW, SWA_KV_HEADS, SWA_HD)
    p_conv = jnp.stack([conv0[:, SUBLANE - 2:], conv1[:, SUBLANE - 2:]])

    xs = x_sample.reshape(n_s, D_MODEL)
    pos_s = np.full((n_s,), past_len)
    u_s, qs_big, ckv_s, kr_s = _even_proj(
        xs, _mla_rope_tables(pos_s), nm0, win, qnorm, wuq, kvnorm, wuk, wuv, tm=n_s, n_seq=n_s,
        with_kv=False)
    ys_s5, s_hr, s_hi = _s5(u_s.reshape(1, n_s, S5_WIDTH), state_s5_re[0].reshape(n_s, S5_NSTATE),
                            state_s5_im[0].reshape(n_s, S5_NSTATE), *s5_consts,
                            nb=n_s, tt=1, transpose_io=False)
    q_lat = _qlat(qs_big, wk_t).reshape(n_s, MLA_HEADS, MLA_KV_LORA)
    q_rope = qs_big.reshape(n_s, MLA_HEADS, HEAD_PAD)[:, :, MLA_NOPE:MLA_NOPE + MLA_ROPE]
    kr_s32 = kr_s[:, MLA_NOPE:MLA_NOPE + MLA_ROPE]
    o_lat = _mla_decode(page_table, q_lat, q_rope, ckv_s.reshape(n_s, 1, MLA_KV_LORA),
                        kr_s32.reshape(n_s, 1, MLA_ROPE), cache_mla_ckv,
                        jnp.swapaxes(cache_mla_krope, 2, 3))
    os_mla = _olat(o_lat.reshape(n_s, MLA_HEADS * MLA_KV_LORA), wv_big)
    xs1, g0 = _post(xs, [ys_s5.reshape(n_s, -1), os_mla], [wout_s5, wout_mla], *ffn[0],
                    n_seq=n_s, tm=n_s, conv_bufs=(state_ffn_conv[0, :, 0], state_ffn_conv[0, :, 1]))
    qs1, ks1, vs1 = _odd_proj(xs1, _swa_rope_tables(pos_s), nm1, owin, tm=n_s)
    expand = lambda a: jnp.repeat(a.reshape(n_s, SWA_KV_HEADS, SWA_HD), SWA_GQ, axis=1)
    os_swa = _swa_decode(sinks, qs1.reshape(n_s, SWA_HEADS, SWA_HD), expand(ks1), expand(vs1),
                         jnp.transpose(cache_swa_k[0], (0, 2, 3, 1)),
                         jnp.transpose(cache_swa_v[0], (0, 2, 3, 1)))
    ys, g1 = _post(xs1, [os_swa.reshape(n_s, -1)], [owout], *ffn[1], n_seq=n_s, tm=n_s,
                   conv_bufs=(state_ffn_conv[1, :, 0], state_ffn_conv[1, :, 1]), final_norm=fnorm)

    y_sample = ys.reshape(n_s, 1, D_MODEL)
    s_s5_re = s_hr.reshape(1, n_s, S5_GROUPS, S5_STATE)
    s_s5_im = s_hi.reshape(1, n_s, S5_GROUPS, S5_STATE)
    s_ckv = ckv_s.reshape(1, n_s, 1, MLA_KV_LORA)
    s_krope = kr_s32.reshape(1, n_s, 1, MLA_ROPE)
    s_swa_k = ks1.reshape(1, n_s, 1, SWA_KV_HEADS, SWA_HD)
    s_swa_v = vs1.reshape(1, n_s, 1, SWA_KV_HEADS, SWA_HD)
    s_conv = jnp.stack([jnp.stack([state_ffn_conv[0, :, 1], g0], axis=1),
                        jnp.stack([state_ffn_conv[1, :, 1], g1], axis=1)])

    return (y_prompt, y_sample, p_s5_re, p_s5_im, p_ckv, p_krope, p_swa_k, p_swa_v, p_conv,
            s_s5_re, s_s5_im, s_ckv, s_krope, s_swa_k, s_swa_v, s_conv)
```

```python
import functools
import math

import numpy as np
import jax
import jax.numpy as jnp
from jax import lax
from jax.experimental import pallas as pl
from jax.experimental.pallas import tpu as pltpu

F32 = jnp.float32
BF16 = jnp.bfloat16

D_MODEL = 1024
S5_WIDTH = 512
S5_GROUPS = 32
S5_GROUP = 16
S5_STATE = 64
S5_NSTATE = S5_GROUPS * S5_STATE
MLA_HEADS = 16
MLA_NOPE = 64
MLA_ROPE = 32
MLA_V = 64
MLA_Q_LORA = 384
MLA_KV_LORA = 256
MLA_THETA = 10000.0
MLA_SCALE = 1.0 / math.sqrt(MLA_NOPE + MLA_ROPE)
SWA_HEADS = 16
SWA_KV_HEADS = 4
SWA_HD = 64
SWA_GQ = SWA_HEADS // SWA_KV_HEADS
WINDOW = 128
ROT_DIM = SWA_HD // 4
ROPE_THETA = 500000.0
SWA_SCALE = 1.0 / math.sqrt(SWA_HD)
D_FF = 2816
PAGE = 128
EPS = 1e-6
NEG = -1e30
LOG2E = math.log2(math.e)

LANE = 128
SUBLANE = 8
HEAD_PAD = LANE
Z_EVEN = S5_WIDTH + MLA_Q_LORA + MLA_KV_LORA + LANE
QK_ODD = (SWA_HEADS + SWA_KV_HEADS) * SWA_HD
FF_CHUNK = 256
VMEM_LIMIT = 48 * 1024 * 1024

TM_EVEN_PROJ = 512
TM_ODD_PROJ = 512
TM_POST = 512
MLA_TQ = 256
S5_TT = 128


def _cparams(sem):
    return pltpu.CompilerParams(dimension_semantics=sem, vmem_limit_bytes=VMEM_LIMIT)


def _rms(x, g):
    return x * lax.rsqrt(jnp.mean(x * x, axis=-1, keepdims=True) + EPS) * g


def _const_spec(shape):
    nd = len(shape)
    return pl.BlockSpec(shape, lambda *_: (0,) * nd)


def _single_spec(shape):
    nd = len(shape)
    return pl.BlockSpec(shape, lambda *_: (0,) * nd, pipeline_mode=pl.Buffered(1))


def _mla_rope_tables(pos):
    half = MLA_ROPE // 2
    inv = MLA_THETA ** (-np.arange(half, dtype=np.float64) * 2.0 / MLA_ROPE)
    ang = np.asarray(pos, np.float64)[:, None] * inv[None, :]
    cos, sin = np.cos(ang), np.sin(ang)
    p = ang.shape[0]
    c = np.zeros((p, LANE)); sa = np.zeros((p, LANE)); sb = np.zeros((p, LANE))
    c[:, :MLA_NOPE] = 1.0
    c[:, MLA_NOPE:MLA_NOPE + half] = cos
    c[:, MLA_NOPE + half:MLA_NOPE + 2 * half] = cos
    sa[:, MLA_NOPE + half:MLA_NOPE + 2 * half] = sin
    sb[:, MLA_NOPE:MLA_NOPE + half] = -sin
    return tuple(jnp.asarray(t, F32) for t in (c, sa, sb))


def _swa_rope_tables(pos):
    half = ROT_DIM // 2
    inv = ROPE_THETA ** (-np.arange(half, dtype=np.float64) * 2.0 / ROT_DIM)
    ang = np.asarray(pos, np.float64)[:, None] * inv[None, :]
    cos, sin = np.cos(ang), np.sin(ang)
    p = ang.shape[0]
    c = np.ones((p, LANE)); sa = np.zeros((p, LANE)); sb = np.zeros((p, LANE))
    for o in (0, SWA_HD):
        c[:, o:o + half] = cos
        c[:, o + half:o + 2 * half] = cos
        sa[:, o + half:o + 2 * half] = sin
        sb[:, o:o + half] = -sin
    return tuple(jnp.asarray(t, F32) for t in (c, sa, sb))


def _rope_lanes(x, c, sa, sb, half):
    width = x.shape[1]
    reps = width // LANE
    if reps > 1:
        c = jnp.concatenate([c] * reps, axis=1)
        sa = jnp.concatenate([sa] * reps, axis=1)
        sb = jnp.concatenate([sb] * reps, axis=1)
    return x * c + pltpu.roll(x, half, 1) * sa + pltpu.roll(x, width - half, 1) * sb


def _rope_sublanes(xt, ct, sat, sbt, half):
    feats = xt.shape[0]
    reps = feats // LANE
    ct = jnp.concatenate([ct] * reps, axis=0)
    sat = jnp.concatenate([sat] * reps, axis=0)
    sbt = jnp.concatenate([sbt] * reps, axis=0)
    return xt * ct + pltpu.roll(xt, half, 0) * sat + pltpu.roll(xt, feats - half, 0) * sbt


def _even_proj_kernel(x_ref, c_ref, sa_ref, sb_ref, ct_ref, sat_ref, sbt_ref, nm_ref, win_ref, qn_ref,
                      wuq_ref, kvn_ref, wuk_ref, wuv_ref, u_ref, q_ref, ckv_ref, kr_ref, *kv_refs):
    h = _rms(x_ref[...], nm_ref[...]).astype(BF16)
    z = jnp.dot(h, win_ref[...], preferred_element_type=F32)
    u_ref[...] = z[:, :S5_WIDTH]
    c, sa, sb = c_ref[...], sa_ref[...], sb_ref[...]
    o1 = S5_WIDTH + MLA_Q_LORA
    o2 = o1 + MLA_KV_LORA
    qn = _rms(z[:, S5_WIDTH:o1], qn_ref[...]).astype(BF16)
    ckv = _rms(z[:, o1:o2], kvn_ref[...])
    ckv_ref[...] = ckv
    kr = _rope_lanes(z[:, o2:], c, sa, sb, MLA_ROPE // 2)
    kr_ref[...] = kr[:, MLA_NOPE:MLA_NOPE + MLA_ROPE]
    nt_dims = (((1,), (1,)), ((), ()))
    if kv_refs:
        k_ref, vt_ref = kv_refs
        qt = lax.dot_general(wuq_ref[...], qn, nt_dims, preferred_element_type=F32)
        qt = _rope_sublanes(qt, ct_ref[...], sat_ref[...], sbt_ref[...], MLA_ROPE // 2)
        tm = qt.shape[1]
        for j in range(tm // MLA_TQ):
            q_ref[0, j] = qt[:, j * MLA_TQ:(j + 1) * MLA_TQ].astype(BF16)
        ckvb = ckv.astype(BF16)
        k = jnp.dot(ckvb, wuk_ref[...], preferred_element_type=F32)
        k_ref[...] = (k + jnp.concatenate([kr] * MLA_HEADS, axis=1)).astype(BF16)
        vt = lax.dot_general(wuv_ref[...], ckvb, nt_dims, preferred_element_type=F32)
        for j in range(tm // MLA_TQ):
            vt_ref[0, j] = vt[:, j * MLA_TQ:(j + 1) * MLA_TQ].astype(BF16)
    else:
        q = jnp.dot(qn, wuq_ref[...], preferred_element_type=F32)
        q_ref[...] = _rope_lanes(q, c, sa, sb, MLA_ROPE // 2).astype(BF16)


def _even_proj(x, tabs, nm, win, qnorm, wuq, kvnorm, wuk, wuv, *, tm, n_seq, with_kv):
    rows = x.shape[0]
    nt = tabs[0].shape[0] // tm
    tabs_t = tuple(t.T for t in tabs)
    row = lambda w: pl.BlockSpec((tm, w), lambda i: (i, 0))
    tab = pl.BlockSpec((tm, LANE), lambda i: (i % nt, 0))
    tab_t = pl.BlockSpec((LANE, tm), lambda i: (0, i % nt))
    tq = min(tm, MLA_TQ)
    tblock = lambda w: pl.BlockSpec((1, tm // tq, w, tq), lambda i: (i // nt, i % nt, 0, 0))
    qw = MLA_HEADS * HEAD_PAD
    vw = MLA_HEADS * MLA_V
    if with_kv:
        q_shape, q_spec = jax.ShapeDtypeStruct((n_seq, nt * tm // tq, qw, tq), BF16), tblock(qw)
    else:
        q_shape, q_spec = jax.ShapeDtypeStruct((rows, qw), BF16), row(qw)
    out_shape = [jax.ShapeDtypeStruct((rows, S5_WIDTH), F32), q_shape,
                 jax.ShapeDtypeStruct((rows, MLA_KV_LORA), F32),
                 jax.ShapeDtypeStruct((rows, MLA_ROPE), F32)]
    out_specs = [row(S5_WIDTH), q_spec, row(MLA_KV_LORA), row(MLA_ROPE)]
    if with_kv:
        out_shape += [jax.ShapeDtypeStruct((rows, qw), BF16),
                      jax.ShapeDtypeStruct((n_seq, nt * tm // tq, vw, tq), BF16)]
        out_specs += [row(qw), tblock(vw)]
    return pl.pallas_call(
        _even_proj_kernel,
        grid=(rows // tm,),
        in_specs=[row(D_MODEL), tab, tab, tab, tab_t, tab_t, tab_t, _const_spec(nm.shape),
                  _const_spec(win.shape), _const_spec(qnorm.shape), _const_spec(wuq.shape),
                  _const_spec(kvnorm.shape), _const_spec(wuk.shape), _const_spec(wuv.shape)],
        out_specs=out_specs,
        out_shape=out_shape,
        compiler_params=_cparams(("arbitrary",)),
        name="even_proj",
    )(x, *tabs, *tabs_t, nm, win, qnorm, wuq, kvnorm, wuk, wuv)


S5_STRIP = 512


def _s5_kernel(u_ref, h0r_ref, h0i_ref, lamr_ref, lami_ref, ldt_ref, wb_ref, wcr_ref, wci_ref,
               d_ref, wglu_ref, bglu_ref, y_ref, hr_ref, hi_ref, ut_ref, xr_ref, xi_ref, yt_ref,
               *, nb, tt, transpose_io):
    @pl.when(pl.program_id(0) == 0)
    def _():
        xr_ref[0:nb, :] = h0r_ref[...]
        xi_ref[0:nb, :] = h0i_ref[...]

    if transpose_io:
        for t in range(tt):
            ut_ref[t * nb:(t + 1) * nb, :] = u_ref[:, t, :]
    else:
        ut_ref[...] = u_ref[0]
    ub = ut_ref[...].astype(BF16)

    lr = jnp.minimum(lamr_ref[...], -1e-4)
    li = lami_ref[...]
    dt = jnp.exp(ldt_ref[...])
    mag = jnp.exp(lr * dt)
    ar = mag * jnp.cos(li * dt)
    ai = mag * jnp.sin(li * dt)
    den = lr * lr + li * li
    zr = ((ar - 1.0) * lr + ai * li) / den
    zi = (ai * lr - (ar - 1.0) * li) / den

    ucols = S5_STRIP // S5_STATE * S5_GROUP
    ys = []
    for s in range(S5_NSTATE // S5_STRIP):
        cols = slice(s * S5_STRIP, (s + 1) * S5_STRIP)
        us = ub[:, s * ucols:(s + 1) * ucols]
        br = jnp.dot(us, wb_ref[s, :, :S5_STRIP], preferred_element_type=F32)
        bi = jnp.dot(us, wb_ref[s, :, S5_STRIP:], preferred_element_type=F32)
        zrs, zis = zr[:, cols], zi[:, cols]
        xr_ref[nb:, cols] = zrs * br - zis * bi
        xi_ref[nb:, cols] = zrs * bi + zis * br
        ars = jnp.broadcast_to(ar[:, cols], (nb, S5_STRIP))
        ais = jnp.broadcast_to(ai[:, cols], (nb, S5_STRIP))
        hr, hi = xr_ref[0:nb, cols], xi_ref[0:nb, cols]
        for t in range(tt):
            r = slice((t + 1) * nb, (t + 2) * nb)
            hr, hi = (ars * hr - ais * hi + xr_ref[r, cols],
                      ars * hi + ais * hr + xi_ref[r, cols])
            xr_ref[r, cols] = hr
            xi_ref[r, cols] = hi
        xr_ref[0:nb, cols] = hr
        xi_ref[0:nb, cols] = hi
        ys.append(jnp.dot(xr_ref[nb:, cols].astype(BF16), wcr_ref[s], preferred_element_type=F32)
                  + jnp.dot(xi_ref[nb:, cols].astype(BF16), wci_ref[s],
                            preferred_element_type=F32))

    hr_ref[...] = xr_ref[0:nb, :]
    hi_ref[...] = xi_ref[0:nb, :]
    y = jnp.concatenate(ys, axis=1) + d_ref[...] * ut_ref[...]
    y = jax.nn.gelu(y)
    gate = jnp.dot(y.astype(BF16), wglu_ref[...], preferred_element_type=F32) + bglu_ref[...]
    y = y * jax.nn.sigmoid(gate)
    if transpose_io:
        groups = S5_WIDTH // LANE
        for c in range(groups):
            yt_ref[c] = y[:, c * LANE:(c + 1) * LANE]
        for n in range(nb):
            y_ref[n] = jnp.concatenate(
                [yt_ref[c, pl.ds(n, tt, stride=nb), :] for c in range(groups)],
                axis=1).astype(y_ref.dtype)
    else:
        y_ref[0] = y.astype(y_ref.dtype)


def _s5(u, h0r, h0i, lamr, lami, ldt, wb, wcr, wci, d, wglu, bglu, *, nb, tt, transpose_io):
    t_total = u.shape[1] if transpose_io else u.shape[0]
    if transpose_io:
        u_spec = pl.BlockSpec((nb, tt, S5_WIDTH), lambda i: (0, i, 0))
        y_shape = (nb, t_total, S5_WIDTH)
    else:
        u_spec = pl.BlockSpec((tt, nb, S5_WIDTH), lambda i: (i, 0, 0))
        y_shape = (t_total, nb, S5_WIDTH)
    consts = (h0r, h0i, lamr, lami, ldt, wb, wcr, wci, d, wglu, bglu)
    kern = functools.partial(_s5_kernel, nb=nb, tt=tt, transpose_io=transpose_io)
    return pl.pallas_call(
        kern,
        grid=(t_total // tt,),
        in_specs=[u_spec] + [_const_spec(c.shape) for c in consts],
        out_specs=[u_spec, _const_spec((nb, S5_NSTATE)), _const_spec((nb, S5_NSTATE))],
        out_shape=[jax.ShapeDtypeStruct(y_shape, BF16),
                   jax.ShapeDtypeStruct((nb, S5_NSTATE), F32),
                   jax.ShapeDtypeStruct((nb, S5_NSTATE), F32)],
        scratch_shapes=[pltpu.VMEM((nb * tt, S5_WIDTH), F32),
                        pltpu.VMEM((nb * (tt + 1), S5_NSTATE), F32),
                        pltpu.VMEM((nb * (tt + 1), S5_NSTATE), F32),
                        pltpu.VMEM((S5_WIDTH // LANE, nb * tt, LANE), F32)],
        compiler_params=_cparams(("arbitrary",)),
        name="s5_mixer",
    )(u, *consts)


MLA_LOOKAHEAD = 3


def _mla_attn_kernel(qt_ref, k_ref, vt_ref, o_ref, m_ref, l_ref, acc_ref, *, tq):
    qi = pl.program_id(1)
    m_ref[...] = jnp.full(m_ref.shape, NEG, F32)
    l_ref[...] = jnp.zeros(l_ref.shape, F32)
    acc_ref[...] = jnp.zeros(acc_ref.shape, F32)

    def causal_mask(nblk):
        key = lax.broadcasted_iota(jnp.int32, (nblk * tq, tq), 0)
        qry = lax.broadcasted_iota(jnp.int32, (nblk * tq, tq), 1)
        return key <= qry + (nblk - 1) * tq

    def process(kb0, nblk, mask):
        keys = pl.ds(pl.multiple_of(kb0 * tq, tq), nblk * tq)

        def scores(h):
            hl = slice(h * HEAD_PAD, (h + 1) * HEAD_PAD)
            return jnp.dot(k_ref[0, keys, hl], qt_ref[0, 0, hl, :],
                           preferred_element_type=F32)

        pending = [scores(h) for h in range(MLA_LOOKAHEAD)]
        for h in range(MLA_HEADS):
            st = pending.pop(0) * (MLA_SCALE * LOG2E)
            if h + MLA_LOOKAHEAD < MLA_HEADS:
                pending.append(scores(h + MLA_LOOKAHEAD))
            if mask is not None:
                st = jnp.where(mask, st, NEG)
            j = h // 2
            vts = [vt_ref[0, kb0 + i, j * LANE:(j + 1) * LANE, :] for i in range(nblk)]
            vt = vts[0] if nblk == 1 else jnp.concatenate(vts, axis=1)
            m_old = m_ref[h]
            m_new = jnp.maximum(m_old, jnp.max(st, axis=0, keepdims=True))
            alpha = jnp.exp2(m_old - m_new)
            pt = jnp.exp2(st - m_new)
            l_ref[h] = alpha * l_ref[h] + jnp.sum(pt, axis=0, keepdims=True)
            m_ref[h] = m_new
            acc_ref[h] = alpha * acc_ref[h] + jnp.dot(vt, pt.astype(BF16),
                                                      preferred_element_type=F32)

    n_wide = qi // 2

    def wide_step(i, carry):
        process(i * 2, 2, None)
        return carry

    lax.fori_loop(0, n_wide, wide_step, 0)

    @pl.when(qi % 2 == 0)
    def _():
        process(qi, 1, causal_mask(1))

    @pl.when(qi % 2 == 1)
    def _():
        process(qi - 1, 2, causal_mask(2))

    half = lax.broadcasted_iota(jnp.int32, (LANE, tq), 0) < MLA_V
    for j in range(MLA_HEADS // 2):
        ot = jnp.where(half, acc_ref[2 * j] / l_ref[2 * j], acc_ref[2 * j + 1] / l_ref[2 * j + 1])
        o_ref[0, :, j * LANE:(j + 1) * LANE] = ot.T.astype(o_ref.dtype)


def _mla_attn(qt, k, vt):
    n, nblocks, qw, tq = qt.shape
    t = k.shape[1]
    kern = functools.partial(_mla_attn_kernel, tq=tq)
    return pl.pallas_call(
        kern,
        grid=(n, nblocks),
        in_specs=[pl.BlockSpec((1, 1, qw, tq), lambda b, i: (b, i, 0, 0)),
                  pl.BlockSpec((1, t, k.shape[2]), lambda b, i: (b, 0, 0)),
                  pl.BlockSpec((1, nblocks, vt.shape[2], tq), lambda b, i: (b, 0, 0, 0))],
        out_specs=pl.BlockSpec((1, tq, MLA_HEADS * MLA_V), lambda b, i: (b, i, 0)),
        out_shape=jax.ShapeDtypeStruct((n, t, MLA_HEADS * MLA_V), BF16),
        scratch_shapes=[pltpu.VMEM((MLA_HEADS, 1, tq), F32),
                        pltpu.VMEM((MLA_HEADS, 1, tq), F32),
                        pltpu.VMEM((MLA_HEADS, LANE, tq), F32)],
        compiler_params=_cparams(("arbitrary", "arbitrary")),
        name="mla_prompt_attn",
    )(qt, k, vt)


DEC_PAGES = 32
DEC_SUB = 4
DEC_AHEAD = 2
DEC_SLOTS = DEC_AHEAD + 1
DEC_KEYS = DEC_PAGES * PAGE


def _qlat_kernel(q_ref, wk_ref, o_ref):
    for h in range(MLA_HEADS):
        o_ref[:, h * MLA_KV_LORA:(h + 1) * MLA_KV_LORA] = jnp.dot(
            q_ref[:, h * HEAD_PAD:(h + 1) * HEAD_PAD], wk_ref[h], preferred_element_type=F32)


def _qlat(q_big, wk_t):
    rows = q_big.shape[0]
    return pl.pallas_call(
        _qlat_kernel,
        grid=(1,),
        in_specs=[_const_spec(q_big.shape), _const_spec(wk_t.shape)],
        out_specs=_const_spec((rows, MLA_HEADS * MLA_KV_LORA)),
        out_shape=jax.ShapeDtypeStruct((rows, MLA_HEADS * MLA_KV_LORA), F32),
        compiler_params=_cparams(("arbitrary",)),
        name="mla_q_absorb",
    )(q_big, wk_t)


def _olat_kernel(o_ref, wv_ref, out_ref):
    for j in range(MLA_HEADS // 2):
        acc = None
        for h in (2 * j, 2 * j + 1):
            part = jnp.dot(o_ref[:, h * MLA_KV_LORA:(h + 1) * MLA_KV_LORA].astype(BF16),
                           wv_ref[:, h * LANE:(h + 1) * LANE], preferred_element_type=F32)
            acc = part if acc is None else acc + part
        out_ref[:, j * LANE:(j + 1) * LANE] = acc.astype(out_ref.dtype)


def _olat(o_lat, wv_big):
    rows = o_lat.shape[0]
    return pl.pallas_call(
        _olat_kernel,
        grid=(1,),
        in_specs=[_const_spec(o_lat.shape), _const_spec(wv_big.shape)],
        out_specs=_const_spec((rows, MLA_HEADS * MLA_V)),
        out_shape=jax.ShapeDtypeStruct((rows, MLA_HEADS * MLA_V), BF16),
        compiler_params=_cparams(("arbitrary",)),
        name="mla_v_expand",
    )(o_lat, wv_big)


def _page_copies(cache_ckv, cache_kr, ckv_buf, kr_buf, sem, slot, page, p):
    rows = pl.ds(p * PAGE, PAGE)
    return (pltpu.make_async_copy(cache_ckv.at[0, page], ckv_buf.at[slot, rows], sem.at[0, slot]),
            pltpu.make_async_copy(cache_kr.at[0, page], kr_buf.at[slot, p], sem.at[1, slot]))


def _mla_decode_kernel(ptc_ref, ptn_ref, ql_ref, qr_ref, ckvn_ref, krn_ref, cache_ckv, cache_kr,
                       o_ref, ckv_buf, kr_buf, sem, *, n_seq, n_chunks):
    b = pl.program_id(0)
    g0 = b * n_chunks

    def start_page(page, slot, p):
        for cp in _page_copies(cache_ckv, cache_kr, ckv_buf, kr_buf, sem, slot, page, p):
            cp.start()

    def wait_chunk(slot):
        for p in range(DEC_PAGES):
            for cp in _page_copies(cache_ckv, cache_kr, ckv_buf, kr_buf, sem, slot, 0, p):
                cp.wait()

    @pl.when(b == 0)
    def _():
        for a in range(DEC_AHEAD):
            for p in range(DEC_PAGES):
                start_page(ptc_ref[0, 0, a * DEC_PAGES + p], a, p)

    ql = ql_ref[0].astype(BF16)
    qr = qr_ref[0].astype(BF16)

    def chunk_body(c, carry):
        m, l, acc = carry
        slot = lax.rem(g0 + c, DEC_SLOTS)
        ahead_slot = lax.rem(g0 + c + DEC_AHEAD, DEC_SLOTS)
        wait_chunk(slot)
        wraps = c + DEC_AHEAD >= n_chunks
        ahead_base = jnp.where(wraps, c + DEC_AHEAD - n_chunks, c + DEC_AHEAD) * DEC_PAGES

        def ahead_page(p):
            return jnp.where(wraps, ptn_ref[0, 0, ahead_base + p], ptc_ref[0, 0, ahead_base + p])

        kbs, scores = [], []
        for i in range(DEC_PAGES // DEC_SUB):
            kb = ckv_buf[slot, i * DEC_SUB * PAGE:(i + 1) * DEC_SUB * PAGE, :].astype(BF16)
            krt = jnp.concatenate([kr_buf[slot, i * DEC_SUB + p] for p in range(DEC_SUB)],
                                  axis=1).astype(BF16)
            kbs.append(kb)
            scores.append(
                (lax.dot_general(ql, kb, (((1,), (1,)), ((), ())), preferred_element_type=F32)
                 + jnp.dot(qr, krt, preferred_element_type=F32)) * MLA_SCALE)
            for p in range(i * DEC_SUB, (i + 1) * DEC_SUB):
                start_page(ahead_page(p), ahead_slot, p)
        for kb, s in zip(kbs, scores):
            m_new = jnp.maximum(m, jnp.max(s, axis=1, keepdims=True))
            alpha = jnp.exp(m - m_new)
            pr = jnp.exp(s - m_new)
            l = alpha * l + jnp.sum(pr, axis=1, keepdims=True)
            acc = alpha * acc + jnp.dot(pr.astype(BF16), kb, preferred_element_type=F32)
            m = m_new
        return m, l, acc

    init = (jnp.full((MLA_HEADS, 1), NEG, F32), jnp.zeros((MLA_HEADS, 1), F32),
            jnp.zeros((MLA_HEADS, MLA_KV_LORA), F32))
    m, l, acc = lax.fori_loop(0, n_chunks, chunk_body, init)

    @pl.when(b == n_seq - 1)
    def _():
        for a in range(DEC_AHEAD):
            wait_chunk((n_seq * n_chunks + a) % DEC_SLOTS)

    cn = ckvn_ref[0].astype(BF16).astype(F32)
    kn = krn_ref[0].astype(BF16).astype(F32)
    s_new = (jnp.sum(ql.astype(F32) * cn, axis=1, keepdims=True)
             + jnp.sum(qr.astype(F32) * kn, axis=1, keepdims=True)) * MLA_SCALE
    m_new = jnp.maximum(m, s_new)
    alpha = jnp.exp(m - m_new)
    p_new = jnp.exp(s_new - m_new)
    l = alpha * l + p_new
    acc = alpha * acc + p_new.astype(BF16).astype(F32) * cn
    o_ref[0] = acc / l


def _mla_decode(page_table, q_lat, q_rope, ckv_new, kr_new, cache_ckv, cache_kr):
    n_seq, n_pages = page_table.shape
    n_chunks = n_pages // DEC_PAGES
    assert n_pages % DEC_PAGES == 0 and n_chunks >= DEC_AHEAD
    pt = page_table.reshape(n_seq, 1, n_pages)
    smem_row = lambda f: pl.BlockSpec((1, 1, n_pages), f, memory_space=pltpu.SMEM)
    per_seq = lambda a: pl.BlockSpec((1,) + a.shape[1:], lambda b: (b, 0, 0))
    kern = functools.partial(_mla_decode_kernel, n_seq=n_seq, n_chunks=n_chunks)
    return pl.pallas_call(
        kern,
        grid=(n_seq,),
        in_specs=[smem_row(lambda b: (b, 0, 0)),
                  smem_row(lambda b: (jnp.minimum(b + 1, n_seq - 1), 0, 0)),
                  per_seq(q_lat), per_seq(q_rope), per_seq(ckv_new), per_seq(kr_new),
                  pl.BlockSpec(memory_space=pl.ANY), pl.BlockSpec(memory_space=pl.ANY)],
        out_specs=pl.BlockSpec((1, MLA_HEADS, MLA_KV_LORA), lambda b: (b, 0, 0)),
        out_shape=jax.ShapeDtypeStruct((n_seq, MLA_HEADS, MLA_KV_LORA), F32),
        scratch_shapes=[pltpu.VMEM((DEC_SLOTS, DEC_KEYS, MLA_KV_LORA), F32),
                        pltpu.VMEM((DEC_SLOTS, DEC_PAGES, MLA_ROPE, PAGE), F32),
                        pltpu.SemaphoreType.DMA((2, DEC_SLOTS))],
        compiler_params=_cparams(("arbitrary",)),
        name="mla_paged_decode",
    )(pt, pt, q_lat, q_rope, ckv_new, kr_new, cache_ckv, cache_kr)


def _odd_proj_kernel(x_ref, c_ref, sa_ref, sb_ref, nm_ref, win_ref, q_ref, k_ref, v_ref):
    h = _rms(x_ref[...], nm_ref[...]).astype(BF16)
    z = jnp.dot(h, win_ref[...], preferred_element_type=F32)
    qk = _rope_lanes(z[:, :QK_ODD], c_ref[...], sa_ref[...], sb_ref[...], ROT_DIM // 2)
    q_ref[...] = qk[:, :SWA_HEADS * SWA_HD].astype(BF16)
    k_ref[...] = qk[:, SWA_HEADS * SWA_HD:]
    v_ref[...] = z[:, QK_ODD:]


def _odd_proj(x, tabs, nm, win, *, tm):
    rows = x.shape[0]
    nt = tabs[0].shape[0] // tm
    row = lambda w: pl.BlockSpec((tm, w), lambda i: (i, 0))
    tab = pl.BlockSpec((tm, LANE), lambda i: (i % nt, 0))
    kvw = SWA_KV_HEADS * SWA_HD
    return pl.pallas_call(
        _odd_proj_kernel,
        grid=(rows // tm,),
        in_specs=[row(D_MODEL), tab, tab, tab, _const_spec(nm.shape), _const_spec(win.shape)],
        out_specs=[row(SWA_HEADS * SWA_HD), row(kvw), row(kvw)],
        out_shape=[jax.ShapeDtypeStruct((rows, SWA_HEADS * SWA_HD), BF16),
                   jax.ShapeDtypeStruct((rows, kvw), F32),
                   jax.ShapeDtypeStruct((rows, kvw), F32)],
        compiler_params=_cparams(("arbitrary",)),
        name="odd_proj",
    )(x, *tabs, nm, win)


def _odd_proj_t_kernel(x_ref, ct_ref, sat_ref, sbt_ref, nm_ref, wt_ref, qt_ref, k_ref, vt_ref,
                       ktail_ref, vtail_ref, *, tm):
    h = _rms(x_ref[...], nm_ref[...]).astype(BF16)
    zt = lax.dot_general(wt_ref[...], h, (((1,), (1,)), ((), ())),
                         preferred_element_type=F32)
    qkt = _rope_sublanes(zt[:QK_ODD], ct_ref[...], sat_ref[...], sbt_ref[...], ROT_DIM // 2)
    nq = SWA_HEADS * SWA_HD
    for i in range(tm // WINDOW):
        cols = slice(i * WINDOW, (i + 1) * WINDOW)
        qt_ref[0, i] = qkt[:nq, cols].astype(BF16)
        vt_ref[0, i] = zt[QK_ODD:, cols].astype(BF16)
    k = qkt[nq:].T
    k_ref[...] = k.astype(BF16)
    ktail_ref[0] = k[tm - WINDOW:]
    vtail_ref[0] = zt[QK_ODD:, tm - WINDOW:].T


def _odd_proj_t(x, tabs, nm, win_t, *, tm, n_seq):
    rows = x.shape[0]
    t = rows // n_seq
    nt = t // tm
    nb = tm // WINDOW
    kvw = SWA_KV_HEADS * SWA_HD
    qw = SWA_HEADS * SWA_HD
    tabs_t = tuple(a.T for a in tabs)
    row = lambda w: pl.BlockSpec((tm, w), lambda i: (i, 0))
    tab_t = pl.BlockSpec((LANE, tm), lambda i: (0, i % nt))
    tblock = lambda w: pl.BlockSpec((1, nb, w, WINDOW), lambda i: (i // nt, i % nt, 0, 0))
    tail = pl.BlockSpec((1, WINDOW, kvw), lambda i: (i // nt, 0, 0))
    return pl.pallas_call(
        functools.partial(_odd_proj_t_kernel, tm=tm),
        grid=(rows // tm,),
        in_specs=[row(D_MODEL), tab_t, tab_t, tab_t, _const_spec(nm.shape),
                  _const_spec(win_t.shape)],
        out_specs=[tblock(qw), row(kvw), tblock(kvw), tail, tail],
        out_shape=[jax.ShapeDtypeStruct((n_seq, t // WINDOW, qw, WINDOW), BF16),
                   jax.ShapeDtypeStruct((rows, kvw), BF16),
                   jax.ShapeDtypeStruct((n_seq, t // WINDOW, kvw, WINDOW), BF16),
                   jax.ShapeDtypeStruct((n_seq, WINDOW, kvw), F32),
                   jax.ShapeDtypeStruct((n_seq, WINDOW, kvw), F32)],
        compiler_params=_cparams(("arbitrary",)),
        name="odd_proj_t",
    )(x, *tabs_t, nm, win_t)


SWA_QBLOCKS = 4


def _swa_attn_kernel(sink_ref, qt_ref, kp_ref, kc_ref, vtp_ref, vtc_ref, o_ref):
    step = pl.program_id(1)
    cols = SWA_GQ * WINDOW
    key = lax.broadcasted_iota(jnp.int32, (2 * WINDOW, cols), 0)
    qry = jnp.bitwise_and(lax.broadcasted_iota(jnp.int32, (2 * WINDOW, cols), 1), WINDOW - 1)
    rel = qry + WINDOW - key
    band = (rel >= 0) & (rel <= WINDOW)
    lane = lax.broadcasted_iota(jnp.int32, (1, cols), 1)
    sinks = []
    for kh in range(SWA_KV_HEADS):
        row = jnp.full((1, cols), sink_ref[kh * SWA_GQ + SWA_GQ - 1], F32)
        for g in range(SWA_GQ - 2, -1, -1):
            row = jnp.where(lane < (g + 1) * WINDOW, sink_ref[kh * SWA_GQ + g], row)
        sinks.append(row * LOG2E)
    for sub in range(SWA_QBLOCKS):
        if sub == 0:
            k_prev, vt_prev = kp_ref[0], vtp_ref[0, 0]
            valid = band & ((key >= WINDOW) | (step > 0))
        else:
            k_prev, vt_prev = kc_ref[0, (sub - 1) * WINDOW:sub * WINDOW], vtc_ref[0, sub - 1]
            valid = band
        kk = jnp.concatenate([k_prev, kc_ref[0, sub * WINDOW:(sub + 1) * WINDOW]], axis=0)
        vvt = jnp.concatenate([vt_prev, vtc_ref[0, sub]], axis=1)
        scores = []
        for kh in range(SWA_KV_HEADS):
            qg = jnp.concatenate(
                [qt_ref[0, sub, (kh * SWA_GQ + g) * SWA_HD:(kh * SWA_GQ + g + 1) * SWA_HD, :]
                 for g in range(SWA_GQ)], axis=1)
            scores.append(jnp.dot(kk[:, kh * SWA_HD:(kh + 1) * SWA_HD], qg,
                                  preferred_element_type=F32))
        outs = []
        for kh in range(SWA_KV_HEADS):
            st = jnp.where(valid, scores[kh] * (SWA_SCALE * LOG2E), NEG)
            m = jnp.maximum(jnp.max(st, axis=0, keepdims=True), sinks[kh])
            p = jnp.exp2(st - m)
            l = jnp.sum(p, axis=0, keepdims=True) + jnp.exp2(sinks[kh] - m)
            ot = jnp.dot(vvt[kh * SWA_HD:(kh + 1) * SWA_HD, :], p.astype(BF16),
                         preferred_element_type=F32) / l
            outs += [ot[:, g * WINDOW:(g + 1) * WINDOW] for g in range(SWA_GQ)]
        o_ref[0, sub * WINDOW:(sub + 1) * WINDOW, :] = (
            jnp.concatenate(outs, axis=0).T.astype(o_ref.dtype))


def _swa_attn(sinks, qt, k, vt):
    n, nblocks, qw, _ = qt.shape
    t = k.shape[1]
    kvw = SWA_KV_HEADS * SWA_HD
    qb = SWA_QBLOCKS
    before = lambda i: jnp.maximum(i * qb - 1, 0)
    return pl.pallas_call(
        _swa_attn_kernel,
        grid=(n, nblocks // qb),
        in_specs=[pl.BlockSpec(memory_space=pltpu.SMEM),
                  pl.BlockSpec((1, qb, qw, WINDOW), lambda a, i: (a, i, 0, 0)),
                  pl.BlockSpec((1, WINDOW, kvw), lambda a, i: (a, before(i), 0)),
                  pl.BlockSpec((1, qb * WINDOW, kvw), lambda a, i: (a, i, 0)),
                  pl.BlockSpec((1, 1, kvw, WINDOW), lambda a, i: (a, before(i), 0, 0)),
                  pl.BlockSpec((1, qb, kvw, WINDOW), lambda a, i: (a, i, 0, 0))],
        out_specs=pl.BlockSpec((1, qb * WINDOW, qw), lambda a, i: (a, i, 0)),
        out_shape=jax.ShapeDtypeStruct((n, t, qw), BF16),
        compiler_params=_cparams(("arbitrary", "arbitrary")),
        name="swa_prompt_attn",
    )(sinks, qt, k, k, vt, vt)


SWA_DEC_SEQS = 8


def _swa_decode_kernel(sink_ref, q_ref, kn_ref, vn_ref, ck_ref, cv_ref, o_ref):
    q = q_ref[...]
    hgrp = lax.broadcasted_iota(jnp.int32, (1, SWA_HEADS, 1), 1) // SWA_GQ
    hidx = lax.broadcasted_iota(jnp.int32, (1, SWA_HEADS, 1), 1)
    sink = jnp.zeros((1, SWA_HEADS, 1), F32)
    for h in range(SWA_HEADS):
        sink = jnp.where(hidx == h, sink_ref[h], sink)
    s = jnp.zeros((q.shape[0], SWA_HEADS, WINDOW), F32)
    for kh in range(SWA_KV_HEADS):
        skh = jnp.einsum('nhd,ndj->nhj', q, ck_ref[:, kh].astype(BF16),
                         preferred_element_type=F32)
        s = jnp.where(hgrp == kh, skh, s)
    s = s * SWA_SCALE
    qf = q.astype(F32)
    kn = kn_ref[...].astype(BF16).astype(F32)
    vn = vn_ref[...].astype(BF16).astype(F32)
    s_new = jnp.sum(qf * kn, axis=2, keepdims=True) * SWA_SCALE
    m = jnp.maximum(jnp.maximum(jnp.max(s, axis=2, keepdims=True), s_new), sink)
    p = jnp.exp(s - m)
    p_new = jnp.exp(s_new - m)
    l = jnp.sum(p, axis=2, keepdims=True) + p_new + jnp.exp(sink - m)
    o = p_new.astype(BF16).astype(F32) * vn
    for kh in range(SWA_KV_HEADS):
        pk = jnp.where(hgrp == kh, p, 0.0).astype(BF16)
        o = o + jnp.einsum('nhj,ndj->nhd', pk, cv_ref[:, kh].astype(BF16),
                           preferred_element_type=F32)
    o_ref[...] = (o / l).astype(o_ref.dtype)


def _swa_decode(sinks, q, k_new, v_new, cache_k, cache_v):
    n = q.shape[0]
    nb = SWA_DEC_SEQS
    seq = pl.BlockSpec((nb, SWA_HEADS, SWA_HD), lambda i: (i, 0, 0))
    cache = pl.BlockSpec((nb, SWA_KV_HEADS, SWA_HD, WINDOW), lambda i: (i, 0, 0, 0))
    return pl.pallas_call(
        _swa_decode_kernel,
        grid=(n // nb,),
        in_specs=[pl.BlockSpec(memory_space=pltpu.SMEM), seq, seq, seq, cache, cache],
        out_specs=seq,
        out_shape=jax.ShapeDtypeStruct((n, SWA_HEADS, SWA_HD), BF16),
        compiler_params=_cparams(("arbitrary",)),
        name="swa_decode_attn",
    )(sinks, q, k_new, v_new, cache_k, cache_v)


def _post_kernel(*refs, n_mix, decode, final, tm):
    it = iter(refs)
    x_ref = next(it)
    mix_refs = [next(it) for _ in range(n_mix)]
    wo_refs = [next(it) for _ in range(n_mix)]
    nf_ref, wg_ref, wu_ref, cw_ref, cb_ref, wd_ref = (next(it) for _ in range(6))
    if decode:
        buf0_ref, buf1_ref = next(it), next(it)
    fn_ref = next(it) if final else None
    y_ref = next(it)
    g_ref = next(it)
    act_ref = next(it)
    carry_ref = None if decode else next(it)

    x1 = x_ref[...]
    for a_ref, w_ref in zip(mix_refs, wo_refs):
        x1 = x1 + jnp.dot(a_ref[...], w_ref[...], preferred_element_type=F32)
    h2 = _rms(x1, nf_ref[...]).astype(BF16)

    if not decode:
        @pl.when(pl.program_id(1) == 0)
        def _():
            carry_ref[...] = jnp.zeros_like(carry_ref)
        row = lax.broadcasted_iota(jnp.int32, (tm, FF_CHUNK), 0)

    for c in range(D_FF // FF_CHUNK):
        sl = slice(c * FF_CHUNK, (c + 1) * FF_CHUNK)
        g = jnp.dot(h2, wg_ref[:, sl], preferred_element_type=F32)
        u = jnp.dot(h2, wu_ref[:, sl], preferred_element_type=F32)
        if decode:
            gm2, gm1 = buf0_ref[:, sl], buf1_ref[:, sl]
            g_ref[:, sl] = g
        else:
            prev = carry_ref[:, sl]
            p6, p7 = prev[SUBLANE - 2:SUBLANE - 1, :], prev[SUBLANE - 1:SUBLANE, :]
            gm1 = jnp.where(row == 0, p7, pltpu.roll(g, 1, 0))
            gm2 = jnp.where(row == 0, p6, jnp.where(row == 1, p7, pltpu.roll(g, 2, 0)))
            last = g[tm - SUBLANE:tm, :]
            carry_ref[:, sl] = last
            g_ref[0, :, sl] = last
        cc = cb_ref[:, sl] + cw_ref[0:1, sl] * gm2 + cw_ref[1:2, sl] * gm1 + cw_ref[2:3, sl] * g
        act_ref[:, sl] = (jax.nn.gelu(cc) * u).astype(BF16)

    x2 = x1 + jnp.dot(act_ref[...], wd_ref[...], preferred_element_type=F32)
    y_ref[...] = _rms(x2, fn_ref[...]) if final else x2


def _post(x, mixes, wos, nf, wg, wu, cw, cb, wd, *, n_seq, tm, conv_bufs=None, final_norm=None):
    rows = x.shape[0]
    decode = conv_bufs is not None
    nt = rows // n_seq // tm if not decode else rows // tm
    grid = (1, nt) if decode else (n_seq, nt)
    row = lambda w: pl.BlockSpec((tm, w), lambda a, i: (a * nt + i, 0))
    args = [x] + list(mixes) + list(wos) + [nf, wg, wu, cw, cb, wd]
    in_specs = ([row(D_MODEL)] + [row(m.shape[1]) for m in mixes]
                + [_single_spec(w.shape) for w in wos]
                + [_const_spec(nf.shape), _single_spec(wg.shape), _single_spec(wu.shape),
                   _const_spec(cw.shape), _const_spec(cb.shape), _single_spec(wd.shape)])
    if decode:
        args += list(conv_bufs)
        in_specs += [row(D_FF), row(D_FF)]
    if final_norm is not None:
        args.append(final_norm)
        in_specs.append(_const_spec(final_norm.shape))
    if decode:
        g_shape = jax.ShapeDtypeStruct((rows, D_FF), F32)
        g_spec = row(D_FF)
    else:
        g_shape = jax.ShapeDtypeStruct((n_seq, SUBLANE, D_FF), F32)
        g_spec = pl.BlockSpec((1, SUBLANE, D_FF), lambda a, i: (a, 0, 0))
    scratch = [pltpu.VMEM((tm, D_FF), BF16)]
    if not decode:
        scratch.append(pltpu.VMEM((SUBLANE, D_FF), F32))
    kern = functools.partial(_post_kernel, n_mix=len(mixes), decode=decode,
                             final=final_norm is not None, tm=tm)
    return pl.pallas_call(
        kern,
        grid=grid,
        in_specs=in_specs,
        out_specs=[row(D_MODEL), g_spec],
        out_shape=[jax.ShapeDtypeStruct((rows, D_MODEL), F32), g_shape],
        scratch_shapes=scratch,
        compiler_params=_cparams(("arbitrary", "arbitrary")),
        name="post_decode" if decode else "post_prompt",
    )(*args)


def _prep_even(e_w_in, e_w_uq, e_w_uk, e_w_uv, e_s5_b_re, e_s5_b_im, e_s5_c_re, e_s5_c_im):
    o1 = S5_WIDTH
    o2 = o1 + MLA_Q_LORA
    o3 = o2 + MLA_KV_LORA
    zpad = lambda n: jnp.zeros((D_MODEL, n), F32)
    win = jnp.concatenate([e_w_in[:, :o3], zpad(MLA_NOPE), e_w_in[:, o3:],
                           zpad(HEAD_PAD - MLA_NOPE - MLA_ROPE)], axis=1).astype(BF16)
    pad_last = lambda w, n: jnp.pad(w, ((0, 0), (0, 0), (0, n - w.shape[2])))
    wuq = pad_last(e_w_uq, HEAD_PAD).reshape(MLA_Q_LORA, MLA_HEADS * HEAD_PAD).astype(BF16)
    wuk = pad_last(e_w_uk, HEAD_PAD).reshape(MLA_KV_LORA, MLA_HEADS * HEAD_PAD).astype(BF16)
    wuv = e_w_uv.reshape(MLA_KV_LORA, MLA_HEADS * MLA_V).astype(BF16)
    wk_t = jnp.pad(jnp.transpose(e_w_uk, (1, 2, 0)),
                   ((0, 0), (0, HEAD_PAD - MLA_NOPE), (0, 0))).astype(BF16)
    wv4 = e_w_uv.reshape(MLA_KV_LORA, MLA_HEADS // 2, 2, MLA_V)
    zv = jnp.zeros_like(wv4[:, :, 0])
    wv_big = jnp.stack([jnp.concatenate([wv4[:, :, 0], zv], axis=-1),
                        jnp.concatenate([zv, wv4[:, :, 1]], axis=-1)], axis=2)
    wv_big = wv_big.reshape(MLA_KV_LORA, MLA_HEADS * LANE).astype(BF16)
    gs = S5_STRIP // S5_STATE
    ns = S5_GROUPS // gs
    eye = jnp.eye(gs, dtype=F32)
    bd_in = lambda b: jnp.einsum('sgph,gk->sghkp', b.reshape(ns, gs, S5_STATE, S5_GROUP),
                                 eye).reshape(ns, gs * S5_GROUP, S5_STRIP)
    bd_out = lambda c: jnp.einsum('sghp,gk->sgpkh', c.reshape(ns, gs, S5_GROUP, S5_STATE),
                                  eye).reshape(ns, S5_STRIP, gs * S5_GROUP)
    wb = jnp.concatenate([bd_in(e_s5_b_re), bd_in(e_s5_b_im)], axis=2).astype(BF16)
    wcr = bd_out(e_s5_c_re).astype(BF16)
    wci = (-bd_out(e_s5_c_im)).astype(BF16)
    return win, wuq, wuk, wuv, wk_t, wv_big, wb, wcr, wci


def kernel(x_prompt, x_sample, page_table, state_s5_re, state_s5_im, cache_mla_ckv, cache_mla_krope,
           cache_swa_k, cache_swa_v, state_ffn_conv, norm_mix, norm_ffn, final_norm, e_w_in,
           e_s5_lam_re, e_s5_lam_im, e_s5_log_dt, e_s5_b_re, e_s5_b_im, e_s5_c_re, e_s5_c_im, e_s5_d,
           e_s5_w_glu, e_s5_b_glu, e_q_norm, e_w_uq, e_kv_norm, e_w_uk, e_w_uv, e_w_out, o_w_in,
           o_sinks, o_w_out, f_w_gate, f_w_up, f_conv_w, f_conv_b, f_w_down):
    n_p, t_p, _ = x_prompt.shape
    n_s, t_s, _ = x_sample.shape
    assert t_s == 1
    past_len = page_table.shape[1] * PAGE
    row2 = lambda v: v.reshape(1, -1)

    (win, wuq, wuk, wuv, wk_t, wv_big, wb, wcr, wci) = _prep_even(
        e_w_in[0], e_w_uq[0], e_w_uk[0], e_w_uv[0], e_s5_b_re[0], e_s5_b_im[0],
        e_s5_c_re[0], e_s5_c_im[0])
    lamr, lami = row2(e_s5_lam_re[0]), row2(e_s5_lam_im[0])
    ldt = row2(jnp.repeat(e_s5_log_dt[0], S5_STATE))
    s5_consts = (lamr, lami, ldt, wb, wcr, wci, row2(e_s5_d[0]), e_s5_w_glu[0].astype(BF16),
                 row2(e_s5_b_glu[0]))
    wout_s5 = e_w_out[0][:S5_WIDTH].astype(BF16)
    wout_mla = e_w_out[0][S5_WIDTH:].astype(BF16)
    owin = o_w_in[0].astype(BF16)
    owout = o_w_out[0].astype(BF16)
    ffn = [(row2(norm_ffn[l]), f_w_gate[l].astype(BF16), f_w_up[l].astype(BF16), f_conv_w[l],
            row2(f_conv_b[l]), f_w_down[l].astype(BF16)) for l in range(2)]
    nm0, nm1 = row2(norm_mix[0]), row2(norm_mix[1])
    qnorm, kvnorm = row2(e_q_norm[0]), row2(e_kv_norm[0])
    fnorm = row2(final_norm)
    sinks = o_sinks[0]

    xp = x_prompt.reshape(n_p * t_p, D_MODEL)
    pos_p = np.arange(t_p)
    u, qt_p, ckv_p, kr_p, k_big, vt_p = _even_proj(
        xp, _mla_rope_tables(pos_p), nm0, win, qnorm, wuq.T, kvnorm, wuk, wuv.T, tm=TM_EVEN_PROJ,
        n_seq=n_p,
        with_kv=True)
    zeros_state = jnp.zeros((n_p, S5_NSTATE), F32)
    y_s5, p_hr, p_hi = _s5(u.reshape(n_p, t_p, S5_WIDTH), zeros_state, zeros_state, *s5_consts,
                           nb=n_p, tt=S5_TT, transpose_io=True)
    o_mla = _mla_attn(qt_p, k_big.reshape(n_p, t_p, -1), vt_p)
    x1, conv0 = _post(xp, [y_s5.reshape(n_p * t_p, -1), o_mla.reshape(n_p * t_p, -1)],
                      [wout_s5, wout_mla], *ffn[0], n_seq=n_p, tm=TM_POST)
    qt1, k1, vt1, k_tail, v_tail = _odd_proj_t(x1, _swa_rope_tables(pos_p), nm1, owin.T,
                                               tm=TM_ODD_PROJ,
                                               n_seq=n_p)
    kvw = SWA_KV_HEADS * SWA_HD
    o_swa = _swa_attn(sinks, qt1, k1.reshape(n_p, t_p, kvw), vt1)
    y_p, conv1 = _post(x1, [o_swa.reshape(n_p * t_p, -1)], [owout], *ffn[1], n_seq=n_p, tm=TM_POST,
                       final_norm=fnorm)

    y_prompt = y_p.reshape(n_p, t_p, D_MODEL)
    p_s5_re = p_hr.reshape(1, n_p, S5_GROUPS, S5_STATE)
    p_s5_im = p_hi.reshape(1, n_p, S5_GROUPS, S5_STATE)
    p_ckv = ckv_p.reshape(1, n_p, t_p, MLA_KV_LORA)
    p_krope = kr_p.reshape(1, n_p, t_p, MLA_ROPE)
    assert t_p >= WINDOW
    p_swa_k = k_tail.reshape(1, n_p, WINDOW, SWA_KV_HEADS, SWA_HD)
    p_swa_v = v_tail.reshape(1, n_p, WINDOW, SWA_KV_HEADS, SWA_HD)
    p_conv = jnp.stack([conv0[:, SUBLANE - 2:], conv1[:, SUBLANE - 2:]])

    xs = x_sample.reshape(n_s, D_MODEL)
    pos_s = np.full((n_s,), past_len)
    u_s, qs_big, ckv_s, kr_s = _even_proj(
        xs, _mla_rope_tables(pos_s), nm0, win, qnorm, wuq, kvnorm, wuk, wuv, tm=n_s, n_seq=n_s,
        with_kv=False)
    ys_s5, s_hr, s_hi = _s5(u_s.reshape(1, n_s, S5_WIDTH), state_s5_re[0].reshape(n_s, S5_NSTATE),
                            state_s5_im[0].reshape(n_s, S5_NSTATE), *s5_consts,
                            nb=n_s, tt=1, transpose_io=False)
    q_lat = _qlat(qs_big, wk_t).reshape(n_s, MLA_HEADS, MLA_KV_LORA)
    q_rope = qs_big.reshape(n_s, MLA_HEADS, HEAD_PAD)[:, :, MLA_NOPE:MLA_NOPE + MLA_ROPE]
    o_lat = _mla_decode(page_table, q_lat, q_rope, ckv_s.reshape(n_s, 1, MLA_KV_LORA),
                        kr_s.reshape(n_s, 1, MLA_ROPE), cache_mla_ckv,
                        jnp.swapaxes(cache_mla_krope, 2, 3))
    os_mla = _olat(o_lat.reshape(n_s, MLA_HEADS * MLA_KV_LORA), wv_big)
    xs1, g0 = _post(xs, [ys_s5.reshape(n_s, -1), os_mla], [wout_s5, wout_mla], *ffn[0],
                    n_seq=n_s, tm=n_s, conv_bufs=(state_ffn_conv[0, :, 0], state_ffn_conv[0, :, 1]))
    qs1, ks1, vs1 = _odd_proj(xs1, _swa_rope_tables(pos_s), nm1, owin, tm=n_s)
    expand = lambda a: jnp.repeat(a.reshape(n_s, SWA_KV_HEADS, SWA_HD), SWA_GQ, axis=1)
    os_swa = _swa_decode(sinks, qs1.reshape(n_s, SWA_HEADS, SWA_HD), expand(ks1), expand(vs1),
                         jnp.transpose(cache_swa_k[0], (0, 2, 3, 1)),
                         jnp.transpose(cache_swa_v[0], (0, 2, 3, 1)))
    ys, g1 = _post(xs1, [os_swa.reshape(n_s, -1)], [owout], *ffn[1], n_seq=n_s, tm=n_s,
                   conv_bufs=(state_ffn_conv[1, :, 0], state_ffn_conv[1, :, 1]), final_norm=fnorm)

    y_sample = ys.reshape(n_s, 1, D_MODEL)
    s_s5_re = s_hr.reshape(1, n_s, S5_GROUPS, S5_STATE)
    s_s5_im = s_hi.reshape(1, n_s, S5_GROUPS, S5_STATE)
    s_ckv = ckv_s.reshape(1, n_s, 1, MLA_KV_LORA)
    s_krope = kr_s.reshape(1, n_s, 1, MLA_ROPE)
    s_swa_k = ks1.reshape(1, n_s, 1, SWA_KV_HEADS, SWA_HD)
    s_swa_v = vs1.reshape(1, n_s, 1, SWA_KV_HEADS, SWA_HD)
    s_conv = jnp.stack([jnp.stack([state_ffn_conv[0, :, 1], g0], axis=1),
                        jnp.stack([state_ffn_conv[1, :, 1], g1], axis=1)])

    return (y_prompt, y_sample, p_s5_re, p_s5_im, p_ckv, p_krope, p_swa_k, p_swa_v, p_conv,
            s_s5_re, s_s5_im, s_ckv, s_krope, s_swa_k, s_swa_v, s_conv)
```

```python
import functools
import math

import numpy as np
import jax
import jax.numpy as jnp
from jax import lax
from jax.experimental import pallas as pl
from jax.experimental.pallas import tpu as pltpu

F32 = jnp.float32
BF16 = jnp.bfloat16

D_MODEL = 1024
S5_WIDTH = 512
S5_GROUPS = 32
S5_GROUP = 16
S5_STATE = 64
S5_NSTATE = S5_GROUPS * S5_STATE
MLA_HEADS = 16
MLA_NOPE = 64
MLA_ROPE = 32
MLA_V = 64
MLA_Q_LORA = 384
MLA_KV_LORA = 256
MLA_THETA = 10000.0
MLA_SCALE = 1.0 / math.sqrt(MLA_NOPE + MLA_ROPE)
SWA_HEADS = 16
SWA_KV_HEADS = 4
SWA_HD = 64
SWA_GQ = SWA_HEADS // SWA_KV_HEADS
WINDOW = 128
ROT_DIM = SWA_HD // 4
ROPE_THETA = 500000.0
SWA_SCALE = 1.0 / math.sqrt(SWA_HD)
D_FF = 2816
PAGE = 128
EPS = 1e-6
NEG = -1e30
LOG2E = math.log2(math.e)

LANE = 128
SUBLANE = 8
HEAD_PAD = LANE
Z_EVEN = S5_WIDTH + MLA_Q_LORA + MLA_KV_LORA + LANE
QK_ODD = (SWA_HEADS + SWA_KV_HEADS) * SWA_HD
FF_CHUNK = 256
VMEM_LIMIT = 48 * 1024 * 1024

TM_EVEN_PROJ = 512
TM_ODD_PROJ = 512
TM_POST = 512
MLA_TQ = 256
S5_TT = 128


def _cparams(sem):
    return pltpu.CompilerParams(dimension_semantics=sem, vmem_limit_bytes=VMEM_LIMIT)


def _rms(x, g):
    return x * lax.rsqrt(jnp.mean(x * x, axis=-1, keepdims=True) + EPS) * g


def _const_spec(shape):
    nd = len(shape)
    return pl.BlockSpec(shape, lambda *_: (0,) * nd)


def _single_spec(shape):
    nd = len(shape)
    return pl.BlockSpec(shape, lambda *_: (0,) * nd, pipeline_mode=pl.Buffered(1))


def _layer_spec(shape, layer):
    tail = (0,) * (len(shape) - 1)
    return pl.BlockSpec((None,) + tuple(shape[1:]), lambda *_: (layer,) + tail,
                        pipeline_mode=pl.Buffered(1))


def _mla_rope_tables(pos):
    half = MLA_ROPE // 2
    inv = MLA_THETA ** (-np.arange(half, dtype=np.float64) * 2.0 / MLA_ROPE)
    ang = np.asarray(pos, np.float64)[:, None] * inv[None, :]
    cos, sin = np.cos(ang), np.sin(ang)
    p = ang.shape[0]
    c = np.zeros((p, LANE)); sa = np.zeros((p, LANE)); sb = np.zeros((p, LANE))
    c[:, :MLA_NOPE] = 1.0
    c[:, MLA_NOPE:MLA_NOPE + half] = cos
    c[:, MLA_NOPE + half:MLA_NOPE + 2 * half] = cos
    sa[:, MLA_NOPE + half:MLA_NOPE + 2 * half] = sin
    sb[:, MLA_NOPE:MLA_NOPE + half] = -sin
    return tuple(jnp.asarray(t, F32) for t in (c, sa, sb))


def _swa_rope_tables(pos):
    half = ROT_DIM // 2
    inv = ROPE_THETA ** (-np.arange(half, dtype=np.float64) * 2.0 / ROT_DIM)
    ang = np.asarray(pos, np.float64)[:, None] * inv[None, :]
    cos, sin = np.cos(ang), np.sin(ang)
    p = ang.shape[0]
    c = np.ones((p, LANE)); sa = np.zeros((p, LANE)); sb = np.zeros((p, LANE))
    for o in (0, SWA_HD):
        c[:, o:o + half] = cos
        c[:, o + half:o + 2 * half] = cos
        sa[:, o + half:o + 2 * half] = sin
        sb[:, o:o + half] = -sin
    return tuple(jnp.asarray(t, F32) for t in (c, sa, sb))


def _rope_lanes(x, c, sa, sb, half):
    width = x.shape[1]
    reps = width // LANE
    if reps > 1:
        c = jnp.concatenate([c] * reps, axis=1)
        sa = jnp.concatenate([sa] * reps, axis=1)
        sb = jnp.concatenate([sb] * reps, axis=1)
    return x * c + pltpu.roll(x, half, 1) * sa + pltpu.roll(x, width - half, 1) * sb


def _rope_sublanes(xt, ct, sat, sbt, half):
    feats = xt.shape[0]
    reps = feats // LANE
    ct = jnp.concatenate([ct] * reps, axis=0)
    sat = jnp.concatenate([sat] * reps, axis=0)
    sbt = jnp.concatenate([sbt] * reps, axis=0)
    return xt * ct + pltpu.roll(xt, half, 0) * sat + pltpu.roll(xt, feats - half, 0) * sbt


def _even_proj_kernel(x_ref, c_ref, sa_ref, sb_ref, ct_ref, sat_ref, sbt_ref, nm_ref, win_ref, qn_ref,
                      wuq_ref, kvn_ref, wuk_ref, wuv_ref, u_ref, q_ref, ckv_ref, kr_ref, *kv_refs):
    h = _rms(x_ref[...], nm_ref[...]).astype(BF16)
    z = jnp.dot(h, win_ref[...], preferred_element_type=F32)
    u_ref[...] = z[:, :S5_WIDTH]
    c, sa, sb = c_ref[...], sa_ref[...], sb_ref[...]
    o1 = S5_WIDTH + MLA_Q_LORA
    o2 = o1 + MLA_KV_LORA
    qn = _rms(z[:, S5_WIDTH:o1], qn_ref[...]).astype(BF16)
    ckv = _rms(z[:, o1:o2], kvn_ref[...])
    ckv_ref[...] = ckv
    kr = _rope_lanes(z[:, o2:], c, sa, sb, MLA_ROPE // 2)
    kr_ref[...] = kr[:, MLA_NOPE:MLA_NOPE + MLA_ROPE]
    nt_dims = (((1,), (1,)), ((), ()))
    if kv_refs:
        k_ref, vt_ref = kv_refs
        qt = lax.dot_general(wuq_ref[...], qn, nt_dims, preferred_element_type=F32)
        qt = _rope_sublanes(qt, ct_ref[...], sat_ref[...], sbt_ref[...], MLA_ROPE // 2)
        tm = qt.shape[1]
        for j in range(tm // MLA_TQ):
            q_ref[0, j] = qt[:, j * MLA_TQ:(j + 1) * MLA_TQ].astype(BF16)
        ckvb = ckv.astype(BF16)
        k = jnp.dot(ckvb, wuk_ref[...], preferred_element_type=F32)
        k_ref[...] = (k + jnp.concatenate([kr] * MLA_HEADS, axis=1)).astype(BF16)
        vt = lax.dot_general(wuv_ref[...], ckvb, nt_dims, preferred_element_type=F32)
        for j in range(tm // MLA_TQ):
            vt_ref[0, j] = vt[:, j * MLA_TQ:(j + 1) * MLA_TQ].astype(BF16)
    else:
        q = jnp.dot(qn, wuq_ref[...], preferred_element_type=F32)
        q_ref[...] = _rope_lanes(q, c, sa, sb, MLA_ROPE // 2).astype(BF16)


def _even_proj(x, tabs, nm, win, qnorm, wuq, kvnorm, wuk, wuv, *, tm, n_seq, with_kv):
    rows = x.shape[0]
    nt = tabs[0].shape[0] // tm
    tabs_t = tuple(t.T for t in tabs)
    row = lambda w: pl.BlockSpec((tm, w), lambda i: (i, 0))
    tab = pl.BlockSpec((tm, LANE), lambda i: (i % nt, 0))
    tab_t = pl.BlockSpec((LANE, tm), lambda i: (0, i % nt))
    tq = min(tm, MLA_TQ)
    tblock = lambda w: pl.BlockSpec((1, tm // tq, w, tq), lambda i: (i // nt, i % nt, 0, 0))
    qw = MLA_HEADS * HEAD_PAD
    vw = MLA_HEADS * MLA_V
    if with_kv:
        q_shape, q_spec = jax.ShapeDtypeStruct((n_seq, nt * tm // tq, qw, tq), BF16), tblock(qw)
    else:
        q_shape, q_spec = jax.ShapeDtypeStruct((rows, qw), BF16), row(qw)
    out_shape = [jax.ShapeDtypeStruct((rows, S5_WIDTH), F32), q_shape,
                 jax.ShapeDtypeStruct((rows, MLA_KV_LORA), F32),
                 jax.ShapeDtypeStruct((rows, MLA_ROPE), F32)]
    out_specs = [row(S5_WIDTH), q_spec, row(MLA_KV_LORA), row(MLA_ROPE)]
    if with_kv:
        out_shape += [jax.ShapeDtypeStruct((rows, qw), BF16),
                      jax.ShapeDtypeStruct((n_seq, nt * tm // tq, vw, tq), BF16)]
        out_specs += [row(qw), tblock(vw)]
    return pl.pallas_call(
        _even_proj_kernel,
        grid=(rows // tm,),
        in_specs=[row(D_MODEL), tab, tab, tab, tab_t, tab_t, tab_t, _const_spec(nm.shape),
                  _const_spec(win.shape), _const_spec(qnorm.shape), _const_spec(wuq.shape),
                  _const_spec(kvnorm.shape), _const_spec(wuk.shape), _const_spec(wuv.shape)],
        out_specs=out_specs,
        out_shape=out_shape,
        compiler_params=_cparams(("arbitrary",)),
        name="even_proj",
    )(x, *tabs, *tabs_t, nm, win, qnorm, wuq, kvnorm, wuk, wuv)


S5_STRIP = 512


def _s5_kernel(u_ref, h0r_ref, h0i_ref, lamr_ref, lami_ref, ldt_ref, wb_ref, wcr_ref, wci_ref,
               d_ref, wglu_ref, bglu_ref, y_ref, hr_ref, hi_ref, ut_ref, xr_ref, xi_ref, yt_ref,
               *, nb, tt, transpose_io):
    @pl.when(pl.program_id(0) == 0)
    def _():
        xr_ref[0:nb, :] = h0r_ref[...]
        xi_ref[0:nb, :] = h0i_ref[...]

    if transpose_io:
        for t in range(tt):
            ut_ref[t * nb:(t + 1) * nb, :] = u_ref[:, t, :]
    else:
        ut_ref[...] = u_ref[0]
    ub = ut_ref[...].astype(BF16)

    lr = jnp.minimum(lamr_ref[...], -1e-4)
    li = lami_ref[...]
    dt = jnp.exp(ldt_ref[...])
    mag = jnp.exp(lr * dt)
    ar = mag * jnp.cos(li * dt)
    ai = mag * jnp.sin(li * dt)
    den = lr * lr + li * li
    zr = ((ar - 1.0) * lr + ai * li) / den
    zi = (ai * lr - (ar - 1.0) * li) / den

    ucols = S5_STRIP // S5_STATE * S5_GROUP
    ys = []
    for s in range(S5_NSTATE // S5_STRIP):
        cols = slice(s * S5_STRIP, (s + 1) * S5_STRIP)
        us = ub[:, s * ucols:(s + 1) * ucols]
        br = jnp.dot(us, wb_ref[s, :, :S5_STRIP], preferred_element_type=F32)
        bi = jnp.dot(us, wb_ref[s, :, S5_STRIP:], preferred_element_type=F32)
        zrs, zis = zr[:, cols], zi[:, cols]
        xr_ref[nb:, cols] = zrs * br - zis * bi
        xi_ref[nb:, cols] = zrs * bi + zis * br
        ars = jnp.broadcast_to(ar[:, cols], (nb, S5_STRIP))
        ais = jnp.broadcast_to(ai[:, cols], (nb, S5_STRIP))
        hr, hi = xr_ref[0:nb, cols], xi_ref[0:nb, cols]
        for t in range(tt):
            r = slice((t + 1) * nb, (t + 2) * nb)
            hr, hi = (ars * hr - ais * hi + xr_ref[r, cols],
                      ars * hi + ais * hr + xi_ref[r, cols])
            xr_ref[r, cols] = hr
            xi_ref[r, cols] = hi
        xr_ref[0:nb, cols] = hr
        xi_ref[0:nb, cols] = hi
        ys.append(jnp.dot(xr_ref[nb:, cols].astype(BF16), wcr_ref[s], preferred_element_type=F32)
                  + jnp.dot(xi_ref[nb:, cols].astype(BF16), wci_ref[s],
                            preferred_element_type=F32))

    hr_ref[...] = xr_ref[0:nb, :]
    hi_ref[...] = xi_ref[0:nb, :]
    y = jnp.concatenate(ys, axis=1) + d_ref[...] * ut_ref[...]
    y = jax.nn.gelu(y)
    gate = jnp.dot(y.astype(BF16), wglu_ref[...], preferred_element_type=F32) + bglu_ref[...]
    y = y * jax.nn.sigmoid(gate)
    if transpose_io:
        groups = S5_WIDTH // LANE
        for c in range(groups):
            yt_ref[c] = y[:, c * LANE:(c + 1) * LANE]
        for n in range(nb):
            y_ref[n] = jnp.concatenate(
                [yt_ref[c, pl.ds(n, tt, stride=nb), :] for c in range(groups)],
                axis=1).astype(y_ref.dtype)
    else:
        y_ref[0] = y.astype(y_ref.dtype)


def _s5(u, h0r, h0i, lamr, lami, ldt, wb, wcr, wci, d, wglu, bglu, *, nb, tt, transpose_io):
    t_total = u.shape[1] if transpose_io else u.shape[0]
    if transpose_io:
        u_spec = pl.BlockSpec((nb, tt, S5_WIDTH), lambda i: (0, i, 0))
        y_shape = (nb, t_total, S5_WIDTH)
    else:
        u_spec = pl.BlockSpec((tt, nb, S5_WIDTH), lambda i: (i, 0, 0))
        y_shape = (t_total, nb, S5_WIDTH)
    consts = (h0r, h0i, lamr, lami, ldt, wb, wcr, wci, d, wglu, bglu)
    kern = functools.partial(_s5_kernel, nb=nb, tt=tt, transpose_io=transpose_io)
    return pl.pallas_call(
        kern,
        grid=(t_total // tt,),
        in_specs=[u_spec] + [_const_spec(c.shape) for c in consts],
        out_specs=[u_spec, _const_spec((nb, S5_NSTATE)), _const_spec((nb, S5_NSTATE))],
        out_shape=[jax.ShapeDtypeStruct(y_shape, BF16),
                   jax.ShapeDtypeStruct((nb, S5_NSTATE), F32),
                   jax.ShapeDtypeStruct((nb, S5_NSTATE), F32)],
        scratch_shapes=[pltpu.VMEM((nb * tt, S5_WIDTH), F32),
                        pltpu.VMEM((nb * (tt + 1), S5_NSTATE), F32),
                        pltpu.VMEM((nb * (tt + 1), S5_NSTATE), F32),
                        pltpu.VMEM((S5_WIDTH // LANE, nb * tt, LANE), F32)],
        compiler_params=_cparams(("arbitrary",)),
        name="s5_mixer",
    )(u, *consts)


MLA_LOOKAHEAD = 4


def _mla_attn_kernel(qt_ref, k_ref, vt_ref, o_ref, m_ref, l_ref, acc_ref, *, tq):
    qi = pl.program_id(1)
    m_ref[...] = jnp.full(m_ref.shape, NEG, F32)
    l_ref[...] = jnp.zeros(l_ref.shape, F32)
    acc_ref[...] = jnp.zeros(acc_ref.shape, F32)

    def causal_mask(nblk):
        key = lax.broadcasted_iota(jnp.int32, (nblk * tq, tq), 0)
        qry = lax.broadcasted_iota(jnp.int32, (nblk * tq, tq), 1)
        return key <= qry + (nblk - 1) * tq

    def process(kb0, nblk, mask):
        keys = pl.ds(pl.multiple_of(kb0 * tq, tq), nblk * tq)

        def scores(h):
            hl = slice(h * HEAD_PAD, (h + 1) * HEAD_PAD)
            return jnp.dot(k_ref[0, keys, hl], qt_ref[0, 0, hl, :],
                           preferred_element_type=F32)

        pending = [scores(h) for h in range(MLA_LOOKAHEAD)]
        for h in range(MLA_HEADS):
            st = pending.pop(0) * (MLA_SCALE * LOG2E)
            if h + MLA_LOOKAHEAD < MLA_HEADS:
                pending.append(scores(h + MLA_LOOKAHEAD))
            if mask is not None:
                st = jnp.where(mask, st, NEG)
            j = h // 2
            vts = [vt_ref[0, kb0 + i, j * LANE:(j + 1) * LANE, :] for i in range(nblk)]
            vt = vts[0] if nblk == 1 else jnp.concatenate(vts, axis=1)
            m_old = m_ref[h]
            m_new = jnp.maximum(m_old, jnp.max(st, axis=0, keepdims=True))
            alpha = jnp.exp2(m_old - m_new)
            pt = jnp.exp2(st - m_new)
            l_ref[h] = alpha * l_ref[h] + jnp.sum(pt, axis=0, keepdims=True)
            m_ref[h] = m_new
            acc_ref[h] = alpha * acc_ref[h] + jnp.dot(vt, pt.astype(BF16),
                                                      preferred_element_type=F32)

    n_wide = qi // 2

    def wide_step(i, carry):
        process(i * 2, 2, None)
        return carry

    lax.fori_loop(0, n_wide, wide_step, 0)

    @pl.when(qi % 2 == 0)
    def _():
        process(qi, 1, causal_mask(1))

    @pl.when(qi % 2 == 1)
    def _():
        process(qi - 1, 2, causal_mask(2))

    half = lax.broadcasted_iota(jnp.int32, (LANE, tq), 0) < MLA_V
    for j in range(MLA_HEADS // 2):
        ot = jnp.where(half, acc_ref[2 * j] / l_ref[2 * j], acc_ref[2 * j + 1] / l_ref[2 * j + 1])
        o_ref[0, :, j * LANE:(j + 1) * LANE] = ot.T.astype(o_ref.dtype)


def _mla_attn(qt, k, vt):
    n, nblocks, qw, tq = qt.shape
    t = k.shape[1]
    kern = functools.partial(_mla_attn_kernel, tq=tq)
    return pl.pallas_call(
        kern,
        grid=(n, nblocks),
        in_specs=[pl.BlockSpec((1, 1, qw, tq), lambda b, i: (b, i, 0, 0)),
                  pl.BlockSpec((1, t, k.shape[2]), lambda b, i: (b, 0, 0)),
                  pl.BlockSpec((1, nblocks, vt.shape[2], tq), lambda b, i: (b, 0, 0, 0))],
        out_specs=pl.BlockSpec((1, tq, MLA_HEADS * MLA_V), lambda b, i: (b, i, 0)),
        out_shape=jax.ShapeDtypeStruct((n, t, MLA_HEADS * MLA_V), BF16),
        scratch_shapes=[pltpu.VMEM((MLA_HEADS, 1, tq), F32),
                        pltpu.VMEM((MLA_HEADS, 1, tq), F32),
                        pltpu.VMEM((MLA_HEADS, LANE, tq), F32)],
        compiler_params=_cparams(("arbitrary", "arbitrary")),
        name="mla_prompt_attn",
    )(qt, k, vt)


DEC_PAGES = 32
DEC_SUB = 4
DEC_AHEAD = 2
DEC_SLOTS = DEC_AHEAD + 1
DEC_KEYS = DEC_PAGES * PAGE


def _qlat_kernel(q_ref, wk_ref, o_ref):
    for h in range(MLA_HEADS):
        o_ref[:, h * MLA_KV_LORA:(h + 1) * MLA_KV_LORA] = jnp.dot(
            q_ref[:, h * HEAD_PAD:(h + 1) * HEAD_PAD], wk_ref[h], preferred_element_type=F32)


def _qlat(q_big, wk_t):
    rows = q_big.shape[0]
    return pl.pallas_call(
        _qlat_kernel,
        grid=(1,),
        in_specs=[_const_spec(q_big.shape), _const_spec(wk_t.shape)],
        out_specs=_const_spec((rows, MLA_HEADS * MLA_KV_LORA)),
        out_shape=jax.ShapeDtypeStruct((rows, MLA_HEADS * MLA_KV_LORA), F32),
        compiler_params=_cparams(("arbitrary",)),
        name="mla_q_absorb",
    )(q_big, wk_t)


def _olat_kernel(o_ref, wv_ref, out_ref):
    for j in range(MLA_HEADS // 2):
        acc = None
        for h in (2 * j, 2 * j + 1):
            part = jnp.dot(o_ref[:, h * MLA_KV_LORA:(h + 1) * MLA_KV_LORA].astype(BF16),
                           wv_ref[:, h * LANE:(h + 1) * LANE], preferred_element_type=F32)
            acc = part if acc is None else acc + part
        out_ref[:, j * LANE:(j + 1) * LANE] = acc.astype(out_ref.dtype)


def _olat(o_lat, wv_big):
    rows = o_lat.shape[0]
    return pl.pallas_call(
        _olat_kernel,
        grid=(1,),
        in_specs=[_const_spec(o_lat.shape), _const_spec(wv_big.shape)],
        out_specs=_const_spec((rows, MLA_HEADS * MLA_V)),
        out_shape=jax.ShapeDtypeStruct((rows, MLA_HEADS * MLA_V), BF16),
        compiler_params=_cparams(("arbitrary",)),
        name="mla_v_expand",
    )(o_lat, wv_big)


def _page_copies(cache_ckv, cache_kr, ckv_buf, kr_buf, sem, slot, page, p):
    rows = pl.ds(p * PAGE, PAGE)
    return (pltpu.make_async_copy(cache_ckv.at[0, page], ckv_buf.at[slot, rows], sem.at[0, slot]),
            pltpu.make_async_copy(cache_kr.at[0, page], kr_buf.at[slot, p], sem.at[1, slot]))


def _mla_decode_kernel(ptc_ref, ptn_ref, ql_ref, qr_ref, ckvn_ref, krn_ref, cache_ckv, cache_kr,
                       o_ref, ckv_buf, kr_buf, sem, *, n_seq, n_chunks):
    b = pl.program_id(0)
    g0 = b * n_chunks

    def start_page(page, slot, p):
        for cp in _page_copies(cache_ckv, cache_kr, ckv_buf, kr_buf, sem, slot, page, p):
            cp.start()

    def wait_chunk(slot):
        for p in range(DEC_PAGES):
            for cp in _page_copies(cache_ckv, cache_kr, ckv_buf, kr_buf, sem, slot, 0, p):
                cp.wait()

    @pl.when(b == 0)
    def _():
        for a in range(DEC_AHEAD):
            for p in range(DEC_PAGES):
                start_page(ptc_ref[0, 0, a * DEC_PAGES + p], a, p)

    ql = ql_ref[0].astype(BF16)
    qr = qr_ref[0].astype(BF16)

    def chunk_body(c, carry):
        m, l, acc = carry
        slot = lax.rem(g0 + c, DEC_SLOTS)
        ahead_slot = lax.rem(g0 + c + DEC_AHEAD, DEC_SLOTS)
        wait_chunk(slot)
        wraps = c + DEC_AHEAD >= n_chunks
        ahead_base = jnp.where(wraps, c + DEC_AHEAD - n_chunks, c + DEC_AHEAD) * DEC_PAGES

        def ahead_page(p):
            return jnp.where(wraps, ptn_ref[0, 0, ahead_base + p], ptc_ref[0, 0, ahead_base + p])

        kbs, scores = [], []
        for i in range(DEC_PAGES // DEC_SUB):
            kb = ckv_buf[slot, i * DEC_SUB * PAGE:(i + 1) * DEC_SUB * PAGE, :].astype(BF16)
            krt = jnp.concatenate([kr_buf[slot, i * DEC_SUB + p] for p in range(DEC_SUB)],
                                  axis=1).astype(BF16)
            kbs.append(kb)
            scores.append(
                (lax.dot_general(ql, kb, (((1,), (1,)), ((), ())), preferred_element_type=F32)
                 + jnp.dot(qr, krt, preferred_element_type=F32)) * MLA_SCALE)
            for p in range(i * DEC_SUB, (i + 1) * DEC_SUB):
                start_page(ahead_page(p), ahead_slot, p)
        for kb, s in zip(kbs, scores):
            m_new = jnp.maximum(m, jnp.max(s, axis=1, keepdims=True))
            alpha = jnp.exp(m - m_new)
            pr = jnp.exp(s - m_new)
            l = alpha * l + jnp.sum(pr, axis=1, keepdims=True)
            acc = alpha * acc + jnp.dot(pr.astype(BF16), kb, preferred_element_type=F32)
            m = m_new
        return m, l, acc

    init = (jnp.full((MLA_HEADS, 1), NEG, F32), jnp.zeros((MLA_HEADS, 1), F32),
            jnp.zeros((MLA_HEADS, MLA_KV_LORA), F32))
    m, l, acc = lax.fori_loop(0, n_chunks, chunk_body, init)

    @pl.when(b == n_seq - 1)
    def _():
        for a in range(DEC_AHEAD):
            wait_chunk((n_seq * n_chunks + a) % DEC_SLOTS)

    cn = ckvn_ref[0].astype(BF16).astype(F32)
    kn = krn_ref[0].astype(BF16).astype(F32)
    s_new = (jnp.sum(ql.astype(F32) * cn, axis=1, keepdims=True)
             + jnp.sum(qr.astype(F32) * kn, axis=1, keepdims=True)) * MLA_SCALE
    m_new = jnp.maximum(m, s_new)
    alpha = jnp.exp(m - m_new)
    p_new = jnp.exp(s_new - m_new)
    l = alpha * l + p_new
    acc = alpha * acc + p_new.astype(BF16).astype(F32) * cn
    o_ref[0] = acc / l


def _mla_decode(page_table, q_lat, q_rope, ckv_new, kr_new, cache_ckv, cache_kr):
    n_seq, n_pages = page_table.shape
    n_chunks = n_pages // DEC_PAGES
    assert n_pages % DEC_PAGES == 0 and n_chunks >= DEC_AHEAD
    pt = page_table.reshape(n_seq, 1, n_pages)
    smem_row = lambda f: pl.BlockSpec((1, 1, n_pages), f, memory_space=pltpu.SMEM)
    per_seq = lambda a: pl.BlockSpec((1,) + a.shape[1:], lambda b: (b, 0, 0))
    kern = functools.partial(_mla_decode_kernel, n_seq=n_seq, n_chunks=n_chunks)
    return pl.pallas_call(
        kern,
        grid=(n_seq,),
        in_specs=[smem_row(lambda b: (b, 0, 0)),
                  smem_row(lambda b: (jnp.minimum(b + 1, n_seq - 1), 0, 0)),
                  per_seq(q_lat), per_seq(q_rope), per_seq(ckv_new), per_seq(kr_new),
                  pl.BlockSpec(memory_space=pl.ANY), pl.BlockSpec(memory_space=pl.ANY)],
        out_specs=pl.BlockSpec((1, MLA_HEADS, MLA_KV_LORA), lambda b: (b, 0, 0)),
        out_shape=jax.ShapeDtypeStruct((n_seq, MLA_HEADS, MLA_KV_LORA), F32),
        scratch_shapes=[pltpu.VMEM((DEC_SLOTS, DEC_KEYS, MLA_KV_LORA), F32),
                        pltpu.VMEM((DEC_SLOTS, DEC_PAGES, MLA_ROPE, PAGE), F32),
                        pltpu.SemaphoreType.DMA((2, DEC_SLOTS))],
        compiler_params=_cparams(("arbitrary",)),
        name="mla_paged_decode",
    )(pt, pt, q_lat, q_rope, ckv_new, kr_new, cache_ckv, cache_kr)


def _odd_proj_kernel(x_ref, c_ref, sa_ref, sb_ref, nm_ref, win_ref, q_ref, k_ref, v_ref):
    h = _rms(x_ref[...], nm_ref[...]).astype(BF16)
    z = jnp.dot(h, win_ref[...], preferred_element_type=F32)
    qk = _rope_lanes(z[:, :QK_ODD], c_ref[...], sa_ref[...], sb_ref[...], ROT_DIM // 2)
    q_ref[...] = qk[:, :SWA_HEADS * SWA_HD].astype(BF16)
    k_ref[...] = qk[:, SWA_HEADS * SWA_HD:]
    v_ref[...] = z[:, QK_ODD:]


def _odd_proj(x, tabs, nm, win, *, tm):
    rows = x.shape[0]
    nt = tabs[0].shape[0] // tm
    row = lambda w: pl.BlockSpec((tm, w), lambda i: (i, 0))
    tab = pl.BlockSpec((tm, LANE), lambda i: (i % nt, 0))
    kvw = SWA_KV_HEADS * SWA_HD
    return pl.pallas_call(
        _odd_proj_kernel,
        grid=(rows // tm,),
        in_specs=[row(D_MODEL), tab, tab, tab, _const_spec(nm.shape), _const_spec(win.shape)],
        out_specs=[row(SWA_HEADS * SWA_HD), row(kvw), row(kvw)],
        out_shape=[jax.ShapeDtypeStruct((rows, SWA_HEADS * SWA_HD), BF16),
                   jax.ShapeDtypeStruct((rows, kvw), F32),
                   jax.ShapeDtypeStruct((rows, kvw), F32)],
        compiler_params=_cparams(("arbitrary",)),
        name="odd_proj",
    )(x, *tabs, nm, win)


def _odd_proj_t_kernel(x_ref, ct_ref, sat_ref, sbt_ref, nm_ref, wt_ref, qt_ref, k_ref, vt_ref,
                       ktail_ref, vtail_ref, *, tm):
    h = _rms(x_ref[...], nm_ref[...]).astype(BF16)
    zt = lax.dot_general(wt_ref[...], h, (((1,), (1,)), ((), ())),
                         preferred_element_type=F32)
    qkt = _rope_sublanes(zt[:QK_ODD], ct_ref[...], sat_ref[...], sbt_ref[...], ROT_DIM // 2)
    nq = SWA_HEADS * SWA_HD
    for i in range(tm // WINDOW):
        cols = slice(i * WINDOW, (i + 1) * WINDOW)
        qt_ref[0, i] = qkt[:nq, cols].astype(BF16)
        vt_ref[0, i] = zt[QK_ODD:, cols].astype(BF16)
    k = qkt[nq:].T
    k_ref[...] = k.astype(BF16)
    ktail_ref[0] = k[tm - WINDOW:]
    vtail_ref[0] = zt[QK_ODD:, tm - WINDOW:].T


def _odd_proj_t(x, tabs, nm, win_t, *, tm, n_seq):
    rows = x.shape[0]
    t = rows // n_seq
    nt = t // tm
    nb = tm // WINDOW
    kvw = SWA_KV_HEADS * SWA_HD
    qw = SWA_HEADS * SWA_HD
    tabs_t = tuple(a.T for a in tabs)
    row = lambda w: pl.BlockSpec((tm, w), lambda i: (i, 0))
    tab_t = pl.BlockSpec((LANE, tm), lambda i: (0, i % nt))
    tblock = lambda w: pl.BlockSpec((1, nb, w, WINDOW), lambda i: (i // nt, i % nt, 0, 0))
    tail = pl.BlockSpec((1, WINDOW, kvw), lambda i: (i // nt, 0, 0))
    return pl.pallas_call(
        functools.partial(_odd_proj_t_kernel, tm=tm),
        grid=(rows // tm,),
        in_specs=[row(D_MODEL), tab_t, tab_t, tab_t, _const_spec(nm.shape),
                  _const_spec(win_t.shape)],
        out_specs=[tblock(qw), row(kvw), tblock(kvw), tail, tail],
        out_shape=[jax.ShapeDtypeStruct((n_seq, t // WINDOW, qw, WINDOW), BF16),
                   jax.ShapeDtypeStruct((rows, kvw), BF16),
                   jax.ShapeDtypeStruct((n_seq, t // WINDOW, kvw, WINDOW), BF16),
                   jax.ShapeDtypeStruct((n_seq, WINDOW, kvw), F32),
                   jax.ShapeDtypeStruct((n_seq, WINDOW, kvw), F32)],
        compiler_params=_cparams(("arbitrary",)),
        name="odd_proj_t",
    )(x, *tabs_t, nm, win_t)


SWA_QBLOCKS = 4


def _swa_attn_kernel(sink_ref, qt_ref, kp_ref, kc_ref, vtp_ref, vtc_ref, o_ref):
    step = pl.program_id(1)
    cols = SWA_GQ * WINDOW
    key = lax.broadcasted_iota(jnp.int32, (2 * WINDOW, cols), 0)
    qry = jnp.bitwise_and(lax.broadcasted_iota(jnp.int32, (2 * WINDOW, cols), 1), WINDOW - 1)
    rel = qry + WINDOW - key
    band = (rel >= 0) & (rel <= WINDOW)
    lane = lax.broadcasted_iota(jnp.int32, (1, cols), 1)
    sinks = []
    for kh in range(SWA_KV_HEADS):
        row = jnp.full((1, cols), sink_ref[kh * SWA_GQ + SWA_GQ - 1], F32)
        for g in range(SWA_GQ - 2, -1, -1):
            row = jnp.where(lane < (g + 1) * WINDOW, sink_ref[kh * SWA_GQ + g], row)
        sinks.append(row * LOG2E)
    for sub in range(SWA_QBLOCKS):
        if sub == 0:
            k_prev, vt_prev = kp_ref[0], vtp_ref[0, 0]
            valid = band & ((key >= WINDOW) | (step > 0))
        else:
            k_prev, vt_prev = kc_ref[0, (sub - 1) * WINDOW:sub * WINDOW], vtc_ref[0, sub - 1]
            valid = band
        kk = jnp.concatenate([k_prev, kc_ref[0, sub * WINDOW:(sub + 1) * WINDOW]], axis=0)
        vvt = jnp.concatenate([vt_prev, vtc_ref[0, sub]], axis=1)
        scores = []
        for kh in range(SWA_KV_HEADS):
            qg = jnp.concatenate(
                [qt_ref[0, sub, (kh * SWA_GQ + g) * SWA_HD:(kh * SWA_GQ + g + 1) * SWA_HD, :]
                 for g in range(SWA_GQ)], axis=1)
            scores.append(jnp.dot(kk[:, kh * SWA_HD:(kh + 1) * SWA_HD], qg,
                                  preferred_element_type=F32))
        outs = []
        for kh in range(SWA_KV_HEADS):
            st = jnp.where(valid, scores[kh] * (SWA_SCALE * LOG2E), NEG)
            m = jnp.maximum(jnp.max(st, axis=0, keepdims=True), sinks[kh])
            p = jnp.exp2(st - m)
            l = jnp.sum(p, axis=0, keepdims=True) + jnp.exp2(sinks[kh] - m)
            ot = jnp.dot(vvt[kh * SWA_HD:(kh + 1) * SWA_HD, :], p.astype(BF16),
                         preferred_element_type=F32) / l
            outs += [ot[:, g * WINDOW:(g + 1) * WINDOW] for g in range(SWA_GQ)]
        o_ref[0, sub * WINDOW:(sub + 1) * WINDOW, :] = (
            jnp.concatenate(outs, axis=0).T.astype(o_ref.dtype))


def _swa_attn(sinks, qt, k, vt):
    n, nblocks, qw, _ = qt.shape
    t = k.shape[1]
    kvw = SWA_KV_HEADS * SWA_HD
    qb = SWA_QBLOCKS
    before = lambda i: jnp.maximum(i * qb - 1, 0)
    return pl.pallas_call(
        _swa_attn_kernel,
        grid=(n, nblocks // qb),
        in_specs=[pl.BlockSpec(memory_space=pltpu.SMEM),
                  pl.BlockSpec((1, qb, qw, WINDOW), lambda a, i: (a, i, 0, 0)),
                  pl.BlockSpec((1, WINDOW, kvw), lambda a, i: (a, before(i), 0)),
                  pl.BlockSpec((1, qb * WINDOW, kvw), lambda a, i: (a, i, 0)),
                  pl.BlockSpec((1, 1, kvw, WINDOW), lambda a, i: (a, before(i), 0, 0)),
                  pl.BlockSpec((1, qb, kvw, WINDOW), lambda a, i: (a, i, 0, 0))],
        out_specs=pl.BlockSpec((1, qb * WINDOW, qw), lambda a, i: (a, i, 0)),
        out_shape=jax.ShapeDtypeStruct((n, t, qw), BF16),
        compiler_params=_cparams(("arbitrary", "arbitrary")),
        name="swa_prompt_attn",
    )(sinks, qt, k, k, vt, vt)


SWA_DEC_SEQS = 8


def _swa_decode_kernel(sink_ref, q_ref, kn_ref, vn_ref, ck_ref, cv_ref, o_ref):
    q = q_ref[...]
    hgrp = lax.broadcasted_iota(jnp.int32, (1, SWA_HEADS, 1), 1) // SWA_GQ
    hidx = lax.broadcasted_iota(jnp.int32, (1, SWA_HEADS, 1), 1)
    sink = jnp.zeros((1, SWA_HEADS, 1), F32)
    for h in range(SWA_HEADS):
        sink = jnp.where(hidx == h, sink_ref[h], sink)
    s = jnp.zeros((q.shape[0], SWA_HEADS, WINDOW), F32)
    for kh in range(SWA_KV_HEADS):
        skh = jnp.einsum('nhd,ndj->nhj', q, ck_ref[:, kh].astype(BF16),
                         preferred_element_type=F32)
        s = jnp.where(hgrp == kh, skh, s)
    s = s * SWA_SCALE
    qf = q.astype(F32)
    kn = kn_ref[...].astype(BF16).astype(F32)
    vn = vn_ref[...].astype(BF16).astype(F32)
    s_new = jnp.sum(qf * kn, axis=2, keepdims=True) * SWA_SCALE
    m = jnp.maximum(jnp.maximum(jnp.max(s, axis=2, keepdims=True), s_new), sink)
    p = jnp.exp(s - m)
    p_new = jnp.exp(s_new - m)
    l = jnp.sum(p, axis=2, keepdims=True) + p_new + jnp.exp(sink - m)
    o = p_new.astype(BF16).astype(F32) * vn
    for kh in range(SWA_KV_HEADS):
        pk = jnp.where(hgrp == kh, p, 0.0).astype(BF16)
        o = o + jnp.einsum('nhj,ndj->nhd', pk, cv_ref[:, kh].astype(BF16),
                           preferred_element_type=F32)
    o_ref[...] = (o / l).astype(o_ref.dtype)


def _swa_decode(sinks, q, k_new, v_new, cache_k, cache_v):
    n = q.shape[0]
    nb = SWA_DEC_SEQS
    seq = pl.BlockSpec((nb, SWA_HEADS, SWA_HD), lambda i: (i, 0, 0))
    cache = pl.BlockSpec((nb, SWA_KV_HEADS, SWA_HD, WINDOW), lambda i: (i, 0, 0, 0))
    return pl.pallas_call(
        _swa_decode_kernel,
        grid=(n // nb,),
        in_specs=[pl.BlockSpec(memory_space=pltpu.SMEM), seq, seq, seq, cache, cache],
        out_specs=seq,
        out_shape=jax.ShapeDtypeStruct((n, SWA_HEADS, SWA_HD), BF16),
        compiler_params=_cparams(("arbitrary",)),
        name="swa_decode_attn",
    )(sinks, q, k_new, v_new, cache_k, cache_v)


def _post_kernel(*refs, n_mix, decode, final, tm):
    it = iter(refs)
    x_ref = next(it)
    mix_refs = [next(it) for _ in range(n_mix)]
    wo_refs = [next(it) for _ in range(n_mix)]
    nf_ref, wg_ref, wu_ref, cw_ref, cb_ref, wd_ref = (next(it) for _ in range(6))
    if decode:
        buf0_ref, buf1_ref = next(it), next(it)
    fn_ref = next(it) if final else None
    y_ref = next(it)
    g_ref = next(it)
    act_ref = next(it)
    carry_ref = None if decode else next(it)

    x1 = x_ref[...]
    for a_ref, w_ref in zip(mix_refs, wo_refs):
        x1 = x1 + jnp.dot(a_ref[...], w_ref[...], preferred_element_type=F32)
    h2 = _rms(x1, nf_ref[...]).astype(BF16)

    if not decode:
        @pl.when(pl.program_id(1) == 0)
        def _():
            carry_ref[...] = jnp.zeros_like(carry_ref)
        row = lax.broadcasted_iota(jnp.int32, (tm, FF_CHUNK), 0)

    for c in range(D_FF // FF_CHUNK):
        sl = slice(c * FF_CHUNK, (c + 1) * FF_CHUNK)
        g = jnp.dot(h2, wg_ref[:, sl], preferred_element_type=F32)
        u = jnp.dot(h2, wu_ref[:, sl], preferred_element_type=F32)
        if decode:
            gm2, gm1 = buf0_ref[:, sl], buf1_ref[:, sl]
            g_ref[:, sl] = g
        else:
            prev = carry_ref[:, sl]
            p6, p7 = prev[SUBLANE - 2:SUBLANE - 1, :], prev[SUBLANE - 1:SUBLANE, :]
            gm1 = jnp.where(row == 0, p7, pltpu.roll(g, 1, 0))
            gm2 = jnp.where(row == 0, p6, jnp.where(row == 1, p7, pltpu.roll(g, 2, 0)))
            last = g[tm - SUBLANE:tm, :]
            carry_ref[:, sl] = last
            g_ref[0, :, sl] = last
        cc = cb_ref[:, sl] + cw_ref[0:1, sl] * gm2 + cw_ref[1:2, sl] * gm1 + cw_ref[2:3, sl] * g
        act_ref[:, sl] = (jax.nn.gelu(cc) * u).astype(BF16)

    x2 = x1 + jnp.dot(act_ref[...], wd_ref[...], preferred_element_type=F32)
    y_ref[...] = _rms(x2, fn_ref[...]) if final else x2


def _post(x, mixes, wos, layer, nf, wg, wu, cw, cb, wd, *, n_seq, tm, conv_bufs=None,
          final_norm=None):
    rows = x.shape[0]
    decode = conv_bufs is not None
    nt = rows // n_seq // tm if not decode else rows // tm
    grid = (1, nt) if decode else (n_seq, nt)
    row = lambda w: pl.BlockSpec((tm, w), lambda a, i: (a * nt + i, 0))
    args = [x] + list(mixes) + list(wos) + [nf, wg, wu, cw, cb, wd]
    in_specs = ([row(D_MODEL)] + [row(m.shape[1]) for m in mixes]
                + [_single_spec(w.shape) for w in wos]
                + [_layer_spec(a.shape, layer) for a in (nf, wg, wu, cw, cb, wd)])
    if decode:
        args += list(conv_bufs)
        in_specs += [row(D_FF), row(D_FF)]
    if final_norm is not None:
        args.append(final_norm)
        in_specs.append(_const_spec(final_norm.shape))
    if decode:
        g_shape = jax.ShapeDtypeStruct((rows, D_FF), F32)
        g_spec = row(D_FF)
    else:
        g_shape = jax.ShapeDtypeStruct((n_seq, SUBLANE, D_FF), F32)
        g_spec = pl.BlockSpec((1, SUBLANE, D_FF), lambda a, i: (a, 0, 0))
    scratch = [pltpu.VMEM((tm, D_FF), BF16)]
    if not decode:
        scratch.append(pltpu.VMEM((SUBLANE, D_FF), F32))
    kern = functools.partial(_post_kernel, n_mix=len(mixes), decode=decode,
                             final=final_norm is not None, tm=tm)
    return pl.pallas_call(
        kern,
        grid=grid,
        in_specs=in_specs,
        out_specs=[row(D_MODEL), g_spec],
        out_shape=[jax.ShapeDtypeStruct((rows, D_MODEL), F32), g_shape],
        scratch_shapes=scratch,
        compiler_params=_cparams(("arbitrary", "arbitrary")),
        name="post_decode" if decode else "post_prompt",
    )(*args)


def _prep_even(e_w_in, e_w_uq, e_w_uk, e_w_uv, e_s5_b_re, e_s5_b_im, e_s5_c_re, e_s5_c_im):
    o1 = S5_WIDTH
    o2 = o1 + MLA_Q_LORA
    o3 = o2 + MLA_KV_LORA
    zpad = lambda n: jnp.zeros((D_MODEL, n), F32)
    win = jnp.concatenate([e_w_in[:, :o3], zpad(MLA_NOPE), e_w_in[:, o3:],
                           zpad(HEAD_PAD - MLA_NOPE - MLA_ROPE)], axis=1).astype(BF16)
    pad_last = lambda w, n: jnp.pad(w, ((0, 0), (0, 0), (0, n - w.shape[2])))
    wuq = pad_last(e_w_uq, HEAD_PAD).reshape(MLA_Q_LORA, MLA_HEADS * HEAD_PAD).astype(BF16)
    wuk = pad_last(e_w_uk, HEAD_PAD).reshape(MLA_KV_LORA, MLA_HEADS * HEAD_PAD).astype(BF16)
    wuv = e_w_uv.reshape(MLA_KV_LORA, MLA_HEADS * MLA_V).astype(BF16)
    wk_t = jnp.pad(jnp.transpose(e_w_uk, (1, 2, 0)),
                   ((0, 0), (0, HEAD_PAD - MLA_NOPE), (0, 0))).astype(BF16)
    wv4 = e_w_uv.reshape(MLA_KV_LORA, MLA_HEADS // 2, 2, MLA_V)
    zv = jnp.zeros_like(wv4[:, :, 0])
    wv_big = jnp.stack([jnp.concatenate([wv4[:, :, 0], zv], axis=-1),
                        jnp.concatenate([zv, wv4[:, :, 1]], axis=-1)], axis=2)
    wv_big = wv_big.reshape(MLA_KV_LORA, MLA_HEADS * LANE).astype(BF16)
    gs = S5_STRIP // S5_STATE
    ns = S5_GROUPS // gs
    eye = jnp.eye(gs, dtype=F32)
    bd_in = lambda b: jnp.einsum('sgph,gk->sghkp', b.reshape(ns, gs, S5_STATE, S5_GROUP),
                                 eye).reshape(ns, gs * S5_GROUP, S5_STRIP)
    bd_out = lambda c: jnp.einsum('sghp,gk->sgpkh', c.reshape(ns, gs, S5_GROUP, S5_STATE),
                                  eye).reshape(ns, S5_STRIP, gs * S5_GROUP)
    wb = jnp.concatenate([bd_in(e_s5_b_re), bd_in(e_s5_b_im)], axis=2).astype(BF16)
    wcr = bd_out(e_s5_c_re).astype(BF16)
    wci = (-bd_out(e_s5_c_im)).astype(BF16)
    return win, wuq, wuk, wuv, wk_t, wv_big, wb, wcr, wci


def kernel(x_prompt, x_sample, page_table, state_s5_re, state_s5_im, cache_mla_ckv, cache_mla_krope,
           cache_swa_k, cache_swa_v, state_ffn_conv, norm_mix, norm_ffn, final_norm, e_w_in,
           e_s5_lam_re, e_s5_lam_im, e_s5_log_dt, e_s5_b_re, e_s5_b_im, e_s5_c_re, e_s5_c_im, e_s5_d,
           e_s5_w_glu, e_s5_b_glu, e_q_norm, e_w_uq, e_kv_norm, e_w_uk, e_w_uv, e_w_out, o_w_in,
           o_sinks, o_w_out, f_w_gate, f_w_up, f_conv_w, f_conv_b, f_w_down):
    n_p, t_p, _ = x_prompt.shape
    n_s, t_s, _ = x_sample.shape
    assert t_s == 1
    past_len = page_table.shape[1] * PAGE
    row2 = lambda v: v.reshape(1, -1)

    (win, wuq, wuk, wuv, wk_t, wv_big, wb, wcr, wci) = _prep_even(
        e_w_in[0], e_w_uq[0], e_w_uk[0], e_w_uv[0], e_s5_b_re[0], e_s5_b_im[0],
        e_s5_c_re[0], e_s5_c_im[0])
    lamr, lami = row2(e_s5_lam_re[0]), row2(e_s5_lam_im[0])
    ldt = row2(jnp.repeat(e_s5_log_dt[0], S5_STATE))
    s5_consts = (lamr, lami, ldt, wb, wcr, wci, row2(e_s5_d[0]), e_s5_w_glu[0].astype(BF16),
                 row2(e_s5_b_glu[0]))
    wout_s5 = e_w_out[0][:S5_WIDTH].astype(BF16)
    wout_mla = e_w_out[0][S5_WIDTH:].astype(BF16)
    owin = o_w_in[0].astype(BF16)
    owout = o_w_out[0].astype(BF16)
    n_layers = norm_ffn.shape[0]
    ffn_all = (norm_ffn.reshape(n_layers, 1, D_MODEL), f_w_gate.astype(BF16), f_w_up.astype(BF16),
               f_conv_w, f_conv_b.reshape(n_layers, 1, D_FF), f_w_down.astype(BF16))
    ffn = [(l,) + ffn_all for l in range(n_layers)]
    nm0, nm1 = row2(norm_mix[0]), row2(norm_mix[1])
    qnorm, kvnorm = row2(e_q_norm[0]), row2(e_kv_norm[0])
    fnorm = row2(final_norm)
    sinks = o_sinks[0]

    xp = x_prompt.reshape(n_p * t_p, D_MODEL)
    pos_p = np.arange(t_p)
    u, qt_p, ckv_p, kr_p, k_big, vt_p = _even_proj(
        xp, _mla_rope_tables(pos_p), nm0, win, qnorm, wuq.T, kvnorm, wuk, wuv.T, tm=TM_EVEN_PROJ,
        n_seq=n_p,
        with_kv=True)
    zeros_state = jnp.zeros((n_p, S5_NSTATE), F32)
    y_s5, p_hr, p_hi = _s5(u.reshape(n_p, t_p, S5_WIDTH), zeros_state, zeros_state, *s5_consts,
                           nb=n_p, tt=S5_TT, transpose_io=True)
    o_mla = _mla_attn(qt_p, k_big.reshape(n_p, t_p, -1), vt_p)
    x1, conv0 = _post(xp, [y_s5.reshape(n_p * t_p, -1), o_mla.reshape(n_p * t_p, -1)],
                      [wout_s5, wout_mla], *ffn[0], n_seq=n_p, tm=TM_POST)
    qt1, k1, vt1, k_tail, v_tail = _odd_proj_t(x1, _swa_rope_tables(pos_p), nm1, owin.T,
                                               tm=TM_ODD_PROJ,
                                               n_seq=n_p)
    kvw = SWA_KV_HEADS * SWA_HD
    o_swa = _swa_attn(sinks, qt1, k1.reshape(n_p, t_p, kvw), vt1)
    y_p, conv1 = _post(x1, [o_swa.reshape(n_p * t_p, -1)], [owout], *ffn[1], n_seq=n_p, tm=TM_POST,
                       final_norm=fnorm)

    y_prompt = y_p.reshape(n_p, t_p, D_MODEL)
    p_s5_re = p_hr.reshape(1, n_p, S5_GROUPS, S5_STATE)
    p_s5_im = p_hi.reshape(1, n_p, S5_GROUPS, S5_STATE)
    p_ckv = ckv_p.reshape(1, n_p, t_p, MLA_KV_LORA)
    p_krope = kr_p.reshape(1, n_p, t_p, MLA_ROPE)
    assert t_p >= WINDOW
    p_swa_k = k_tail.reshape(1, n_p, WINDOW, SWA_KV_HEADS, SWA_HD)
    p_swa_v = v_tail.reshape(1, n_p, WINDOW, SWA_KV_HEADS, SWA_HD)
    p_conv = jnp.stack([conv0[:, SUBLANE - 2:], conv1[:, SUBLANE - 2:]])

    xs = x_sample.reshape(n_s, D_MODEL)
    pos_s = np.full((n_s,), past_len)
    u_s, qs_big, ckv_s, kr_s = _even_proj(
        xs, _mla_rope_tables(pos_s), nm0, win, qnorm, wuq, kvnorm, wuk, wuv, tm=n_s, n_seq=n_s,
        with_kv=False)
    ys_s5, s_hr, s_hi = _s5(u_s.reshape(1, n_s, S5_WIDTH), state_s5_re[0].reshape(n_s, S5_NSTATE),
                            state_s5_im[0].reshape(n_s, S5_NSTATE), *s5_consts,
                            nb=n_s, tt=1, transpose_io=False)
    q_lat = _qlat(qs_big, wk_t).reshape(n_s, MLA_HEADS, MLA_KV_LORA)
    q_rope = qs_big.reshape(n_s, MLA_HEADS, HEAD_PAD)[:, :, MLA_NOPE:MLA_NOPE + MLA_ROPE]
    o_lat = _mla_decode(page_table, q_lat, q_rope, ckv_s.reshape(n_s, 1, MLA_KV_LORA),
                        kr_s.reshape(n_s, 1, MLA_ROPE), cache_mla_ckv,
                        jnp.swapaxes(cache_mla_krope, 2, 3))
    os_mla = _olat(o_lat.reshape(n_s, MLA_HEADS * MLA_KV_LORA), wv_big)
    xs1, g0 = _post(xs, [ys_s5.reshape(n_s, -1), os_mla], [wout_s5, wout_mla], *ffn[0],
                    n_seq=n_s, tm=n_s, conv_bufs=(state_ffn_conv[0, :, 0], state_ffn_conv[0, :, 1]))
    qs1, ks1, vs1 = _odd_proj(xs1, _swa_rope_tables(pos_s), nm1, owin, tm=n_s)
    expand = lambda a: jnp.repeat(a.reshape(n_s, SWA_KV_HEADS, SWA_HD), SWA_GQ, axis=1)
    os_swa = _swa_decode(sinks, qs1.reshape(n_s, SWA_HEADS, SWA_HD), expand(ks1), expand(vs1),
                         jnp.transpose(cache_swa_k[0], (0, 2, 3, 1)),
                         jnp.transpose(cache_swa_v[0], (0, 2, 3, 1)))
    ys, g1 = _post(xs1, [os_swa.reshape(n_s, -1)], [owout], *ffn[1], n_seq=n_s, tm=n_s,
                   conv_bufs=(state_ffn_conv[1, :, 0], state_ffn_conv[1, :, 1]), final_norm=fnorm)

    y_sample = ys.reshape(n_s, 1, D_MODEL)
    s_s5_re = s_hr.reshape(1, n_s, S5_GROUPS, S5_STATE)
    s_s5_im = s_hi.reshape(1, n_s, S5_GROUPS, S5_STATE)
    s_ckv = ckv_s.reshape(1, n_s, 1, MLA_KV_LORA)
    s_krope = kr_s.reshape(1, n_s, 1, MLA_ROPE)
    s_swa_k = ks1.reshape(1, n_s, 1, SWA_KV_HEADS, SWA_HD)
    s_swa_v = vs1.reshape(1, n_s, 1, SWA_KV_HEADS, SWA_HD)
    s_conv = jnp.stack([jnp.stack([state_ffn_conv[0, :, 1], g0], axis=1),
                        jnp.stack([state_ffn_conv[1, :, 1], g1], axis=1)])

    return (y_prompt, y_sample, p_s5_re, p_s5_im, p_ckv, p_krope, p_swa_k, p_swa_v, p_conv,
            s_s5_re, s_s5_im, s_ckv, s_krope, s_swa_k, s_swa_v, s_conv)
```

```python
import functools
import math

import numpy as np
import jax
import jax.numpy as jnp
from jax import lax
from jax.experimental import pallas as pl
from jax.experimental.pallas import tpu as pltpu

F32 = jnp.float32
BF16 = jnp.bfloat16

D_MODEL = 1024
S5_WIDTH = 512
S5_GROUPS = 32
S5_GROUP = 16
S5_STATE = 64
S5_NSTATE = S5_GROUPS * S5_STATE
MLA_HEADS = 16
MLA_NOPE = 64
MLA_ROPE = 32
MLA_V = 64
MLA_Q_LORA = 384
MLA_KV_LORA = 256
MLA_THETA = 10000.0
MLA_SCALE = 1.0 / math.sqrt(MLA_NOPE + MLA_ROPE)
SWA_HEADS = 16
SWA_KV_HEADS = 4
SWA_HD = 64
SWA_GQ = SWA_HEADS // SWA_KV_HEADS
WINDOW = 128
ROT_DIM = SWA_HD // 4
ROPE_THETA = 500000.0
SWA_SCALE = 1.0 / math.sqrt(SWA_HD)
D_FF = 2816
PAGE = 128
EPS = 1e-6
NEG = -1e30
LOG2E = math.log2(math.e)

LANE = 128
SUBLANE = 8
HEAD_PAD = LANE
Z_EVEN = S5_WIDTH + MLA_Q_LORA + MLA_KV_LORA + LANE
QK_ODD = (SWA_HEADS + SWA_KV_HEADS) * SWA_HD
FF_CHUNK = 256
VMEM_LIMIT = 48 * 1024 * 1024

TM_EVEN_PROJ = 512
TM_ODD_PROJ = 512
TM_POST = 512
MLA_TQ = 256
S5_TT = 128


def _cparams(sem):
    return pltpu.CompilerParams(dimension_semantics=sem, vmem_limit_bytes=VMEM_LIMIT)


def _rms(x, g):
    return x * lax.rsqrt(jnp.mean(x * x, axis=-1, keepdims=True) + EPS) * g


def _const_spec(shape):
    nd = len(shape)
    return pl.BlockSpec(shape, lambda *_: (0,) * nd)


def _single_spec(shape):
    nd = len(shape)
    return pl.BlockSpec(shape, lambda *_: (0,) * nd, pipeline_mode=pl.Buffered(1))


def _layer_spec(shape, layer):
    tail = (0,) * (len(shape) - 1)
    return pl.BlockSpec((None,) + tuple(shape[1:]), lambda *_: (layer,) + tail,
                        pipeline_mode=pl.Buffered(1))


def _mla_rope_tables(pos):
    half = MLA_ROPE // 2
    inv = MLA_THETA ** (-np.arange(half, dtype=np.float64) * 2.0 / MLA_ROPE)
    ang = np.asarray(pos, np.float64)[:, None] * inv[None, :]
    cos, sin = np.cos(ang), np.sin(ang)
    p = ang.shape[0]
    c = np.zeros((p, LANE)); sa = np.zeros((p, LANE)); sb = np.zeros((p, LANE))
    c[:, :MLA_NOPE] = 1.0
    c[:, MLA_NOPE:MLA_NOPE + half] = cos
    c[:, MLA_NOPE + half:MLA_NOPE + 2 * half] = cos
    sa[:, MLA_NOPE + half:MLA_NOPE + 2 * half] = sin
    sb[:, MLA_NOPE:MLA_NOPE + half] = -sin
    return tuple(jnp.asarray(t, F32) for t in (c, sa, sb))


def _swa_rope_tables(pos):
    half = ROT_DIM // 2
    inv = ROPE_THETA ** (-np.arange(half, dtype=np.float64) * 2.0 / ROT_DIM)
    ang = np.asarray(pos, np.float64)[:, None] * inv[None, :]
    cos, sin = np.cos(ang), np.sin(ang)
    p = ang.shape[0]
    c = np.ones((p, LANE)); sa = np.zeros((p, LANE)); sb = np.zeros((p, LANE))
    for o in (0, SWA_HD):
        c[:, o:o + half] = cos
        c[:, o + half:o + 2 * half] = cos
        sa[:, o + half:o + 2 * half] = sin
        sb[:, o:o + half] = -sin
    return tuple(jnp.asarray(t, F32) for t in (c, sa, sb))


def _rope_lanes(x, c, sa, sb, half):
    width = x.shape[1]
    reps = width // LANE
    if reps > 1:
        c = jnp.concatenate([c] * reps, axis=1)
        sa = jnp.concatenate([sa] * reps, axis=1)
        sb = jnp.concatenate([sb] * reps, axis=1)
    return x * c + pltpu.roll(x, half, 1) * sa + pltpu.roll(x, width - half, 1) * sb


def _rope_sublanes(xt, ct, sat, sbt, half):
    feats = xt.shape[0]
    reps = feats // LANE
    ct = jnp.concatenate([ct] * reps, axis=0)
    sat = jnp.concatenate([sat] * reps, axis=0)
    sbt = jnp.concatenate([sbt] * reps, axis=0)
    return xt * ct + pltpu.roll(xt, half, 0) * sat + pltpu.roll(xt, feats - half, 0) * sbt


def _even_proj_kernel(x_ref, c_ref, sa_ref, sb_ref, ct_ref, sat_ref, sbt_ref, nm_ref, win_ref, qn_ref,
                      wuq_ref, kvn_ref, wuk_ref, wuv_ref, u_ref, q_ref, ckv_ref, kr_ref, *kv_refs):
    h = _rms(x_ref[...], nm_ref[...]).astype(BF16)
    z = jnp.dot(h, win_ref[...], preferred_element_type=F32)
    u_ref[...] = z[:, :S5_WIDTH]
    c, sa, sb = c_ref[...], sa_ref[...], sb_ref[...]
    o1 = S5_WIDTH + MLA_Q_LORA
    o2 = o1 + MLA_KV_LORA
    qn = _rms(z[:, S5_WIDTH:o1], qn_ref[...]).astype(BF16)
    ckv = _rms(z[:, o1:o2], kvn_ref[...])
    ckv_ref[...] = ckv
    kr = _rope_lanes(z[:, o2:], c, sa, sb, MLA_ROPE // 2)
    kr_ref[...] = kr[:, MLA_NOPE:MLA_NOPE + MLA_ROPE]
    nt_dims = (((1,), (1,)), ((), ()))
    if kv_refs:
        k_ref, vt_ref = kv_refs
        qt = lax.dot_general(wuq_ref[...], qn, nt_dims, preferred_element_type=F32)
        qt = _rope_sublanes(qt, ct_ref[...], sat_ref[...], sbt_ref[...], MLA_ROPE // 2)
        tm = qt.shape[1]
        for j in range(tm // MLA_TQ):
            q_ref[0, j] = qt[:, j * MLA_TQ:(j + 1) * MLA_TQ].astype(BF16)
        ckvb = ckv.astype(BF16)
        k = jnp.dot(ckvb, wuk_ref[...], preferred_element_type=F32)
        k_ref[...] = (k + jnp.concatenate([kr] * MLA_HEADS, axis=1)).astype(BF16)
        vt = lax.dot_general(wuv_ref[...], ckvb, nt_dims, preferred_element_type=F32)
        for j in range(tm // MLA_TQ):
            vt_ref[0, j] = vt[:, j * MLA_TQ:(j + 1) * MLA_TQ].astype(BF16)
    else:
        q = jnp.dot(qn, wuq_ref[...], preferred_element_type=F32)
        q_ref[...] = _rope_lanes(q, c, sa, sb, MLA_ROPE // 2).astype(BF16)


def _even_proj(x, tabs, nm, win, qnorm, wuq, kvnorm, wuk, wuv, *, tm, n_seq, with_kv):
    rows = x.shape[0]
    nt = tabs[0].shape[0] // tm
    tabs_t = tuple(t.T for t in tabs)
    row = lambda w: pl.BlockSpec((tm, w), lambda i: (i, 0))
    tab = pl.BlockSpec((tm, LANE), lambda i: (i % nt, 0))
    tab_t = pl.BlockSpec((LANE, tm), lambda i: (0, i % nt))
    tq = min(tm, MLA_TQ)
    tblock = lambda w: pl.BlockSpec((1, tm // tq, w, tq), lambda i: (i // nt, i % nt, 0, 0))
    qw = MLA_HEADS * HEAD_PAD
    vw = MLA_HEADS * MLA_V
    if with_kv:
        q_shape, q_spec = jax.ShapeDtypeStruct((n_seq, nt * tm // tq, qw, tq), BF16), tblock(qw)
    else:
        q_shape, q_spec = jax.ShapeDtypeStruct((rows, qw), BF16), row(qw)
    out_shape = [jax.ShapeDtypeStruct((rows, S5_WIDTH), F32), q_shape,
                 jax.ShapeDtypeStruct((rows, MLA_KV_LORA), F32),
                 jax.ShapeDtypeStruct((rows, MLA_ROPE), F32)]
    out_specs = [row(S5_WIDTH), q_spec, row(MLA_KV_LORA), row(MLA_ROPE)]
    if with_kv:
        out_shape += [jax.ShapeDtypeStruct((rows, qw), BF16),
                      jax.ShapeDtypeStruct((n_seq, nt * tm // tq, vw, tq), BF16)]
        out_specs += [row(qw), tblock(vw)]
    return pl.pallas_call(
        _even_proj_kernel,
        grid=(rows // tm,),
        in_specs=[row(D_MODEL), tab, tab, tab, tab_t, tab_t, tab_t, _const_spec(nm.shape),
                  _const_spec(win.shape), _const_spec(qnorm.shape), _const_spec(wuq.shape),
                  _const_spec(kvnorm.shape), _const_spec(wuk.shape), _const_spec(wuv.shape)],
        out_specs=out_specs,
        out_shape=out_shape,
        compiler_params=_cparams(("arbitrary",)),
        name="even_proj",
    )(x, *tabs, *tabs_t, nm, win, qnorm, wuq, kvnorm, wuk, wuv)


S5_STRIP = 512


def _s5_kernel(u_ref, h0r_ref, h0i_ref, lamr_ref, lami_ref, ldt_ref, wb_ref, wcr_ref, wci_ref,
               d_ref, wglu_ref, bglu_ref, y_ref, hr_ref, hi_ref, ut_ref, xr_ref, xi_ref, yt_ref,
               *, nb, tt, transpose_io):
    @pl.when(pl.program_id(0) == 0)
    def _():
        xr_ref[0:nb, :] = h0r_ref[...]
        xi_ref[0:nb, :] = h0i_ref[...]

    if transpose_io:
        for t in range(tt):
            ut_ref[t * nb:(t + 1) * nb, :] = u_ref[:, t, :]
    else:
        ut_ref[...] = u_ref[0]
    ub = ut_ref[...].astype(BF16)

    lr = jnp.minimum(lamr_ref[...], -1e-4)
    li = lami_ref[...]
    dt = jnp.exp(ldt_ref[...])
    mag = jnp.exp(lr * dt)
    ar = mag * jnp.cos(li * dt)
    ai = mag * jnp.sin(li * dt)
    den = lr * lr + li * li
    zr = ((ar - 1.0) * lr + ai * li) / den
    zi = (ai * lr - (ar - 1.0) * li) / den

    ucols = S5_STRIP // S5_STATE * S5_GROUP
    ys = []
    for s in range(S5_NSTATE // S5_STRIP):
        cols = slice(s * S5_STRIP, (s + 1) * S5_STRIP)
        us = ub[:, s * ucols:(s + 1) * ucols]
        br = jnp.dot(us, wb_ref[s, :, :S5_STRIP], preferred_element_type=F32)
        bi = jnp.dot(us, wb_ref[s, :, S5_STRIP:], preferred_element_type=F32)
        zrs, zis = zr[:, cols], zi[:, cols]
        xr_ref[nb:, cols] = zrs * br - zis * bi
        xi_ref[nb:, cols] = zrs * bi + zis * br
        ars = jnp.broadcast_to(ar[:, cols], (nb, S5_STRIP))
        ais = jnp.broadcast_to(ai[:, cols], (nb, S5_STRIP))
        hr, hi = xr_ref[0:nb, cols], xi_ref[0:nb, cols]
        for t in range(tt):
            r = slice((t + 1) * nb, (t + 2) * nb)
            hr, hi = (ars * hr - ais * hi + xr_ref[r, cols],
                      ars * hi + ais * hr + xi_ref[r, cols])
            xr_ref[r, cols] = hr
            xi_ref[r, cols] = hi
        xr_ref[0:nb, cols] = hr
        xi_ref[0:nb, cols] = hi
        ys.append(jnp.dot(xr_ref[nb:, cols].astype(BF16), wcr_ref[s], preferred_element_type=F32)
                  + jnp.dot(xi_ref[nb:, cols].astype(BF16), wci_ref[s],
                            preferred_element_type=F32))

    hr_ref[...] = xr_ref[0:nb, :]
    hi_ref[...] = xi_ref[0:nb, :]
    y = jnp.concatenate(ys, axis=1) + d_ref[...] * ut_ref[...]
    y = jax.nn.gelu(y)
    gate = jnp.dot(y.astype(BF16), wglu_ref[...], preferred_element_type=F32) + bglu_ref[...]
    y = y * jax.nn.sigmoid(gate)
    if transpose_io:
        groups = S5_WIDTH // LANE
        for c in range(groups):
            yt_ref[c] = y[:, c * LANE:(c + 1) * LANE]
        for n in range(nb):
            y_ref[n] = jnp.concatenate(
                [yt_ref[c, pl.ds(n, tt, stride=nb), :] for c in range(groups)],
                axis=1).astype(y_ref.dtype)
    else:
        y_ref[0] = y.astype(y_ref.dtype)


def _s5(u, h0r, h0i, lamr, lami, ldt, wb, wcr, wci, d, wglu, bglu, *, nb, tt, transpose_io):
    t_total = u.shape[1] if transpose_io else u.shape[0]
    if transpose_io:
        u_spec = pl.BlockSpec((nb, tt, S5_WIDTH), lambda i: (0, i, 0))
        y_shape = (nb, t_total, S5_WIDTH)
    else:
        u_spec = pl.BlockSpec((tt, nb, S5_WIDTH), lambda i: (i, 0, 0))
        y_shape = (t_total, nb, S5_WIDTH)
    consts = (h0r, h0i, lamr, lami, ldt, wb, wcr, wci, d, wglu, bglu)
    kern = functools.partial(_s5_kernel, nb=nb, tt=tt, transpose_io=transpose_io)
    return pl.pallas_call(
        kern,
        grid=(t_total // tt,),
        in_specs=[u_spec] + [_const_spec(c.shape) for c in consts],
        out_specs=[u_spec, _const_spec((nb, S5_NSTATE)), _const_spec((nb, S5_NSTATE))],
        out_shape=[jax.ShapeDtypeStruct(y_shape, BF16),
                   jax.ShapeDtypeStruct((nb, S5_NSTATE), F32),
                   jax.ShapeDtypeStruct((nb, S5_NSTATE), F32)],
        scratch_shapes=[pltpu.VMEM((nb * tt, S5_WIDTH), F32),
                        pltpu.VMEM((nb * (tt + 1), S5_NSTATE), F32),
                        pltpu.VMEM((nb * (tt + 1), S5_NSTATE), F32),
                        pltpu.VMEM((S5_WIDTH // LANE, nb * tt, LANE), F32)],
        compiler_params=_cparams(("arbitrary",)),
        name="s5_mixer",
    )(u, *consts)


MLA_LOOKAHEAD = 4


def _mla_attn_kernel(qt_ref, k_ref, vt_ref, o_ref, m_ref, l_ref, acc_ref, *, tq):
    qi = pl.program_id(1)
    m_ref[...] = jnp.full(m_ref.shape, NEG, F32)
    l_ref[...] = jnp.zeros(l_ref.shape, F32)
    acc_ref[...] = jnp.zeros(acc_ref.shape, F32)

    def causal_mask(nblk):
        key = lax.broadcasted_iota(jnp.int32, (nblk * tq, tq), 0)
        qry = lax.broadcasted_iota(jnp.int32, (nblk * tq, tq), 1)
        return key <= qry + (nblk - 1) * tq

    def process(kb0, nblk, mask):
        keys = pl.ds(pl.multiple_of(kb0 * tq, tq), nblk * tq)

        def scores(h):
            hl = slice(h * HEAD_PAD, (h + 1) * HEAD_PAD)
            return jnp.dot(k_ref[0, keys, hl], qt_ref[0, 0, hl, :],
                           preferred_element_type=F32)

        pending = [scores(h) for h in range(MLA_LOOKAHEAD)]
        for h in range(MLA_HEADS):
            st = pending.pop(0) * (MLA_SCALE * LOG2E)
            if h + MLA_LOOKAHEAD < MLA_HEADS:
                pending.append(scores(h + MLA_LOOKAHEAD))
            if mask is not None:
                st = jnp.where(mask, st, NEG)
            j = h // 2
            vts = [vt_ref[0, kb0 + i, j * LANE:(j + 1) * LANE, :] for i in range(nblk)]
            vt = vts[0] if nblk == 1 else jnp.concatenate(vts, axis=1)
            m_old = m_ref[h]
            m_new = jnp.maximum(m_old, jnp.max(st, axis=0, keepdims=True))
            alpha = jnp.exp2(m_old - m_new)
            pt = jnp.exp2(st - m_new)
            l_ref[h] = alpha * l_ref[h] + jnp.sum(pt, axis=0, keepdims=True)
            m_ref[h] = m_new
            acc_ref[h] = alpha * acc_ref[h] + jnp.dot(vt, pt.astype(BF16),
                                                      preferred_element_type=F32)

    n_wide = qi // 2

    def wide_step(i, carry):
        process(i * 2, 2, None)
        return carry

    lax.fori_loop(0, n_wide, wide_step, 0)

    @pl.when(qi % 2 == 0)
    def _():
        process(qi, 1, causal_mask(1))

    @pl.when(qi % 2 == 1)
    def _():
        process(qi - 1, 2, causal_mask(2))

    half = lax.broadcasted_iota(jnp.int32, (LANE, tq), 0) < MLA_V
    for j in range(MLA_HEADS // 2):
        ot = jnp.where(half, acc_ref[2 * j] / l_ref[2 * j], acc_ref[2 * j + 1] / l_ref[2 * j + 1])
        o_ref[0, :, j * LANE:(j + 1) * LANE] = ot.T.astype(o_ref.dtype)


def _mla_attn(qt, k, vt):
    n, nblocks, qw, tq = qt.shape
    t = k.shape[1]
    kern = functools.partial(_mla_attn_kernel, tq=tq)
    return pl.pallas_call(
        kern,
        grid=(n, nblocks),
        in_specs=[pl.BlockSpec((1, 1, qw, tq), lambda b, i: (b, i, 0, 0)),
                  pl.BlockSpec((1, t, k.shape[2]), lambda b, i: (b, 0, 0)),
                  pl.BlockSpec((1, nblocks, vt.shape[2], tq), lambda b, i: (b, 0, 0, 0))],
        out_specs=pl.BlockSpec((1, tq, MLA_HEADS * MLA_V), lambda b, i: (b, i, 0)),
        out_shape=jax.ShapeDtypeStruct((n, t, MLA_HEADS * MLA_V), BF16),
        scratch_shapes=[pltpu.VMEM((MLA_HEADS, 1, tq), F32),
                        pltpu.VMEM((MLA_HEADS, 1, tq), F32),
                        pltpu.VMEM((MLA_HEADS, LANE, tq), F32)],
        compiler_params=_cparams(("arbitrary", "arbitrary")),
        name="mla_prompt_attn",
    )(qt, k, vt)


DEC_PAGES = 32
DEC_SUB = 4
DEC_AHEAD = 3
DEC_SLOTS = DEC_AHEAD + 1
DEC_KEYS = DEC_PAGES * PAGE


def _qlat_kernel(q_ref, wk_ref, o_ref):
    for h in range(MLA_HEADS):
        o_ref[:, h * MLA_KV_LORA:(h + 1) * MLA_KV_LORA] = jnp.dot(
            q_ref[:, h * HEAD_PAD:(h + 1) * HEAD_PAD], wk_ref[h], preferred_element_type=F32)


def _qlat(q_big, wk_t):
    rows = q_big.shape[0]
    return pl.pallas_call(
        _qlat_kernel,
        grid=(1,),
        in_specs=[_const_spec(q_big.shape), _const_spec(wk_t.shape)],
        out_specs=_const_spec((rows, MLA_HEADS * MLA_KV_LORA)),
        out_shape=jax.ShapeDtypeStruct((rows, MLA_HEADS * MLA_KV_LORA), F32),
        compiler_params=_cparams(("arbitrary",)),
        name="mla_q_absorb",
    )(q_big, wk_t)


def _olat_kernel(o_ref, wv_ref, out_ref):
    for j in range(MLA_HEADS // 2):
        acc = None
        for h in (2 * j, 2 * j + 1):
            part = jnp.dot(o_ref[:, h * MLA_KV_LORA:(h + 1) * MLA_KV_LORA].astype(BF16),
                           wv_ref[:, h * LANE:(h + 1) * LANE], preferred_element_type=F32)
            acc = part if acc is None else acc + part
        out_ref[:, j * LANE:(j + 1) * LANE] = acc.astype(out_ref.dtype)


def _olat(o_lat, wv_big):
    rows = o_lat.shape[0]
    return pl.pallas_call(
        _olat_kernel,
        grid=(1,),
        in_specs=[_const_spec(o_lat.shape), _const_spec(wv_big.shape)],
        out_specs=_const_spec((rows, MLA_HEADS * MLA_V)),
        out_shape=jax.ShapeDtypeStruct((rows, MLA_HEADS * MLA_V), BF16),
        compiler_params=_cparams(("arbitrary",)),
        name="mla_v_expand",
    )(o_lat, wv_big)


def _page_copies(cache_ckv, cache_kr, ckv_buf, kr_buf, sem, slot, page, p):
    rows = pl.ds(p * PAGE, PAGE)
    return (pltpu.make_async_copy(cache_ckv.at[0, page], ckv_buf.at[slot, rows], sem.at[0, slot]),
            pltpu.make_async_copy(cache_kr.at[0, page], kr_buf.at[slot, p], sem.at[1, slot]))


def _mla_decode_kernel(ptc_ref, ptn_ref, ql_ref, qr_ref, ckvn_ref, krn_ref, cache_ckv, cache_kr,
                       o_ref, ckv_buf, kr_buf, sem, *, n_seq, n_chunks):
    b = pl.program_id(0)
    g0 = b * n_chunks

    def start_page(page, slot, p):
        for cp in _page_copies(cache_ckv, cache_kr, ckv_buf, kr_buf, sem, slot, page, p):
            cp.start()

    def wait_chunk(slot):
        for p in range(DEC_PAGES):
            for cp in _page_copies(cache_ckv, cache_kr, ckv_buf, kr_buf, sem, slot, 0, p):
                cp.wait()

    @pl.when(b == 0)
    def _():
        for a in range(DEC_AHEAD):
            for p in range(DEC_PAGES):
                start_page(ptc_ref[0, 0, a * DEC_PAGES + p], a, p)

    ql = ql_ref[0].astype(BF16)
    qr = qr_ref[0].astype(BF16)

    def chunk_body(c, carry):
        m, l, acc = carry
        slot = lax.rem(g0 + c, DEC_SLOTS)
        ahead_slot = lax.rem(g0 + c + DEC_AHEAD, DEC_SLOTS)
        wait_chunk(slot)
        wraps = c + DEC_AHEAD >= n_chunks
        ahead_base = jnp.where(wraps, c + DEC_AHEAD - n_chunks, c + DEC_AHEAD) * DEC_PAGES

        def ahead_page(p):
            return jnp.where(wraps, ptn_ref[0, 0, ahead_base + p], ptc_ref[0, 0, ahead_base + p])

        kbs, scores = [], []
        for i in range(DEC_PAGES // DEC_SUB):
            kb = ckv_buf[slot, i * DEC_SUB * PAGE:(i + 1) * DEC_SUB * PAGE, :].astype(BF16)
            krt = jnp.concatenate([kr_buf[slot, i * DEC_SUB + p] for p in range(DEC_SUB)],
                                  axis=1).astype(BF16)
            kbs.append(kb)
            scores.append(
                (lax.dot_general(ql, kb, (((1,), (1,)), ((), ())), preferred_element_type=F32)
                 + jnp.dot(qr, krt, preferred_element_type=F32)) * MLA_SCALE)
            for p in range(i * DEC_SUB, (i + 1) * DEC_SUB):
                start_page(ahead_page(p), ahead_slot, p)
        for kb, s in zip(kbs, scores):
            m_new = jnp.maximum(m, jnp.max(s, axis=1, keepdims=True))
            alpha = jnp.exp(m - m_new)
            pr = jnp.exp(s - m_new)
            l = alpha * l + jnp.sum(pr, axis=1, keepdims=True)
            acc = alpha * acc + jnp.dot(pr.astype(BF16), kb, preferred_element_type=F32)
            m = m_new
        return m, l, acc

    init = (jnp.full((MLA_HEADS, 1), NEG, F32), jnp.zeros((MLA_HEADS, 1), F32),
            jnp.zeros((MLA_HEADS, MLA_KV_LORA), F32))
    m, l, acc = lax.fori_loop(0, n_chunks, chunk_body, init)

    @pl.when(b == n_seq - 1)
    def _():
        for a in range(DEC_AHEAD):
            wait_chunk((n_seq * n_chunks + a) % DEC_SLOTS)

    cn = ckvn_ref[0].astype(BF16).astype(F32)
    kn = krn_ref[0].astype(BF16).astype(F32)
    s_new = (jnp.sum(ql.astype(F32) * cn, axis=1, keepdims=True)
             + jnp.sum(qr.astype(F32) * kn, axis=1, keepdims=True)) * MLA_SCALE
    m_new = jnp.maximum(m, s_new)
    alpha = jnp.exp(m - m_new)
    p_new = jnp.exp(s_new - m_new)
    l = alpha * l + p_new
    acc = alpha * acc + p_new.astype(BF16).astype(F32) * cn
    o_ref[0] = acc / l


def _mla_decode(page_table, q_lat, q_rope, ckv_new, kr_new, cache_ckv, cache_kr):
    n_seq, n_pages = page_table.shape
    n_chunks = n_pages // DEC_PAGES
    assert n_pages % DEC_PAGES == 0 and n_chunks >= DEC_AHEAD
    pt = page_table.reshape(n_seq, 1, n_pages)
    smem_row = lambda f: pl.BlockSpec((1, 1, n_pages), f, memory_space=pltpu.SMEM)
    per_seq = lambda a: pl.BlockSpec((1,) + a.shape[1:], lambda b: (b, 0, 0))
    kern = functools.partial(_mla_decode_kernel, n_seq=n_seq, n_chunks=n_chunks)
    return pl.pallas_call(
        kern,
        grid=(n_seq,),
        in_specs=[smem_row(lambda b: (b, 0, 0)),
                  smem_row(lambda b: (jnp.minimum(b + 1, n_seq - 1), 0, 0)),
                  per_seq(q_lat), per_seq(q_rope), per_seq(ckv_new), per_seq(kr_new),
                  pl.BlockSpec(memory_space=pl.ANY), pl.BlockSpec(memory_space=pl.ANY)],
        out_specs=pl.BlockSpec((1, MLA_HEADS, MLA_KV_LORA), lambda b: (b, 0, 0)),
        out_shape=jax.ShapeDtypeStruct((n_seq, MLA_HEADS, MLA_KV_LORA), F32),
        scratch_shapes=[pltpu.VMEM((DEC_SLOTS, DEC_KEYS, MLA_KV_LORA), F32),
                        pltpu.VMEM((DEC_SLOTS, DEC_PAGES, MLA_ROPE, PAGE), F32),
                        pltpu.SemaphoreType.DMA((2, DEC_SLOTS))],
        compiler_params=_cparams(("arbitrary",)),
        name="mla_paged_decode",
    )(pt, pt, q_lat, q_rope, ckv_new, kr_new, cache_ckv, cache_kr)


def _odd_proj_kernel(x_ref, c_ref, sa_ref, sb_ref, nm_ref, win_ref, q_ref, k_ref, v_ref):
    h = _rms(x_ref[...], nm_ref[...]).astype(BF16)
    z = jnp.dot(h, win_ref[...], preferred_element_type=F32)
    qk = _rope_lanes(z[:, :QK_ODD], c_ref[...], sa_ref[...], sb_ref[...], ROT_DIM // 2)
    q_ref[...] = qk[:, :SWA_HEADS * SWA_HD].astype(BF16)
    k_ref[...] = qk[:, SWA_HEADS * SWA_HD:]
    v_ref[...] = z[:, QK_ODD:]


def _odd_proj(x, tabs, nm, win, *, tm):
    rows = x.shape[0]
    nt = tabs[0].shape[0] // tm
    row = lambda w: pl.BlockSpec((tm, w), lambda i: (i, 0))
    tab = pl.BlockSpec((tm, LANE), lambda i: (i % nt, 0))
    kvw = SWA_KV_HEADS * SWA_HD
    return pl.pallas_call(
        _odd_proj_kernel,
        grid=(rows // tm,),
        in_specs=[row(D_MODEL), tab, tab, tab, _const_spec(nm.shape), _const_spec(win.shape)],
        out_specs=[row(SWA_HEADS * SWA_HD), row(kvw), row(kvw)],
        out_shape=[jax.ShapeDtypeStruct((rows, SWA_HEADS * SWA_HD), BF16),
                   jax.ShapeDtypeStruct((rows, kvw), F32),
                   jax.ShapeDtypeStruct((rows, kvw), F32)],
        compiler_params=_cparams(("arbitrary",)),
        name="odd_proj",
    )(x, *tabs, nm, win)


def _odd_proj_t_kernel(x_ref, ct_ref, sat_ref, sbt_ref, nm_ref, wt_ref, qt_ref, k_ref, vt_ref,
                       ktail_ref, vtail_ref, *, tm):
    h = _rms(x_ref[...], nm_ref[...]).astype(BF16)
    zt = lax.dot_general(wt_ref[...], h, (((1,), (1,)), ((), ())),
                         preferred_element_type=F32)
    qkt = _rope_sublanes(zt[:QK_ODD], ct_ref[...], sat_ref[...], sbt_ref[...], ROT_DIM // 2)
    nq = SWA_HEADS * SWA_HD
    for i in range(tm // WINDOW):
        cols = slice(i * WINDOW, (i + 1) * WINDOW)
        qt_ref[0, i] = qkt[:nq, cols].astype(BF16)
        vt_ref[0, i] = zt[QK_ODD:, cols].astype(BF16)
    k = qkt[nq:].T
    k_ref[...] = k.astype(BF16)
    ktail_ref[0] = k[tm - WINDOW:]
    vtail_ref[0] = zt[QK_ODD:, tm - WINDOW:].T


def _odd_proj_t(x, tabs, nm, win_t, *, tm, n_seq):
    rows = x.shape[0]
    t = rows // n_seq
    nt = t // tm
    nb = tm // WINDOW
    kvw = SWA_KV_HEADS * SWA_HD
    qw = SWA_HEADS * SWA_HD
    tabs_t = tuple(a.T for a in tabs)
    row = lambda w: pl.BlockSpec((tm, w), lambda i: (i, 0))
    tab_t = pl.BlockSpec((LANE, tm), lambda i: (0, i % nt))
    tblock = lambda w: pl.BlockSpec((1, nb, w, WINDOW), lambda i: (i // nt, i % nt, 0, 0))
    tail = pl.BlockSpec((1, WINDOW, kvw), lambda i: (i // nt, 0, 0))
    return pl.pallas_call(
        functools.partial(_odd_proj_t_kernel, tm=tm),
        grid=(rows // tm,),
        in_specs=[row(D_MODEL), tab_t, tab_t, tab_t, _const_spec(nm.shape),
                  _const_spec(win_t.shape)],
        out_specs=[tblock(qw), row(kvw), tblock(kvw), tail, tail],
        out_shape=[jax.ShapeDtypeStruct((n_seq, t // WINDOW, qw, WINDOW), BF16),
                   jax.ShapeDtypeStruct((rows, kvw), BF16),
                   jax.ShapeDtypeStruct((n_seq, t // WINDOW, kvw, WINDOW), BF16),
                   jax.ShapeDtypeStruct((n_seq, WINDOW, kvw), F32),
                   jax.ShapeDtypeStruct((n_seq, WINDOW, kvw), F32)],
        compiler_params=_cparams(("arbitrary",)),
        name="odd_proj_t",
    )(x, *tabs_t, nm, win_t)


SWA_QBLOCKS = 4


def _swa_attn_kernel(sink_ref, qt_ref, kp_ref, kc_ref, vtp_ref, vtc_ref, o_ref):
    step = pl.program_id(1)
    cols = SWA_GQ * WINDOW
    key = lax.broadcasted_iota(jnp.int32, (2 * WINDOW, cols), 0)
    qry = jnp.bitwise_and(lax.broadcasted_iota(jnp.int32, (2 * WINDOW, cols), 1), WINDOW - 1)
    rel = qry + WINDOW - key
    band = (rel >= 0) & (rel <= WINDOW)
    lane = lax.broadcasted_iota(jnp.int32, (1, cols), 1)
    sinks = []
    for kh in range(SWA_KV_HEADS):
        row = jnp.full((1, cols), sink_ref[kh * SWA_GQ + SWA_GQ - 1], F32)
        for g in range(SWA_GQ - 2, -1, -1):
            row = jnp.where(lane < (g + 1) * WINDOW, sink_ref[kh * SWA_GQ + g], row)
        sinks.append(row * LOG2E)

    def block_scores(sub):
        if sub == 0:
            k_prev, vt_prev = kp_ref[0], vtp_ref[0, 0]
            valid = band & ((key >= WINDOW) | (step > 0))
        else:
            k_prev, vt_prev = kc_ref[0, (sub - 1) * WINDOW:sub * WINDOW], vtc_ref[0, sub - 1]
            valid = band
        kk = jnp.concatenate([k_prev, kc_ref[0, sub * WINDOW:(sub + 1) * WINDOW]], axis=0)
        vvt = jnp.concatenate([vt_prev, vtc_ref[0, sub]], axis=1)
        scores = []
        for kh in range(SWA_KV_HEADS):
            qg = jnp.concatenate(
                [qt_ref[0, sub, (kh * SWA_GQ + g) * SWA_HD:(kh * SWA_GQ + g + 1) * SWA_HD, :]
                 for g in range(SWA_GQ)], axis=1)
            scores.append(jnp.dot(kk[:, kh * SWA_HD:(kh + 1) * SWA_HD], qg,
                                  preferred_element_type=F32))
        return vvt, valid, scores

    ahead = block_scores(0)
    for sub in range(SWA_QBLOCKS):
        vvt, valid, scores = ahead
        if sub + 1 < SWA_QBLOCKS:
            ahead = block_scores(sub + 1)
        outs = []
        for kh in range(SWA_KV_HEADS):
            st = jnp.where(valid, scores[kh] * (SWA_SCALE * LOG2E), NEG)
            m = jnp.maximum(jnp.max(st, axis=0, keepdims=True), sinks[kh])
            p = jnp.exp2(st - m)
            l = jnp.sum(p, axis=0, keepdims=True) + jnp.exp2(sinks[kh] - m)
            ot = jnp.dot(vvt[kh * SWA_HD:(kh + 1) * SWA_HD, :], p.astype(BF16),
                         preferred_element_type=F32) / l
            outs += [ot[:, g * WINDOW:(g + 1) * WINDOW] for g in range(SWA_GQ)]
        o_ref[0, sub * WINDOW:(sub + 1) * WINDOW, :] = (
            jnp.concatenate(outs, axis=0).T.astype(o_ref.dtype))


def _swa_attn(sinks, qt, k, vt):
    n, nblocks, qw, _ = qt.shape
    t = k.shape[1]
    kvw = SWA_KV_HEADS * SWA_HD
    qb = SWA_QBLOCKS
    before = lambda i: jnp.maximum(i * qb - 1, 0)
    return pl.pallas_call(
        _swa_attn_kernel,
        grid=(n, nblocks // qb),
        in_specs=[pl.BlockSpec(memory_space=pltpu.SMEM),
                  pl.BlockSpec((1, qb, qw, WINDOW), lambda a, i: (a, i, 0, 0)),
                  pl.BlockSpec((1, WINDOW, kvw), lambda a, i: (a, before(i), 0)),
                  pl.BlockSpec((1, qb * WINDOW, kvw), lambda a, i: (a, i, 0)),
                  pl.BlockSpec((1, 1, kvw, WINDOW), lambda a, i: (a, before(i), 0, 0)),
                  pl.BlockSpec((1, qb, kvw, WINDOW), lambda a, i: (a, i, 0, 0))],
        out_specs=pl.BlockSpec((1, qb * WINDOW, qw), lambda a, i: (a, i, 0)),
        out_shape=jax.ShapeDtypeStruct((n, t, qw), BF16),
        compiler_params=_cparams(("arbitrary", "arbitrary")),
        name="swa_prompt_attn",
    )(sinks, qt, k, k, vt, vt)


SWA_DEC_SEQS = 8


def _swa_decode_kernel(sink_ref, q_ref, kn_ref, vn_ref, ck_ref, cv_ref, o_ref):
    q = q_ref[...]
    hgrp = lax.broadcasted_iota(jnp.int32, (1, SWA_HEADS, 1), 1) // SWA_GQ
    hidx = lax.broadcasted_iota(jnp.int32, (1, SWA_HEADS, 1), 1)
    sink = jnp.zeros((1, SWA_HEADS, 1), F32)
    for h in range(SWA_HEADS):
        sink = jnp.where(hidx == h, sink_ref[h], sink)
    s = jnp.zeros((q.shape[0], SWA_HEADS, WINDOW), F32)
    for kh in range(SWA_KV_HEADS):
        skh = jnp.einsum('nhd,ndj->nhj', q, ck_ref[:, kh].astype(BF16),
                         preferred_element_type=F32)
        s = jnp.where(hgrp == kh, skh, s)
    s = s * SWA_SCALE
    qf = q.astype(F32)
    kn = kn_ref[...].astype(BF16).astype(F32)
    vn = vn_ref[...].astype(BF16).astype(F32)
    s_new = jnp.sum(qf * kn, axis=2, keepdims=True) * SWA_SCALE
    m = jnp.maximum(jnp.maximum(jnp.max(s, axis=2, keepdims=True), s_new), sink)
    p = jnp.exp(s - m)
    p_new = jnp.exp(s_new - m)
    l = jnp.sum(p, axis=2, keepdims=True) + p_new + jnp.exp(sink - m)
    o = p_new.astype(BF16).astype(F32) * vn
    for kh in range(SWA_KV_HEADS):
        pk = jnp.where(hgrp == kh, p, 0.0).astype(BF16)
        o = o + jnp.einsum('nhj,ndj->nhd', pk, cv_ref[:, kh].astype(BF16),
                           preferred_element_type=F32)
    o_ref[...] = (o / l).astype(o_ref.dtype)


def _swa_decode(sinks, q, k_new, v_new, cache_k, cache_v):
    n = q.shape[0]
    nb = SWA_DEC_SEQS
    seq = pl.BlockSpec((nb, SWA_HEADS, SWA_HD), lambda i: (i, 0, 0))
    cache = pl.BlockSpec((nb, SWA_KV_HEADS, SWA_HD, WINDOW), lambda i: (i, 0, 0, 0))
    return pl.pallas_call(
        _swa_decode_kernel,
        grid=(n // nb,),
        in_specs=[pl.BlockSpec(memory_space=pltpu.SMEM), seq, seq, seq, cache, cache],
        out_specs=seq,
        out_shape=jax.ShapeDtypeStruct((n, SWA_HEADS, SWA_HD), BF16),
        compiler_params=_cparams(("arbitrary",)),
        name="swa_decode_attn",
    )(sinks, q, k_new, v_new, cache_k, cache_v)


def _post_kernel(*refs, n_mix, decode, final, tm):
    it = iter(refs)
    x_ref = next(it)
    mix_refs = [next(it) for _ in range(n_mix)]
    wo_refs = [next(it) for _ in range(n_mix)]
    nf_ref, wg_ref, wu_ref, cw_ref, cb_ref, wd_ref = (next(it) for _ in range(6))
    if decode:
        buf0_ref, buf1_ref = next(it), next(it)
    fn_ref = next(it) if final else None
    y_ref = next(it)
    g_ref = next(it)
    act_ref = next(it)
    carry_ref = None if decode else next(it)

    x1 = x_ref[...]
    for a_ref, w_ref in zip(mix_refs, wo_refs):
        x1 = x1 + jnp.dot(a_ref[...], w_ref[...], preferred_element_type=F32)
    h2 = _rms(x1, nf_ref[...]).astype(BF16)

    if not decode:
        @pl.when(pl.program_id(1) == 0)
        def _():
            carry_ref[...] = jnp.zeros_like(carry_ref)
        row = lax.broadcasted_iota(jnp.int32, (tm, FF_CHUNK), 0)

    for c in range(D_FF // FF_CHUNK):
        sl = slice(c * FF_CHUNK, (c + 1) * FF_CHUNK)
        g = jnp.dot(h2, wg_ref[:, sl], preferred_element_type=F32)
        u = jnp.dot(h2, wu_ref[:, sl], preferred_element_type=F32)
        if decode:
            gm2, gm1 = buf0_ref[:, sl], buf1_ref[:, sl]
            g_ref[:, sl] = g
        else:
            prev = carry_ref[:, sl]
            p6, p7 = prev[SUBLANE - 2:SUBLANE - 1, :], prev[SUBLANE - 1:SUBLANE, :]
            gm1 = jnp.where(row == 0, p7, pltpu.roll(g, 1, 0))
            gm2 = jnp.where(row == 0, p6, jnp.where(row == 1, p7, pltpu.roll(g, 2, 0)))
            last = g[tm - SUBLANE:tm, :]
            carry_ref[:, sl] = last
            g_ref[0, :, sl] = last
        cc = cb_ref[:, sl] + cw_ref[0:1, sl] * gm2 + cw_ref[1:2, sl] * gm1 + cw_ref[2:3, sl] * g
        act_ref[:, sl] = (jax.nn.gelu(cc) * u).astype(BF16)

    x2 = x1 + jnp.dot(act_ref[...], wd_ref[...], preferred_element_type=F32)
    y_ref[...] = _rms(x2, fn_ref[...]) if final else x2


def _post(x, mixes, wos, layer, nf, wg, wu, cw, cb, wd, *, n_seq, tm, conv_bufs=None,
          final_norm=None):
    rows = x.shape[0]
    decode = conv_bufs is not None
    nt = rows // n_seq // tm if not decode else rows // tm
    grid = (1, nt) if decode else (n_seq, nt)
    row = lambda w: pl.BlockSpec((tm, w), lambda a, i: (a * nt + i, 0))
    args = [x] + list(mixes) + list(wos) + [nf, wg, wu, cw, cb, wd]
    in_specs = ([row(D_MODEL)] + [row(m.shape[1]) for m in mixes]
                + [_single_spec(w.shape) for w in wos]
                + [_layer_spec(a.shape, layer) for a in (nf, wg, wu, cw, cb, wd)])
    if decode:
        args += list(conv_bufs)
        in_specs += [row(D_FF), row(D_FF)]
    if final_norm is not None:
        args.append(final_norm)
        in_specs.append(_const_spec(final_norm.shape))
    if decode:
        g_shape = jax.ShapeDtypeStruct((rows, D_FF), F32)
        g_spec = row(D_FF)
    else:
        g_shape = jax.ShapeDtypeStruct((n_seq, SUBLANE, D_FF), F32)
        g_spec = pl.BlockSpec((1, SUBLANE, D_FF), lambda a, i: (a, 0, 0))
    scratch = [pltpu.VMEM((tm, D_FF), BF16)]
    if not decode:
        scratch.append(pltpu.VMEM((SUBLANE, D_FF), F32))
    kern = functools.partial(_post_kernel, n_mix=len(mixes), decode=decode,
                             final=final_norm is not None, tm=tm)
    return pl.pallas_call(
        kern,
        grid=grid,
        in_specs=in_specs,
        out_specs=[row(D_MODEL), g_spec],
        out_shape=[jax.ShapeDtypeStruct((rows, D_MODEL), F32), g_shape],
        scratch_shapes=scratch,
        compiler_params=_cparams(("arbitrary", "arbitrary")),
        name="post_decode" if decode else "post_prompt",
    )(*args)


def _prep_even(e_w_in, e_w_uq, e_w_uk, e_w_uv, e_s5_b_re, e_s5_b_im, e_s5_c_re, e_s5_c_im):
    o1 = S5_WIDTH
    o2 = o1 + MLA_Q_LORA
    o3 = o2 + MLA_KV_LORA
    zpad = lambda n: jnp.zeros((D_MODEL, n), F32)
    win = jnp.concatenate([e_w_in[:, :o3], zpad(MLA_NOPE), e_w_in[:, o3:],
                           zpad(HEAD_PAD - MLA_NOPE - MLA_ROPE)], axis=1).astype(BF16)
    pad_last = lambda w, n: jnp.pad(w, ((0, 0), (0, 0), (0, n - w.shape[2])))
    wuq = pad_last(e_w_uq, HEAD_PAD).reshape(MLA_Q_LORA, MLA_HEADS * HEAD_PAD).astype(BF16)
    wuk = pad_last(e_w_uk, HEAD_PAD).reshape(MLA_KV_LORA, MLA_HEADS * HEAD_PAD).astype(BF16)
    wuv = e_w_uv.reshape(MLA_KV_LORA, MLA_HEADS * MLA_V).astype(BF16)
    wk_t = jnp.pad(jnp.transpose(e_w_uk, (1, 2, 0)),
                   ((0, 0), (0, HEAD_PAD - MLA_NOPE), (0, 0))).astype(BF16)
    wv4 = e_w_uv.reshape(MLA_KV_LORA, MLA_HEADS // 2, 2, MLA_V)
    zv = jnp.zeros_like(wv4[:, :, 0])
    wv_big = jnp.stack([jnp.concatenate([wv4[:, :, 0], zv], axis=-1),
                        jnp.concatenate([zv, wv4[:, :, 1]], axis=-1)], axis=2)
    wv_big = wv_big.reshape(MLA_KV_LORA, MLA_HEADS * LANE).astype(BF16)
    gs = S5_STRIP // S5_STATE
    ns = S5_GROUPS // gs
    eye = jnp.eye(gs, dtype=F32)
    bd_in = lambda b: jnp.einsum('sgph,gk->sghkp', b.reshape(ns, gs, S5_STATE, S5_GROUP),
                                 eye).reshape(ns, gs * S5_GROUP, S5_STRIP)
    bd_out = lambda c: jnp.einsum('sghp,gk->sgpkh', c.reshape(ns, gs, S5_GROUP, S5_STATE),
                                  eye).reshape(ns, S5_STRIP, gs * S5_GROUP)
    wb = jnp.concatenate([bd_in(e_s5_b_re), bd_in(e_s5_b_im)], axis=2).astype(BF16)
    wcr = bd_out(e_s5_c_re).astype(BF16)
    wci = (-bd_out(e_s5_c_im)).astype(BF16)
    return win, wuq, wuk, wuv, wk_t, wv_big, wb, wcr, wci


def kernel(x_prompt, x_sample, page_table, state_s5_re, state_s5_im, cache_mla_ckv, cache_mla_krope,
           cache_swa_k, cache_swa_v, state_ffn_conv, norm_mix, norm_ffn, final_norm, e_w_in,
           e_s5_lam_re, e_s5_lam_im, e_s5_log_dt, e_s5_b_re, e_s5_b_im, e_s5_c_re, e_s5_c_im, e_s5_d,
           e_s5_w_glu, e_s5_b_glu, e_q_norm, e_w_uq, e_kv_norm, e_w_uk, e_w_uv, e_w_out, o_w_in,
           o_sinks, o_w_out, f_w_gate, f_w_up, f_conv_w, f_conv_b, f_w_down):
    n_p, t_p, _ = x_prompt.shape
    n_s, t_s, _ = x_sample.shape
    assert t_s == 1
    past_len = page_table.shape[1] * PAGE
    row2 = lambda v: v.reshape(1, -1)

    (win, wuq, wuk, wuv, wk_t, wv_big, wb, wcr, wci) = _prep_even(
        e_w_in[0], e_w_uq[0], e_w_uk[0], e_w_uv[0], e_s5_b_re[0], e_s5_b_im[0],
        e_s5_c_re[0], e_s5_c_im[0])
    lamr, lami = row2(e_s5_lam_re[0]), row2(e_s5_lam_im[0])
    ldt = row2(jnp.repeat(e_s5_log_dt[0], S5_STATE))
    s5_consts = (lamr, lami, ldt, wb, wcr, wci, row2(e_s5_d[0]), e_s5_w_glu[0].astype(BF16),
                 row2(e_s5_b_glu[0]))
    wout_s5 = e_w_out[0][:S5_WIDTH].astype(BF16)
    wout_mla = e_w_out[0][S5_WIDTH:].astype(BF16)
    owin = o_w_in[0].astype(BF16)
    owout = o_w_out[0].astype(BF16)
    n_layers = norm_ffn.shape[0]
    ffn_all = (norm_ffn.reshape(n_layers, 1, D_MODEL), f_w_gate.astype(BF16), f_w_up.astype(BF16),
               f_conv_w, f_conv_b.reshape(n_layers, 1, D_FF), f_w_down.astype(BF16))
    ffn = [(l,) + ffn_all for l in range(n_layers)]
    nm0, nm1 = row2(norm_mix[0]), row2(norm_mix[1])
    qnorm, kvnorm = row2(e_q_norm[0]), row2(e_kv_norm[0])
    fnorm = row2(final_norm)
    sinks = o_sinks[0]

    xp = x_prompt.reshape(n_p * t_p, D_MODEL)
    pos_p = np.arange(t_p)
    u, qt_p, ckv_p, kr_p, k_big, vt_p = _even_proj(
        xp, _mla_rope_tables(pos_p), nm0, win, qnorm, wuq.T, kvnorm, wuk, wuv.T, tm=TM_EVEN_PROJ,
        n_seq=n_p,
        with_kv=True)
    zeros_state = jnp.zeros((n_p, S5_NSTATE), F32)
    y_s5, p_hr, p_hi = _s5(u.reshape(n_p, t_p, S5_WIDTH), zeros_state, zeros_state, *s5_consts,
                           nb=n_p, tt=S5_TT, transpose_io=True)
    o_mla = _mla_attn(qt_p, k_big.reshape(n_p, t_p, -1), vt_p)
    x1, conv0 = _post(xp, [y_s5.reshape(n_p * t_p, -1), o_mla.reshape(n_p * t_p, -1)],
                      [wout_s5, wout_mla], *ffn[0], n_seq=n_p, tm=TM_POST)
    qt1, k1, vt1, k_tail, v_tail = _odd_proj_t(x1, _swa_rope_tables(pos_p), nm1, owin.T,
                                               tm=TM_ODD_PROJ,
                                               n_seq=n_p)
    kvw = SWA_KV_HEADS * SWA_HD
    o_swa = _swa_attn(sinks, qt1, k1.reshape(n_p, t_p, kvw), vt1)
    y_p, conv1 = _post(x1, [o_swa.reshape(n_p * t_p, -1)], [owout], *ffn[1], n_seq=n_p, tm=TM_POST,
                       final_norm=fnorm)

    y_prompt = y_p.reshape(n_p, t_p, D_MODEL)
    p_s5_re = p_hr.reshape(1, n_p, S5_GROUPS, S5_STATE)
    p_s5_im = p_hi.reshape(1, n_p, S5_GROUPS, S5_STATE)
    p_ckv = ckv_p.reshape(1, n_p, t_p, MLA_KV_LORA)
    p_krope = kr_p.reshape(1, n_p, t_p, MLA_ROPE)
    assert t_p >= WINDOW
    p_swa_k = k_tail.reshape(1, n_p, WINDOW, SWA_KV_HEADS, SWA_HD)
    p_swa_v = v_tail.reshape(1, n_p, WINDOW, SWA_KV_HEADS, SWA_HD)
    p_conv = jnp.stack([conv0[:, SUBLANE - 2:], conv1[:, SUBLANE - 2:]])

    xs = x_sample.reshape(n_s, D_MODEL)
    pos_s = np.full((n_s,), past_len)
    u_s, qs_big, ckv_s, kr_s = _even_proj(
        xs, _mla_rope_tables(pos_s), nm0, win, qnorm, wuq, kvnorm, wuk, wuv, tm=n_s, n_seq=n_s,
        with_kv=False)
    ys_s5, s_hr, s_hi = _s5(u_s.reshape(1, n_s, S5_WIDTH), state_s5_re[0].reshape(n_s, S5_NSTATE),
                            state_s5_im[0].reshape(n_s, S5_NSTATE), *s5_consts,
                            nb=n_s, tt=1, transpose_io=False)
    q_lat = _qlat(qs_big, wk_t).reshape(n_s, MLA_HEADS, MLA_KV_LORA)
    q_rope = qs_big.reshape(n_s, MLA_HEADS, HEAD_PAD)[:, :, MLA_NOPE:MLA_NOPE + MLA_ROPE]
    o_lat = _mla_decode(page_table, q_lat, q_rope, ckv_s.reshape(n_s, 1, MLA_KV_LORA),
                        kr_s.reshape(n_s, 1, MLA_ROPE), cache_mla_ckv,
                        jnp.swapaxes(cache_mla_krope, 2, 3))
    os_mla = _olat(o_lat.reshape(n_s, MLA_HEADS * MLA_KV_LORA), wv_big)
    xs1, g0 = _post(xs, [ys_s5.reshape(n_s, -1), os_mla], [wout_s5, wout_mla], *ffn[0],
                    n_seq=n_s, tm=n_s, conv_bufs=(state_ffn_conv[0, :, 0], state_ffn_conv[0, :, 1]))
    qs1, ks1, vs1 = _odd_proj(xs1, _swa_rope_tables(pos_s), nm1, owin, tm=n_s)
    expand = lambda a: jnp.repeat(a.reshape(n_s, SWA_KV_HEADS, SWA_HD), SWA_GQ, axis=1)
    os_swa = _swa_decode(sinks, qs1.reshape(n_s, SWA_HEADS, SWA_HD), expand(ks1), expand(vs1),
                         jnp.transpose(cache_swa_k[0], (0, 2, 3, 1)),
                         jnp.transpose(cache_swa_v[0], (0, 2, 3, 1)))
    ys, g1 = _post(xs1, [os_swa.reshape(n_s, -1)], [owout], *ffn[1], n_seq=n_s, tm=n_s,
                   conv_bufs=(state_ffn_conv[1, :, 0], state_ffn_conv[1, :, 1]), final_norm=fnorm)

    y_sample = ys.reshape(n_s, 1, D_MODEL)
    s_s5_re = s_hr.reshape(1, n_s, S5_GROUPS, S5_STATE)
    s_s5_im = s_hi.reshape(1, n_s, S5_GROUPS, S5_STATE)
    s_ckv = ckv_s.reshape(1, n_s, 1, MLA_KV_LORA)
    s_krope = kr_s.reshape(1, n_s, 1, MLA_ROPE)
    s_swa_k = ks1.reshape(1, n_s, 1, SWA_KV_HEADS, SWA_HD)
    s_swa_v = vs1.reshape(1, n_s, 1, SWA_KV_HEADS, SWA_HD)
    s_conv = jnp.stack([jnp.stack([state_ffn_conv[0, :, 1], g0], axis=1),
                        jnp.stack([state_ffn_conv[1, :, 1], g1], axis=1)])

    return (y_prompt, y_sample, p_s5_re, p_s5_im, p_ckv, p_krope, p_swa_k, p_swa_v, p_conv,
            s_s5_re, s_s5_im, s_ckv, s_krope, s_swa_k, s_swa_v, s_conv)
```

```python
import functools
import math

import numpy as np
import jax
import jax.numpy as jnp
from jax import lax
from jax.experimental import pallas as pl
from jax.experimental.pallas import tpu as pltpu

F32 = jnp.float32
BF16 = jnp.bfloat16

D_MODEL = 1024
S5_WIDTH = 512
S5_GROUPS = 32
S5_GROUP = 16
S5_STATE = 64
S5_NSTATE = S5_GROUPS * S5_STATE
MLA_HEADS = 16
MLA_NOPE = 64
MLA_ROPE = 32
MLA_V = 64
MLA_Q_LORA = 384
MLA_KV_LORA = 256
MLA_THETA = 10000.0
MLA_SCALE = 1.0 / math.sqrt(MLA_NOPE + MLA_ROPE)
SWA_HEADS = 16
SWA_KV_HEADS = 4
SWA_HD = 64
SWA_GQ = SWA_HEADS // SWA_KV_HEADS
WINDOW = 128
ROT_DIM = SWA_HD // 4
ROPE_THETA = 500000.0
SWA_SCALE = 1.0 / math.sqrt(SWA_HD)
D_FF = 2816
PAGE = 128
EPS = 1e-6
NEG = -1e30
LOG2E = math.log2(math.e)

LANE = 128
SUBLANE = 8
HEAD_PAD = LANE
Z_EVEN = S5_WIDTH + MLA_Q_LORA + MLA_KV_LORA + LANE
QK_ODD = (SWA_HEADS + SWA_KV_HEADS) * SWA_HD
FF_CHUNK = 256
VMEM_LIMIT = 48 * 1024 * 1024

TM_EVEN_PROJ = 512
TM_ODD_PROJ = 512
TM_POST = 512
MLA_TQ = 256
S5_TT = 128


def _cparams(sem):
    return pltpu.CompilerParams(dimension_semantics=sem, vmem_limit_bytes=VMEM_LIMIT)


def _rms(x, g):
    return x * lax.rsqrt(jnp.mean(x * x, axis=-1, keepdims=True) + EPS) * g


def _const_spec(shape):
    nd = len(shape)
    return pl.BlockSpec(shape, lambda *_: (0,) * nd)


def _single_spec(shape):
    nd = len(shape)
    return pl.BlockSpec(shape, lambda *_: (0,) * nd, pipeline_mode=pl.Buffered(1))


def _layer_spec(shape, layer):
    tail = (0,) * (len(shape) - 1)
    return pl.BlockSpec((None,) + tuple(shape[1:]), lambda *_: (layer,) + tail,
                        pipeline_mode=pl.Buffered(1))


def _mla_rope_tables(pos):
    half = MLA_ROPE // 2
    inv = MLA_THETA ** (-np.arange(half, dtype=np.float64) * 2.0 / MLA_ROPE)
    ang = np.asarray(pos, np.float64)[:, None] * inv[None, :]
    cos, sin = np.cos(ang), np.sin(ang)
    p = ang.shape[0]
    c = np.zeros((p, LANE)); sa = np.zeros((p, LANE)); sb = np.zeros((p, LANE))
    c[:, :MLA_NOPE] = 1.0
    c[:, MLA_NOPE:MLA_NOPE + half] = cos
    c[:, MLA_NOPE + half:MLA_NOPE + 2 * half] = cos
    sa[:, MLA_NOPE + half:MLA_NOPE + 2 * half] = sin
    sb[:, MLA_NOPE:MLA_NOPE + half] = -sin
    return tuple(jnp.asarray(t, F32) for t in (c, sa, sb))


def _swa_rope_tables(pos):
    half = ROT_DIM // 2
    inv = ROPE_THETA ** (-np.arange(half, dtype=np.float64) * 2.0 / ROT_DIM)
    ang = np.asarray(pos, np.float64)[:, None] * inv[None, :]
    cos, sin = np.cos(ang), np.sin(ang)
    p = ang.shape[0]
    c = np.ones((p, LANE)); sa = np.zeros((p, LANE)); sb = np.zeros((p, LANE))
    for o in (0, SWA_HD):
        c[:, o:o + half] = cos
        c[:, o + half:o + 2 * half] = cos
        sa[:, o + half:o + 2 * half] = sin
        sb[:, o:o + half] = -sin
    return tuple(jnp.asarray(t, F32) for t in (c, sa, sb))


def _rope_lanes(x, c, sa, sb, half):
    width = x.shape[1]
    reps = width // LANE
    if reps > 1:
        c = jnp.concatenate([c] * reps, axis=1)
        sa = jnp.concatenate([sa] * reps, axis=1)
        sb = jnp.concatenate([sb] * reps, axis=1)
    return x * c + pltpu.roll(x, half, 1) * sa + pltpu.roll(x, width - half, 1) * sb


def _rope_sublanes(xt, ct, sat, sbt, half):
    feats = xt.shape[0]
    reps = feats // LANE
    ct = jnp.concatenate([ct] * reps, axis=0)
    sat = jnp.concatenate([sat] * reps, axis=0)
    sbt = jnp.concatenate([sbt] * reps, axis=0)
    return xt * ct + pltpu.roll(xt, half, 0) * sat + pltpu.roll(xt, feats - half, 0) * sbt


def _even_proj_kernel(x_ref, c_ref, sa_ref, sb_ref, ct_ref, sat_ref, sbt_ref, nm_ref, win_ref, qn_ref,
                      wuq_ref, kvn_ref, wuk_ref, wuv_ref, u_ref, q_ref, ckv_ref, kr_ref, *kv_refs):
    h = _rms(x_ref[...], nm_ref[...]).astype(BF16)
    z = jnp.dot(h, win_ref[...], preferred_element_type=F32)
    u_ref[...] = z[:, :S5_WIDTH]
    c, sa, sb = c_ref[...], sa_ref[...], sb_ref[...]
    o1 = S5_WIDTH + MLA_Q_LORA
    o2 = o1 + MLA_KV_LORA
    qn = _rms(z[:, S5_WIDTH:o1], qn_ref[...]).astype(BF16)
    ckv = _rms(z[:, o1:o2], kvn_ref[...])
    ckv_ref[...] = ckv
    kr = _rope_lanes(z[:, o2:], c, sa, sb, MLA_ROPE // 2)
    kr_ref[...] = kr[:, MLA_NOPE:MLA_NOPE + MLA_ROPE]
    nt_dims = (((1,), (1,)), ((), ()))
    if kv_refs:
        k_ref, vt_ref = kv_refs
        qt = lax.dot_general(wuq_ref[...], qn, nt_dims, preferred_element_type=F32)
        qt = _rope_sublanes(qt, ct_ref[...], sat_ref[...], sbt_ref[...], MLA_ROPE // 2)
        tm = qt.shape[1]
        for j in range(tm // MLA_TQ):
            q_ref[0, j] = qt[:, j * MLA_TQ:(j + 1) * MLA_TQ].astype(BF16)
        ckvb = ckv.astype(BF16)
        k = jnp.dot(ckvb, wuk_ref[...], preferred_element_type=F32)
        k_ref[...] = (k + jnp.concatenate([kr] * MLA_HEADS, axis=1)).astype(BF16)
        vt = lax.dot_general(wuv_ref[...], ckvb, nt_dims, preferred_element_type=F32)
        for j in range(tm // MLA_TQ):
            vt_ref[0, j] = vt[:, j * MLA_TQ:(j + 1) * MLA_TQ].astype(BF16)
    else:
        q = jnp.dot(qn, wuq_ref[...], preferred_element_type=F32)
        q_ref[...] = _rope_lanes(q, c, sa, sb, MLA_ROPE // 2).astype(BF16)


def _even_proj(x, tabs, nm, win, qnorm, wuq, kvnorm, wuk, wuv, *, tm, n_seq, with_kv):
    rows = x.shape[0]
    nt = tabs[0].shape[0] // tm
    tabs_t = tuple(t.T for t in tabs)
    row = lambda w: pl.BlockSpec((tm, w), lambda i: (i, 0))
    tab = pl.BlockSpec((tm, LANE), lambda i: (i % nt, 0))
    tab_t = pl.BlockSpec((LANE, tm), lambda i: (0, i % nt))
    tq = min(tm, MLA_TQ)
    tblock = lambda w: pl.BlockSpec((1, tm // tq, w, tq), lambda i: (i // nt, i % nt, 0, 0))
    qw = MLA_HEADS * HEAD_PAD
    vw = MLA_HEADS * MLA_V
    if with_kv:
        q_shape, q_spec = jax.ShapeDtypeStruct((n_seq, nt * tm // tq, qw, tq), BF16), tblock(qw)
    else:
        q_shape, q_spec = jax.ShapeDtypeStruct((rows, qw), BF16), row(qw)
    out_shape = [jax.ShapeDtypeStruct((rows, S5_WIDTH), F32), q_shape,
                 jax.ShapeDtypeStruct((rows, MLA_KV_LORA), F32),
                 jax.ShapeDtypeStruct((rows, MLA_ROPE), F32)]
    out_specs = [row(S5_WIDTH), q_spec, row(MLA_KV_LORA), row(MLA_ROPE)]
    if with_kv:
        out_shape += [jax.ShapeDtypeStruct((rows, qw), BF16),
                      jax.ShapeDtypeStruct((n_seq, nt * tm // tq, vw, tq), BF16)]
        out_specs += [row(qw), tblock(vw)]
    return pl.pallas_call(
        _even_proj_kernel,
        grid=(rows // tm,),
        in_specs=[row(D_MODEL), tab, tab, tab, tab_t, tab_t, tab_t, _const_spec(nm.shape),
                  _const_spec(win.shape), _const_spec(qnorm.shape), _const_spec(wuq.shape),
                  _const_spec(kvnorm.shape), _const_spec(wuk.shape), _const_spec(wuv.shape)],
        out_specs=out_specs,
        out_shape=out_shape,
        compiler_params=_cparams(("arbitrary",)),
        name="even_proj",
    )(x, *tabs, *tabs_t, nm, win, qnorm, wuq, kvnorm, wuk, wuv)


S5_STRIP = 512


def _s5_kernel(u_ref, h0r_ref, h0i_ref, lamr_ref, lami_ref, ldt_ref, wb_ref, wcr_ref, wci_ref,
               d_ref, wglu_ref, bglu_ref, y_ref, hr_ref, hi_ref, ut_ref, xr_ref, xi_ref, yt_ref,
               *, nb, tt, transpose_io):
    @pl.when(pl.program_id(0) == 0)
    def _():
        xr_ref[0:nb, :] = h0r_ref[...]
        xi_ref[0:nb, :] = h0i_ref[...]

    if transpose_io:
        for t in range(tt):
            ut_ref[t * nb:(t + 1) * nb, :] = u_ref[:, t, :]
    else:
        ut_ref[...] = u_ref[0]
    ub = ut_ref[...].astype(BF16)

    lr = jnp.minimum(lamr_ref[...], -1e-4)
    li = lami_ref[...]
    dt = jnp.exp(ldt_ref[...])
    mag = jnp.exp(lr * dt)
    ar = mag * jnp.cos(li * dt)
    ai = mag * jnp.sin(li * dt)
    den = lr * lr + li * li
    zr = ((ar - 1.0) * lr + ai * li) / den
    zi = (ai * lr - (ar - 1.0) * li) / den

    ucols = S5_STRIP // S5_STATE * S5_GROUP
    ys = []
    for s in range(S5_NSTATE // S5_STRIP):
        cols = slice(s * S5_STRIP, (s + 1) * S5_STRIP)
        us = ub[:, s * ucols:(s + 1) * ucols]
        br = jnp.dot(us, wb_ref[s, :, :S5_STRIP], preferred_element_type=F32)
        bi = jnp.dot(us, wb_ref[s, :, S5_STRIP:], preferred_element_type=F32)
        zrs, zis = zr[:, cols], zi[:, cols]
        xr_ref[nb:, cols] = zrs * br - zis * bi
        xi_ref[nb:, cols] = zrs * bi + zis * br
        ars = jnp.broadcast_to(ar[:, cols], (nb, S5_STRIP))
        ais = jnp.broadcast_to(ai[:, cols], (nb, S5_STRIP))
        hr, hi = xr_ref[0:nb, cols], xi_ref[0:nb, cols]
        for t in range(tt):
            r = slice((t + 1) * nb, (t + 2) * nb)
            hr, hi = (ars * hr - ais * hi + xr_ref[r, cols],
                      ars * hi + ais * hr + xi_ref[r, cols])
            xr_ref[r, cols] = hr
            xi_ref[r, cols] = hi
        xr_ref[0:nb, cols] = hr
        xi_ref[0:nb, cols] = hi
        ys.append(jnp.dot(xr_ref[nb:, cols].astype(BF16), wcr_ref[s], preferred_element_type=F32)
                  + jnp.dot(xi_ref[nb:, cols].astype(BF16), wci_ref[s],
                            preferred_element_type=F32))

    hr_ref[...] = xr_ref[0:nb, :]
    hi_ref[...] = xi_ref[0:nb, :]
    y = jnp.concatenate(ys, axis=1) + d_ref[...] * ut_ref[...]
    y = jax.nn.gelu(y)
    gate = jnp.dot(y.astype(BF16), wglu_ref[...], preferred_element_type=F32) + bglu_ref[...]
    y = y * jax.nn.sigmoid(gate)
    if transpose_io:
        groups = S5_WIDTH // LANE
        for c in range(groups):
            yt_ref[c] = y[:, c * LANE:(c + 1) * LANE]
        for n in range(nb):
            y_ref[n] = jnp.concatenate(
                [yt_ref[c, pl.ds(n, tt, stride=nb), :] for c in range(groups)],
                axis=1).astype(y_ref.dtype)
    else:
        y_ref[0] = y.astype(y_ref.dtype)


def _s5(u, h0r, h0i, lamr, lami, ldt, wb, wcr, wci, d, wglu, bglu, *, nb, tt, transpose_io):
    t_total = u.shape[1] if transpose_io else u.shape[0]
    if transpose_io:
        u_spec = pl.BlockSpec((nb, tt, S5_WIDTH), lambda i: (0, i, 0))
        y_shape = (nb, t_total, S5_WIDTH)
    else:
        u_spec = pl.BlockSpec((tt, nb, S5_WIDTH), lambda i: (i, 0, 0))
        y_shape = (t_total, nb, S5_WIDTH)
    consts = (h0r, h0i, lamr, lami, ldt, wb, wcr, wci, d, wglu, bglu)
    kern = functools.partial(_s5_kernel, nb=nb, tt=tt, transpose_io=transpose_io)
    return pl.pallas_call(
        kern,
        grid=(t_total // tt,),
        in_specs=[u_spec] + [_const_spec(c.shape) for c in consts],
        out_specs=[u_spec, _const_spec((nb, S5_NSTATE)), _const_spec((nb, S5_NSTATE))],
        out_shape=[jax.ShapeDtypeStruct(y_shape, BF16),
                   jax.ShapeDtypeStruct((nb, S5_NSTATE), F32),
                   jax.ShapeDtypeStruct((nb, S5_NSTATE), F32)],
        scratch_shapes=[pltpu.VMEM((nb * tt, S5_WIDTH), F32),
                        pltpu.VMEM((nb * (tt + 1), S5_NSTATE), F32),
                        pltpu.VMEM((nb * (tt + 1), S5_NSTATE), F32),
                        pltpu.VMEM((S5_WIDTH // LANE, nb * tt, LANE), F32)],
        compiler_params=_cparams(("arbitrary",)),
        name="s5_mixer",
    )(u, *consts)


MLA_LOOKAHEAD = 4


def _mla_attn_kernel(qt_ref, k_ref, vt_ref, o_ref, m_ref, l_ref, acc_ref, *, tq):
    qi = pl.program_id(1)
    m_ref[...] = jnp.full(m_ref.shape, NEG, F32)
    l_ref[...] = jnp.zeros(l_ref.shape, F32)
    acc_ref[...] = jnp.zeros(acc_ref.shape, F32)

    def causal_mask(nblk):
        key = lax.broadcasted_iota(jnp.int32, (nblk * tq, tq), 0)
        qry = lax.broadcasted_iota(jnp.int32, (nblk * tq, tq), 1)
        return key <= qry + (nblk - 1) * tq

    def process(kb0, nblk, mask):
        keys = pl.ds(pl.multiple_of(kb0 * tq, tq), nblk * tq)

        def scores(h):
            hl = slice(h * HEAD_PAD, (h + 1) * HEAD_PAD)
            return jnp.dot(k_ref[0, keys, hl], qt_ref[0, 0, hl, :],
                           preferred_element_type=F32)

        pending = [scores(h) for h in range(MLA_LOOKAHEAD)]
        for h in range(MLA_HEADS):
            st = pending.pop(0) * (MLA_SCALE * LOG2E)
            if h + MLA_LOOKAHEAD < MLA_HEADS:
                pending.append(scores(h + MLA_LOOKAHEAD))
            if mask is not None:
                st = jnp.where(mask, st, NEG)
            j = h // 2
            vts = [vt_ref[0, kb0 + i, j * LANE:(j + 1) * LANE, :] for i in range(nblk)]
            vt = vts[0] if nblk == 1 else jnp.concatenate(vts, axis=1)
            m_old = m_ref[h]
            m_new = jnp.maximum(m_old, jnp.max(st, axis=0, keepdims=True))
            alpha = jnp.exp2(m_old - m_new)
            pt = jnp.exp2(st - m_new)
            l_ref[h] = alpha * l_ref[h] + jnp.sum(pt, axis=0, keepdims=True)
            m_ref[h] = m_new
            acc_ref[h] = alpha * acc_ref[h] + jnp.dot(vt, pt.astype(BF16),
                                                      preferred_element_type=F32)

    n_wide = qi // 2

    def wide_step(i, carry):
        process(i * 2, 2, None)
        return carry

    lax.fori_loop(0, n_wide, wide_step, 0)

    @pl.when(qi % 2 == 0)
    def _():
        process(qi, 1, causal_mask(1))

    @pl.when(qi % 2 == 1)
    def _():
        process(qi - 1, 2, causal_mask(2))

    half = lax.broadcasted_iota(jnp.int32, (LANE, tq), 0) < MLA_V
    for j in range(MLA_HEADS // 2):
        ot = jnp.where(half, acc_ref[2 * j] / l_ref[2 * j], acc_ref[2 * j + 1] / l_ref[2 * j + 1])
        o_ref[0, :, j * LANE:(j + 1) * LANE] = ot.T.astype(o_ref.dtype)


def _mla_attn(qt, k, vt):
    n, nblocks, qw, tq = qt.shape
    t = k.shape[1]
    kern = functools.partial(_mla_attn_kernel, tq=tq)
    return pl.pallas_call(
        kern,
        grid=(n, nblocks),
        in_specs=[pl.BlockSpec((1, 1, qw, tq), lambda b, i: (b, i, 0, 0)),
                  pl.BlockSpec((1, t, k.shape[2]), lambda b, i: (b, 0, 0)),
                  pl.BlockSpec((1, nblocks, vt.shape[2], tq), lambda b, i: (b, 0, 0, 0))],
        out_specs=pl.BlockSpec((1, tq, MLA_HEADS * MLA_V), lambda b, i: (b, i, 0)),
        out_shape=jax.ShapeDtypeStruct((n, t, MLA_HEADS * MLA_V), BF16),
        scratch_shapes=[pltpu.VMEM((MLA_HEADS, 1, tq), F32),
                        pltpu.VMEM((MLA_HEADS, 1, tq), F32),
                        pltpu.VMEM((MLA_HEADS, LANE, tq), F32)],
        compiler_params=_cparams(("arbitrary", "arbitrary")),
        name="mla_prompt_attn",
    )(qt, k, vt)


DEC_PAGES = 32
DEC_SUB = 4
DEC_AHEAD = 3
DEC_SLOTS = DEC_AHEAD + 1
DEC_KEYS = DEC_PAGES * PAGE


def _qlat_kernel(q_ref, wk_ref, o_ref):
    for h in range(MLA_HEADS):
        o_ref[:, h * MLA_KV_LORA:(h + 1) * MLA_KV_LORA] = jnp.dot(
            q_ref[:, h * HEAD_PAD:(h + 1) * HEAD_PAD], wk_ref[h], preferred_element_type=F32)


def _qlat(q_big, wk_t):
    rows = q_big.shape[0]
    return pl.pallas_call(
        _qlat_kernel,
        grid=(1,),
        in_specs=[_const_spec(q_big.shape), _const_spec(wk_t.shape)],
        out_specs=_const_spec((rows, MLA_HEADS * MLA_KV_LORA)),
        out_shape=jax.ShapeDtypeStruct((rows, MLA_HEADS * MLA_KV_LORA), F32),
        compiler_params=_cparams(("arbitrary",)),
        name="mla_q_absorb",
    )(q_big, wk_t)


def _olat_kernel(o_ref, wv_ref, out_ref):
    for j in range(MLA_HEADS // 2):
        acc = None
        for h in (2 * j, 2 * j + 1):
            part = jnp.dot(o_ref[:, h * MLA_KV_LORA:(h + 1) * MLA_KV_LORA].astype(BF16),
                           wv_ref[:, h * LANE:(h + 1) * LANE], preferred_element_type=F32)
            acc = part if acc is None else acc + part
        out_ref[:, j * LANE:(j + 1) * LANE] = acc.astype(out_ref.dtype)


def _olat(o_lat, wv_big):
    rows = o_lat.shape[0]
    return pl.pallas_call(
        _olat_kernel,
        grid=(1,),
        in_specs=[_const_spec(o_lat.shape), _const_spec(wv_big.shape)],
        out_specs=_const_spec((rows, MLA_HEADS * MLA_V)),
        out_shape=jax.ShapeDtypeStruct((rows, MLA_HEADS * MLA_V), BF16),
        compiler_params=_cparams(("arbitrary",)),
        name="mla_v_expand",
    )(o_lat, wv_big)


def _page_copies(cache_ckv, cache_kr, ckv_buf, kr_buf, sem, slot, page, p):
    rows = pl.ds(p * PAGE, PAGE)
    return (pltpu.make_async_copy(cache_ckv.at[0, page], ckv_buf.at[slot, rows], sem.at[0, slot]),
            pltpu.make_async_copy(cache_kr.at[0, page], kr_buf.at[slot, p], sem.at[1, slot]))


def _mla_decode_kernel(ptc_ref, ptn_ref, ql_ref, qr_ref, ckvn_ref, krn_ref, cache_ckv, cache_kr,
                       o_ref, ckv_buf, kr_buf, sem, *, n_seq, n_chunks):
    b = pl.program_id(0)
    g0 = b * n_chunks

    def start_page(page, slot, p):
        for cp in _page_copies(cache_ckv, cache_kr, ckv_buf, kr_buf, sem, slot, page, p):
            cp.start(priority=p % 2)

    def wait_chunk(slot):
        for p in range(DEC_PAGES):
            for cp in _page_copies(cache_ckv, cache_kr, ckv_buf, kr_buf, sem, slot, 0, p):
                cp.wait()

    @pl.when(b == 0)
    def _():
        for a in range(DEC_AHEAD):
            for p in range(DEC_PAGES):
                start_page(ptc_ref[0, 0, a * DEC_PAGES + p], a, p)

    ql = ql_ref[0].astype(BF16)
    qr = qr_ref[0].astype(BF16)

    def chunk_body(c, carry):
        m, l, acc = carry
        slot = lax.rem(g0 + c, DEC_SLOTS)
        ahead_slot = lax.rem(g0 + c + DEC_AHEAD, DEC_SLOTS)
        wait_chunk(slot)
        wraps = c + DEC_AHEAD >= n_chunks
        ahead_base = jnp.where(wraps, c + DEC_AHEAD - n_chunks, c + DEC_AHEAD) * DEC_PAGES

        def ahead_page(p):
            return jnp.where(wraps, ptn_ref[0, 0, ahead_base + p], ptc_ref[0, 0, ahead_base + p])

        kbs, scores = [], []
        for i in range(DEC_PAGES // DEC_SUB):
            kb = ckv_buf[slot, i * DEC_SUB * PAGE:(i + 1) * DEC_SUB * PAGE, :].astype(BF16)
            krt = jnp.concatenate([kr_buf[slot, i * DEC_SUB + p] for p in range(DEC_SUB)],
                                  axis=1).astype(BF16)
            kbs.append(kb)
            scores.append(
                (lax.dot_general(ql, kb, (((1,), (1,)), ((), ())), preferred_element_type=F32)
                 + jnp.dot(qr, krt, preferred_element_type=F32)) * MLA_SCALE)
            for p in range(i * DEC_SUB, (i + 1) * DEC_SUB):
                start_page(ahead_page(p), ahead_slot, p)
        for kb, s in zip(kbs, scores):
            m_new = jnp.maximum(m, jnp.max(s, axis=1, keepdims=True))
            alpha = jnp.exp(m - m_new)
            pr = jnp.exp(s - m_new)
            l = alpha * l + jnp.sum(pr, axis=1, keepdims=True)
            acc = alpha * acc + jnp.dot(pr.astype(BF16), kb, preferred_element_type=F32)
            m = m_new
        return m, l, acc

    init = (jnp.full((MLA_HEADS, 1), NEG, F32), jnp.zeros((MLA_HEADS, 1), F32),
            jnp.zeros((MLA_HEADS, MLA_KV_LORA), F32))
    m, l, acc = lax.fori_loop(0, n_chunks, chunk_body, init)

    @pl.when(b == n_seq - 1)
    def _():
        for a in range(DEC_AHEAD):
            wait_chunk((n_seq * n_chunks + a) % DEC_SLOTS)

    cn = ckvn_ref[0].astype(BF16).astype(F32)
    kn = krn_ref[0].astype(BF16).astype(F32)
    s_new = (jnp.sum(ql.astype(F32) * cn, axis=1, keepdims=True)
             + jnp.sum(qr.astype(F32) * kn, axis=1, keepdims=True)) * MLA_SCALE
    m_new = jnp.maximum(m, s_new)
    alpha = jnp.exp(m - m_new)
    p_new = jnp.exp(s_new - m_new)
    l = alpha * l + p_new
    acc = alpha * acc + p_new.astype(BF16).astype(F32) * cn
    o_ref[0] = acc / l


def _mla_decode(page_table, q_lat, q_rope, ckv_new, kr_new, cache_ckv, cache_kr):
    n_seq, n_pages = page_table.shape
    n_chunks = n_pages // DEC_PAGES
    assert n_pages % DEC_PAGES == 0 and n_chunks >= DEC_AHEAD
    pt = page_table.reshape(n_seq, 1, n_pages)
    smem_row = lambda f: pl.BlockSpec((1, 1, n_pages), f, memory_space=pltpu.SMEM)
    per_seq = lambda a: pl.BlockSpec((1,) + a.shape[1:], lambda b: (b, 0, 0))
    kern = functools.partial(_mla_decode_kernel, n_seq=n_seq, n_chunks=n_chunks)
    return pl.pallas_call(
        kern,
        grid=(n_seq,),
        in_specs=[smem_row(lambda b: (b, 0, 0)),
                  smem_row(lambda b: (jnp.minimum(b + 1, n_seq - 1), 0, 0)),
                  per_seq(q_lat), per_seq(q_rope), per_seq(ckv_new), per_seq(kr_new),
                  pl.BlockSpec(memory_space=pl.ANY), pl.BlockSpec(memory_space=pl.ANY)],
        out_specs=pl.BlockSpec((1, MLA_HEADS, MLA_KV_LORA), lambda b: (b, 0, 0)),
        out_shape=jax.ShapeDtypeStruct((n_seq, MLA_HEADS, MLA_KV_LORA), F32),
        scratch_shapes=[pltpu.VMEM((DEC_SLOTS, DEC_KEYS, MLA_KV_LORA), F32),
                        pltpu.VMEM((DEC_SLOTS, DEC_PAGES, MLA_ROPE, PAGE), F32),
                        pltpu.SemaphoreType.DMA((2, DEC_SLOTS))],
        compiler_params=_cparams(("arbitrary",)),
        name="mla_paged_decode",
    )(pt, pt, q_lat, q_rope, ckv_new, kr_new, cache_ckv, cache_kr)


def _odd_proj_kernel(x_ref, c_ref, sa_ref, sb_ref, nm_ref, win_ref, q_ref, k_ref, v_ref):
    h = _rms(x_ref[...], nm_ref[...]).astype(BF16)
    z = jnp.dot(h, win_ref[...], preferred_element_type=F32)
    qk = _rope_lanes(z[:, :QK_ODD], c_ref[...], sa_ref[...], sb_ref[...], ROT_DIM // 2)
    q_ref[...] = qk[:, :SWA_HEADS * SWA_HD].astype(BF16)
    k_ref[...] = qk[:, SWA_HEADS * SWA_HD:]
    v_ref[...] = z[:, QK_ODD:]


def _odd_proj(x, tabs, nm, win, *, tm):
    rows = x.shape[0]
    nt = tabs[0].shape[0] // tm
    row = lambda w: pl.BlockSpec((tm, w), lambda i: (i, 0))
    tab = pl.BlockSpec((tm, LANE), lambda i: (i % nt, 0))
    kvw = SWA_KV_HEADS * SWA_HD
    return pl.pallas_call(
        _odd_proj_kernel,
        grid=(rows // tm,),
        in_specs=[row(D_MODEL), tab, tab, tab, _const_spec(nm.shape), _const_spec(win.shape)],
        out_specs=[row(SWA_HEADS * SWA_HD), row(kvw), row(kvw)],
        out_shape=[jax.ShapeDtypeStruct((rows, SWA_HEADS * SWA_HD), BF16),
                   jax.ShapeDtypeStruct((rows, kvw), F32),
                   jax.ShapeDtypeStruct((rows, kvw), F32)],
        compiler_params=_cparams(("arbitrary",)),
        name="odd_proj",
    )(x, *tabs, nm, win)


def _odd_proj_t_kernel(x_ref, ct_ref, sat_ref, sbt_ref, nm_ref, wt_ref, qt_ref, k_ref, vt_ref,
                       ktail_ref, vtail_ref, *, tm):
    h = _rms(x_ref[...], nm_ref[...]).astype(BF16)
    zt = lax.dot_general(wt_ref[...], h, (((1,), (1,)), ((), ())),
                         preferred_element_type=F32)
    qkt = _rope_sublanes(zt[:QK_ODD], ct_ref[...], sat_ref[...], sbt_ref[...], ROT_DIM // 2)
    nq = SWA_HEADS * SWA_HD
    for i in range(tm // WINDOW):
        cols = slice(i * WINDOW, (i + 1) * WINDOW)
        qt_ref[0, i] = qkt[:nq, cols].astype(BF16)
        vt_ref[0, i] = zt[QK_ODD:, cols].astype(BF16)
    k = qkt[nq:].T
    k_ref[...] = k.astype(BF16)
    ktail_ref[0] = k[tm - WINDOW:]
    vtail_ref[0] = zt[QK_ODD:, tm - WINDOW:].T


def _odd_proj_t(x, tabs, nm, win_t, *, tm, n_seq):
    rows = x.shape[0]
    t = rows // n_seq
    nt = t // tm
    nb = tm // WINDOW
    kvw = SWA_KV_HEADS * SWA_HD
    qw = SWA_HEADS * SWA_HD
    tabs_t = tuple(a.T for a in tabs)
    row = lambda w: pl.BlockSpec((tm, w), lambda i: (i, 0))
    tab_t = pl.BlockSpec((LANE, tm), lambda i: (0, i % nt))
    tblock = lambda w: pl.BlockSpec((1, nb, w, WINDOW), lambda i: (i // nt, i % nt, 0, 0))
    tail = pl.BlockSpec((1, WINDOW, kvw), lambda i: (i // nt, 0, 0))
    return pl.pallas_call(
        functools.partial(_odd_proj_t_kernel, tm=tm),
        grid=(rows // tm,),
        in_specs=[row(D_MODEL), tab_t, tab_t, tab_t, _const_spec(nm.shape),
                  _const_spec(win_t.shape)],
        out_specs=[tblock(qw), row(kvw), tblock(kvw), tail, tail],
        out_shape=[jax.ShapeDtypeStruct((n_seq, t // WINDOW, qw, WINDOW), BF16),
                   jax.ShapeDtypeStruct((rows, kvw), BF16),
                   jax.ShapeDtypeStruct((n_seq, t // WINDOW, kvw, WINDOW), BF16),
                   jax.ShapeDtypeStruct((n_seq, WINDOW, kvw), F32),
                   jax.ShapeDtypeStruct((n_seq, WINDOW, kvw), F32)],
        compiler_params=_cparams(("arbitrary",)),
        name="odd_proj_t",
    )(x, *tabs_t, nm, win_t)


SWA_QBLOCKS = 4


def _swa_attn_kernel(sink_ref, qt_ref, kp_ref, kc_ref, vtp_ref, vtc_ref, o_ref):
    step = pl.program_id(1)
    cols = SWA_GQ * WINDOW
    key = lax.broadcasted_iota(jnp.int32, (2 * WINDOW, cols), 0)
    qry = jnp.bitwise_and(lax.broadcasted_iota(jnp.int32, (2 * WINDOW, cols), 1), WINDOW - 1)
    rel = qry + WINDOW - key
    band = (rel >= 0) & (rel <= WINDOW)
    lane = lax.broadcasted_iota(jnp.int32, (1, cols), 1)
    sinks = []
    for kh in range(SWA_KV_HEADS):
        row = jnp.full((1, cols), sink_ref[kh * SWA_GQ + SWA_GQ - 1], F32)
        for g in range(SWA_GQ - 2, -1, -1):
            row = jnp.where(lane < (g + 1) * WINDOW, sink_ref[kh * SWA_GQ + g], row)
        sinks.append(row * LOG2E)

    def block_scores(sub):
        if sub == 0:
            k_prev, vt_prev = kp_ref[0], vtp_ref[0, 0]
            valid = band & ((key >= WINDOW) | (step > 0))
        else:
            k_prev, vt_prev = kc_ref[0, (sub - 1) * WINDOW:sub * WINDOW], vtc_ref[0, sub - 1]
            valid = band
        kk = jnp.concatenate([k_prev, kc_ref[0, sub * WINDOW:(sub + 1) * WINDOW]], axis=0)
        vvt = jnp.concatenate([vt_prev, vtc_ref[0, sub]], axis=1)
        scores = []
        for kh in range(SWA_KV_HEADS):
            qg = jnp.concatenate(
                [qt_ref[0, sub, (kh * SWA_GQ + g) * SWA_HD:(kh * SWA_GQ + g + 1) * SWA_HD, :]
                 for g in range(SWA_GQ)], axis=1)
            scores.append(jnp.dot(kk[:, kh * SWA_HD:(kh + 1) * SWA_HD], qg,
                                  preferred_element_type=F32))
        return vvt, valid, scores

    ahead = block_scores(0)
    for sub in range(SWA_QBLOCKS):
        vvt, valid, scores = ahead
        if sub + 1 < SWA_QBLOCKS:
            ahead = block_scores(sub + 1)
        outs = []
        for kh in range(SWA_KV_HEADS):
            st = jnp.where(valid, scores[kh] * (SWA_SCALE * LOG2E), NEG)
            m = jnp.maximum(jnp.max(st, axis=0, keepdims=True), sinks[kh])
            p = jnp.exp2(st - m)
            l = jnp.sum(p, axis=0, keepdims=True) + jnp.exp2(sinks[kh] - m)
            ot = jnp.dot(vvt[kh * SWA_HD:(kh + 1) * SWA_HD, :], p.astype(BF16),
                         preferred_element_type=F32) / l
            outs += [ot[:, g * WINDOW:(g + 1) * WINDOW] for g in range(SWA_GQ)]
        o_ref[0, sub * WINDOW:(sub + 1) * WINDOW, :] = (
            jnp.concatenate(outs, axis=0).T.astype(o_ref.dtype))


def _swa_attn(sinks, qt, k, vt):
    n, nblocks, qw, _ = qt.shape
    t = k.shape[1]
    kvw = SWA_KV_HEADS * SWA_HD
    qb = SWA_QBLOCKS
    before = lambda i: jnp.maximum(i * qb - 1, 0)
    return pl.pallas_call(
        _swa_attn_kernel,
        grid=(n, nblocks // qb),
        in_specs=[pl.BlockSpec(memory_space=pltpu.SMEM),
                  pl.BlockSpec((1, qb, qw, WINDOW), lambda a, i: (a, i, 0, 0)),
                  pl.BlockSpec((1, WINDOW, kvw), lambda a, i: (a, before(i), 0)),
                  pl.BlockSpec((1, qb * WINDOW, kvw), lambda a, i: (a, i, 0)),
                  pl.BlockSpec((1, 1, kvw, WINDOW), lambda a, i: (a, before(i), 0, 0)),
                  pl.BlockSpec((1, qb, kvw, WINDOW), lambda a, i: (a, i, 0, 0))],
        out_specs=pl.BlockSpec((1, qb * WINDOW, qw), lambda a, i: (a, i, 0)),
        out_shape=jax.ShapeDtypeStruct((n, t, qw), BF16),
        compiler_params=_cparams(("arbitrary", "arbitrary")),
        name="swa_prompt_attn",
    )(sinks, qt, k, k, vt, vt)


SWA_DEC_SEQS = 8


def _swa_decode_kernel(sink_ref, q_ref, kn_ref, vn_ref, ck_ref, cv_ref, o_ref):
    q = q_ref[...]
    hgrp = lax.broadcasted_iota(jnp.int32, (1, SWA_HEADS, 1), 1) // SWA_GQ
    hidx = lax.broadcasted_iota(jnp.int32, (1, SWA_HEADS, 1), 1)
    sink = jnp.zeros((1, SWA_HEADS, 1), F32)
    for h in range(SWA_HEADS):
        sink = jnp.where(hidx == h, sink_ref[h], sink)
    s = jnp.zeros((q.shape[0], SWA_HEADS, WINDOW), F32)
    for kh in range(SWA_KV_HEADS):
        skh = jnp.einsum('nhd,ndj->nhj', q, ck_ref[:, kh].astype(BF16),
                         preferred_element_type=F32)
        s = jnp.where(hgrp == kh, skh, s)
    s = s * SWA_SCALE
    qf = q.astype(F32)
    kn = kn_ref[...].astype(BF16).astype(F32)
    vn = vn_ref[...].astype(BF16).astype(F32)
    s_new = jnp.sum(qf * kn, axis=2, keepdims=True) * SWA_SCALE
    m = jnp.maximum(jnp.maximum(jnp.max(s, axis=2, keepdims=True), s_new), sink)
    p = jnp.exp(s - m)
    p_new = jnp.exp(s_new - m)
    l = jnp.sum(p, axis=2, keepdims=True) + p_new + jnp.exp(sink - m)
    o = p_new.astype(BF16).astype(F32) * vn
    for kh in range(SWA_KV_HEADS):
        pk = jnp.where(hgrp == kh, p, 0.0).astype(BF16)
        o = o + jnp.einsum('nhj,ndj->nhd', pk, cv_ref[:, kh].astype(BF16),
                           preferred_element_type=F32)
    o_ref[...] = (o / l).astype(o_ref.dtype)


def _swa_decode(sinks, q, k_new, v_new, cache_k, cache_v):
    n = q.shape[0]
    nb = SWA_DEC_SEQS
    seq = pl.BlockSpec((nb, SWA_HEADS, SWA_HD), lambda i: (i, 0, 0))
    cache = pl.BlockSpec((nb, SWA_KV_HEADS, SWA_HD, WINDOW), lambda i: (i, 0, 0, 0))
    return pl.pallas_call(
        _swa_decode_kernel,
        grid=(n // nb,),
        in_specs=[pl.BlockSpec(memory_space=pltpu.SMEM), seq, seq, seq, cache, cache],
        out_specs=seq,
        out_shape=jax.ShapeDtypeStruct((n, SWA_HEADS, SWA_HD), BF16),
        compiler_params=_cparams(("arbitrary",)),
        name="swa_decode_attn",
    )(sinks, q, k_new, v_new, cache_k, cache_v)


def _post_kernel(*refs, n_mix, decode, final, tm):
    it = iter(refs)
    x_ref = next(it)
    mix_refs = [next(it) for _ in range(n_mix)]
    wo_refs = [next(it) for _ in range(n_mix)]
    nf_ref, wg_ref, wu_ref, cw_ref, cb_ref, wd_ref = (next(it) for _ in range(6))
    if decode:
        buf0_ref, buf1_ref = next(it), next(it)
    fn_ref = next(it) if final else None
    y_ref = next(it)
    g_ref = next(it)
    act_ref = next(it)
    carry_ref = None if decode else next(it)

    x1 = x_ref[...]
    for a_ref, w_ref in zip(mix_refs, wo_refs):
        x1 = x1 + jnp.dot(a_ref[...], w_ref[...], preferred_element_type=F32)
    h2 = _rms(x1, nf_ref[...]).astype(BF16)

    if not decode:
        @pl.when(pl.program_id(1) == 0)
        def _():
            carry_ref[...] = jnp.zeros_like(carry_ref)
        row = lax.broadcasted_iota(jnp.int32, (tm, FF_CHUNK), 0)

    for c in range(D_FF // FF_CHUNK):
        sl = slice(c * FF_CHUNK, (c + 1) * FF_CHUNK)
        g = jnp.dot(h2, wg_ref[:, sl], preferred_element_type=F32)
        u = jnp.dot(h2, wu_ref[:, sl], preferred_element_type=F32)
        if decode:
            gm2, gm1 = buf0_ref[:, sl], buf1_ref[:, sl]
            g_ref[:, sl] = g
        else:
            prev = carry_ref[:, sl]
            p6, p7 = prev[SUBLANE - 2:SUBLANE - 1, :], prev[SUBLANE - 1:SUBLANE, :]
            gm1 = jnp.where(row == 0, p7, pltpu.roll(g, 1, 0))
            gm2 = jnp.where(row == 0, p6, jnp.where(row == 1, p7, pltpu.roll(g, 2, 0)))
            last = g[tm - SUBLANE:tm, :]
            carry_ref[:, sl] = last
            g_ref[0, :, sl] = last
        cc = cb_ref[:, sl] + cw_ref[0:1, sl] * gm2 + cw_ref[1:2, sl] * gm1 + cw_ref[2:3, sl] * g
        act_ref[:, sl] = (jax.nn.gelu(cc) * u).astype(BF16)

    x2 = x1 + jnp.dot(act_ref[...], wd_ref[...], preferred_element_type=F32)
    y_ref[...] = _rms(x2, fn_ref[...]) if final else x2


def _post(x, mixes, wos, layer, nf, wg, wu, cw, cb, wd, *, n_seq, tm, conv_bufs=None,
          final_norm=None):
    rows = x.shape[0]
    decode = conv_bufs is not None
    nt = rows // n_seq // tm if not decode else rows // tm
    grid = (1, nt) if decode else (n_seq, nt)
    row = lambda w: pl.BlockSpec((tm, w), lambda a, i: (a * nt + i, 0))
    args = [x] + list(mixes) + list(wos) + [nf, wg, wu, cw, cb, wd]
    in_specs = ([row(D_MODEL)] + [row(m.shape[1]) for m in mixes]
                + [_single_spec(w.shape) for w in wos]
                + [_layer_spec(a.shape, layer) for a in (nf, wg, wu, cw, cb, wd)])
    if decode:
        args += list(conv_bufs)
        in_specs += [row(D_FF), row(D_FF)]
    if final_norm is not None:
        args.append(final_norm)
        in_specs.append(_const_spec(final_norm.shape))
    if decode:
        g_shape = jax.ShapeDtypeStruct((rows, D_FF), F32)
        g_spec = row(D_FF)
    else:
        g_shape = jax.ShapeDtypeStruct((n_seq, SUBLANE, D_FF), F32)
        g_spec = pl.BlockSpec((1, SUBLANE, D_FF), lambda a, i: (a, 0, 0))
    scratch = [pltpu.VMEM((tm, D_FF), BF16)]
    if not decode:
        scratch.append(pltpu.VMEM((SUBLANE, D_FF), F32))
    kern = functools.partial(_post_kernel, n_mix=len(mixes), decode=decode,
                             final=final_norm is not None, tm=tm)
    return pl.pallas_call(
        kern,
        grid=grid,
        in_specs=in_specs,
        out_specs=[row(D_MODEL), g_spec],
        out_shape=[jax.ShapeDtypeStruct((rows, D_MODEL), F32), g_shape],
        scratch_shapes=scratch,
        compiler_params=_cparams(("arbitrary", "arbitrary")),
        name="post_decode" if decode else "post_prompt",
    )(*args)


def _prep_even(e_w_in, e_w_uq, e_w_uk, e_w_uv, e_s5_b_re, e_s5_b_im, e_s5_c_re, e_s5_c_im):
    o1 = S5_WIDTH
    o2 = o1 + MLA_Q_LORA
    o3 = o2 + MLA_KV_LORA
    zpad = lambda n: jnp.zeros((D_MODEL, n), F32)
    win = jnp.concatenate([e_w_in[:, :o3], zpad(MLA_NOPE), e_w_in[:, o3:],
                           zpad(HEAD_PAD - MLA_NOPE - MLA_ROPE)], axis=1).astype(BF16)
    pad_last = lambda w, n: jnp.pad(w, ((0, 0), (0, 0), (0, n - w.shape[2])))
    wuq = pad_last(e_w_uq, HEAD_PAD).reshape(MLA_Q_LORA, MLA_HEADS * HEAD_PAD).astype(BF16)
    wuk = pad_last(e_w_uk, HEAD_PAD).reshape(MLA_KV_LORA, MLA_HEADS * HEAD_PAD).astype(BF16)
    wuv = e_w_uv.reshape(MLA_KV_LORA, MLA_HEADS * MLA_V).astype(BF16)
    wk_t = jnp.pad(jnp.transpose(e_w_uk, (1, 2, 0)),
                   ((0, 0), (0, HEAD_PAD - MLA_NOPE), (0, 0))).astype(BF16)
    wv4 = e_w_uv.reshape(MLA_KV_LORA, MLA_HEADS // 2, 2, MLA_V)
    zv = jnp.zeros_like(wv4[:, :, 0])
    wv_big = jnp.stack([jnp.concatenate([wv4[:, :, 0], zv], axis=-1),
                        jnp.concatenate([zv, wv4[:, :, 1]], axis=-1)], axis=2)
    wv_big = wv_big.reshape(MLA_KV_LORA, MLA_HEADS * LANE).astype(BF16)
    gs = S5_STRIP // S5_STATE
    ns = S5_GROUPS // gs
    eye = jnp.eye(gs, dtype=F32)
    bd_in = lambda b: jnp.einsum('sgph,gk->sghkp', b.reshape(ns, gs, S5_STATE, S5_GROUP),
                                 eye).reshape(ns, gs * S5_GROUP, S5_STRIP)
    bd_out = lambda c: jnp.einsum('sghp,gk->sgpkh', c.reshape(ns, gs, S5_GROUP, S5_STATE),
                                  eye).reshape(ns, S5_STRIP, gs * S5_GROUP)
    wb = jnp.concatenate([bd_in(e_s5_b_re), bd_in(e_s5_b_im)], axis=2).astype(BF16)
    wcr = bd_out(e_s5_c_re).astype(BF16)
    wci = (-bd_out(e_s5_c_im)).astype(BF16)
    return win, wuq, wuk, wuv, wk_t, wv_big, wb, wcr, wci


def kernel(x_prompt, x_sample, page_table, state_s5_re, state_s5_im, cache_mla_ckv, cache_mla_krope,
           cache_swa_k, cache_swa_v, state_ffn_conv, norm_mix, norm_ffn, final_norm, e_w_in,
           e_s5_lam_re, e_s5_lam_im, e_s5_log_dt, e_s5_b_re, e_s5_b_im, e_s5_c_re, e_s5_c_im, e_s5_d,
           e_s5_w_glu, e_s5_b_glu, e_q_norm, e_w_uq, e_kv_norm, e_w_uk, e_w_uv, e_w_out, o_w_in,
           o_sinks, o_w_out, f_w_gate, f_w_up, f_conv_w, f_conv_b, f_w_down):
    n_p, t_p, _ = x_prompt.shape
    n_s, t_s, _ = x_sample.shape
    assert t_s == 1
    past_len = page_table.shape[1] * PAGE
    row2 = lambda v: v.reshape(1, -1)

    (win, wuq, wuk, wuv, wk_t, wv_big, wb, wcr, wci) = _prep_even(
        e_w_in[0], e_w_uq[0], e_w_uk[0], e_w_uv[0], e_s5_b_re[0], e_s5_b_im[0],
        e_s5_c_re[0], e_s5_c_im[0])
    lamr, lami = row2(e_s5_lam_re[0]), row2(e_s5_lam_im[0])
    ldt = row2(jnp.repeat(e_s5_log_dt[0], S5_STATE))
    s5_consts = (lamr, lami, ldt, wb, wcr, wci, row2(e_s5_d[0]), e_s5_w_glu[0].astype(BF16),
                 row2(e_s5_b_glu[0]))
    wout_s5 = e_w_out[0][:S5_WIDTH].astype(BF16)
    wout_mla = e_w_out[0][S5_WIDTH:].astype(BF16)
    owin = o_w_in[0].astype(BF16)
    owout = o_w_out[0].astype(BF16)
    n_layers = norm_ffn.shape[0]
    ffn_all = (norm_ffn.reshape(n_layers, 1, D_MODEL), f_w_gate.astype(BF16), f_w_up.astype(BF16),
               f_conv_w, f_conv_b.reshape(n_layers, 1, D_FF), f_w_down.astype(BF16))
    ffn = [(l,) + ffn_all for l in range(n_layers)]
    nm0, nm1 = row2(norm_mix[0]), row2(norm_mix[1])
    qnorm, kvnorm = row2(e_q_norm[0]), row2(e_kv_norm[0])
    fnorm = row2(final_norm)
    sinks = o_sinks[0]

    xp = x_prompt.reshape(n_p * t_p, D_MODEL)
    pos_p = np.arange(t_p)
    u, qt_p, ckv_p, kr_p, k_big, vt_p = _even_proj(
        xp, _mla_rope_tables(pos_p), nm0, win, qnorm, wuq.T, kvnorm, wuk, wuv.T, tm=TM_EVEN_PROJ,
        n_seq=n_p,
        with_kv=True)
    zeros_state = jnp.zeros((n_p, S5_NSTATE), F32)
    y_s5, p_hr, p_hi = _s5(u.reshape(n_p, t_p, S5_WIDTH), zeros_state, zeros_state, *s5_consts,
                           nb=n_p, tt=S5_TT, transpose_io=True)
    o_mla = _mla_attn(qt_p, k_big.reshape(n_p, t_p, -1), vt_p)
    x1, conv0 = _post(xp, [y_s5.reshape(n_p * t_p, -1), o_mla.reshape(n_p * t_p, -1)],
                      [wout_s5, wout_mla], *ffn[0], n_seq=n_p, tm=TM_POST)
    qt1, k1, vt1, k_tail, v_tail = _odd_proj_t(x1, _swa_rope_tables(pos_p), nm1, owin.T,
                                               tm=TM_ODD_PROJ,
                                               n_seq=n_p)
    kvw = SWA_KV_HEADS * SWA_HD
    o_swa = _swa_attn(sinks, qt1, k1.reshape(n_p, t_p, kvw), vt1)
    y_p, conv1 = _post(x1, [o_swa.reshape(n_p * t_p, -1)], [owout], *ffn[1], n_seq=n_p, tm=TM_POST,
                       final_norm=fnorm)

    y_prompt = y_p.reshape(n_p, t_p, D_MODEL)
    p_s5_re = p_hr.reshape(1, n_p, S5_GROUPS, S5_STATE)
    p_s5_im = p_hi.reshape(1, n_p, S5_GROUPS, S5_STATE)
    p_ckv = ckv_p.reshape(1, n_p, t_p, MLA_KV_LORA)
    p_krope = kr_p.reshape(1, n_p, t_p, MLA_ROPE)
    assert t_p >= WINDOW
    p_swa_k = k_tail.reshape(1, n_p, WINDOW, SWA_KV_HEADS, SWA_HD)
    p_swa_v = v_tail.reshape(1, n_p, WINDOW, SWA_KV_HEADS, SWA_HD)
    p_conv = jnp.stack([conv0[:, SUBLANE - 2:], conv1[:, SUBLANE - 2:]])

    xs = x_sample.reshape(n_s, D_MODEL)
    pos_s = np.full((n_s,), past_len)
    u_s, qs_big, ckv_s, kr_s = _even_proj(
        xs, _mla_rope_tables(pos_s), nm0, win, qnorm, wuq, kvnorm, wuk, wuv, tm=n_s, n_seq=n_s,
        with_kv=False)
    ys_s5, s_hr, s_hi = _s5(u_s.reshape(1, n_s, S5_WIDTH), state_s5_re[0].reshape(n_s, S5_NSTATE),
                            state_s5_im[0].reshape(n_s, S5_NSTATE), *s5_consts,
                            nb=n_s, tt=1, transpose_io=False)
    q_lat = _qlat(qs_big, wk_t).reshape(n_s, MLA_HEADS, MLA_KV_LORA)
    q_rope = qs_big.reshape(n_s, MLA_HEADS, HEAD_PAD)[:, :, MLA_NOPE:MLA_NOPE + MLA_ROPE]
    o_lat = _mla_decode(page_table, q_lat, q_rope, ckv_s.reshape(n_s, 1, MLA_KV_LORA),
                        kr_s.reshape(n_s, 1, MLA_ROPE), cache_mla_ckv,
                        jnp.swapaxes(cache_mla_krope, 2, 3))
    os_mla = _olat(o_lat.reshape(n_s, MLA_HEADS * MLA_KV_LORA), wv_big)
    xs1, g0 = _post(xs, [ys_s5.reshape(n_s, -1), os_mla], [wout_s5, wout_mla], *ffn[0],
                    n_seq=n_s, tm=n_s, conv_bufs=(state_ffn_conv[0, :, 0], state_ffn_conv[0, :, 1]))
    qs1, ks1, vs1 = _odd_proj(xs1, _swa_rope_tables(pos_s), nm1, owin, tm=n_s)
    expand = lambda a: jnp.repeat(a.reshape(n_s, SWA_KV_HEADS, SWA_HD), SWA_GQ, axis=1)
    os_swa = _swa_decode(sinks, qs1.reshape(n_s, SWA_HEADS, SWA_HD), expand(ks1), expand(vs1),
                         jnp.transpose(cache_swa_k[0], (0, 2, 3, 1)),
                         jnp.transpose(cache_swa_v[0], (0, 2, 3, 1)))
    ys, g1 = _post(xs1, [os_swa.reshape(n_s, -1)], [owout], *ffn[1], n_seq=n_s, tm=n_s,
                   conv_bufs=(state_ffn_conv[1, :, 0], state_ffn_conv[1, :, 1]), final_norm=fnorm)

    y_sample = ys.reshape(n_s, 1, D_MODEL)
    s_s5_re = s_hr.reshape(1, n_s, S5_GROUPS, S5_STATE)
    s_s5_im = s_hi.reshape(1, n_s, S5_GROUPS, S5_STATE)
    s_ckv = ckv_s.reshape(1, n_s, 1, MLA_KV_LORA)
    s_krope = kr_s.reshape(1, n_s, 1, MLA_ROPE)
    s_swa_k = ks1.reshape(1, n_s, 1, SWA_KV_HEADS, SWA_HD)
    s_swa_v = vs1.reshape(1, n_s, 1, SWA_KV_HEADS, SWA_HD)
    s_conv = jnp.stack([jnp.stack([state_ffn_conv[0, :, 1], g0], axis=1),
                        jnp.stack([state_ffn_conv[1, :, 1], g1], axis=1)])

    return (y_prompt, y_sample, p_s5_re, p_s5_im, p_ckv, p_krope, p_swa_k, p_swa_v, p_conv,
            s_s5_re, s_s5_im, s_ckv, s_krope, s_swa_k, s_swa_v, s_conv)
```
